```python
import jax, jax.numpy as jnp
from jax import lax
import numpy as np

D_MODEL = 1024
BATCH = 8
SEQ = 2048
DEPTH = 2
DEC_BATCH = 32
DEC_SEQ = 8
PAST_LEN = 16384
PAGE_SIZE = 128

MLA_HEADS = 8
QK_NOPE = 64
QK_ROPE = 32
V_HEAD = 64
Q_LORA = 384
KV_LORA = 256
D_MLA = MLA_HEADS * V_HEAD
ROPE_THETA = 10000.0
ATTN_SCALE = (QK_NOPE + QK_ROPE) ** -0.5
Q_BLOCK = 128
D_CONV = 256
CONV_W = 3
R_HEADS = 4
R_HEAD = 64
D_R = R_HEADS * R_HEAD
W_LORA = 64
A_LORA = 64
SHIFT_W = 3 * D_R + W_LORA + A_LORA
GN_EPS = 64e-5
RMS_EPS = 1e-6
N_BRANCH = 3
SPLITS = (Q_LORA, KV_LORA, QK_ROPE, D_MLA,
          D_CONV, D_CONV, D_CONV, D_CONV,
          SHIFT_W, D_R,
          N_BRANCH * D_MODEL)
PROJ_W = Q_LORA + KV_LORA + QK_ROPE + D_MLA + 4 * D_CONV + SHIFT_W + D_R + N_BRANCH * D_MODEL

kernel_name = 'hybrid_mla_shortconv_rwkv7_step'


def rms_norm(x, g):
    xf = x.astype(jnp.float32)
    y = xf * lax.rsqrt(jnp.mean(xf * xf, axis=-1, keepdims=True) + RMS_EPS)
    return (y * g.astype(jnp.float32)).astype(x.dtype)


def split_cols(proj):
    out, idx = [], 0
    for s in SPLITS:
        out.append(proj[..., idx:idx + s])
        idx += s
    return out


def rope(x, pos):
    half = QK_ROPE // 2
    inv = ROPE_THETA ** (-jnp.arange(half, dtype=jnp.float32) / half)
    ang = pos.astype(jnp.float32)[:, None] * inv[None, :]
    cos = jnp.cos(ang)[None, :, None, :]
    sin = jnp.sin(ang)[None, :, None, :]
    xf = x.astype(jnp.float32)
    x1, x2 = xf[..., :half], xf[..., half:]
    return jnp.concatenate([x1 * cos - x2 * sin, x2 * cos + x1 * sin], axis=-1).astype(x.dtype)


def mla_attend(q_lat, q_pe, ckv, kpe, q_pos, k_pos):
    s = (jnp.einsum('bqhc,bkc->bhqk', q_lat, ckv).astype(jnp.float32)
         + jnp.einsum('bqhr,bkr->bhqk', q_pe, kpe).astype(jnp.float32)) * ATTN_SCALE
    mask = k_pos[None, :] <= q_pos[:, None]
    s = jnp.where(mask[None, None], s, -jnp.inf)
    p = jax.nn.softmax(s, axis=-1).astype(ckv.dtype)
    return jnp.einsum('bhqk,bkc->bqhc', p, ckv)


def mla_attend_blocked(q_lat, q_pe, ckv, kpe, q_pos, k_pos):
    B, Q, H, C = q_lat.shape
    nb = Q // Q_BLOCK
    ql = q_lat.reshape(B, nb, Q_BLOCK, H, C).swapaxes(0, 1)
    qp = q_pe.reshape(B, nb, Q_BLOCK, H, QK_ROPE).swapaxes(0, 1)
    qpos = q_pos.reshape(nb, Q_BLOCK)

    def one_block(args):
        a, b, c = args
        return mla_attend(a, b, ckv, kpe, c, k_pos)

    out = lax.map(one_block, (ql, qp, qpos))
    return out.swapaxes(0, 1).reshape(B, Q, H, C)


def causal_conv(u, prev, w):
    full = jnp.concatenate([prev, u], axis=1)
    T = u.shape[1]
    y = full[:, 0:T] * w[0]
    for j in range(1, CONV_W):
        y = y + full[:, j:j + T] * w[j]
    return y, full[:, T:]


def token_shift(f, prev, mu):
    f_prev = jnp.concatenate([prev[:, None], f[:, :-1]], axis=1)
    return f + mu * (f_prev - f), f[:, -1]


def rwkv7_scan(r, w, k, v, kk, a, S0):
    def step(S, inp):
        r_t, w_t, k_t, v_t, kk_t, a_t = inp
        sa = jnp.einsum('bhij,bhj->bhi', S, -kk_t)
        S = (S * w_t[:, :, None, :] + sa[..., None] * (kk_t * a_t)[:, :, None, :]
             + v_t[..., None] * k_t[:, :, None, :])
        y = jnp.einsum('bhij,bhj->bhi', S, r_t)
        return S, y

    xs = tuple(jnp.moveaxis(t.astype(jnp.float32), 1, 0) for t in (r, w, k, v, kk, a))
    S, ys = lax.scan(step, S0.astype(jnp.float32), xs)
    return jnp.moveaxis(ys, 0, 1), S


def hybrid_layer(x, c, pos, ckv_past, kpe_past, conv_prev, shift_prev, S0, p):
    B, T, _ = x.shape
    dt = x.dtype
    mod = jax.nn.silu(c) @ p['w_ada'] + p['b_ada']
    shift, scale, gate = mod[:, :D_MODEL], mod[:, D_MODEL:2 * D_MODEL], mod[:, 2 * D_MODEL:]
    u = rms_norm(x, p['norm_g']) * (1.0 + scale[:, None]) + shift[:, None]
    proj = u @ p['w_in']
    q_a, kv_a, k_rope, z_mla, cb, cc, cx, z_conv, rw, z_rw, g_merge = split_cols(proj)

    c_q = rms_norm(q_a, p['q_norm_g'])
    q = (c_q @ p['w_q_b']).reshape(B, T, MLA_HEADS, QK_NOPE + QK_ROPE)
    q_nope = q[..., :QK_NOPE]
    q_pe = rope(q[..., QK_NOPE:], pos)
    ckv = rms_norm(kv_a, p['kv_norm_g'])
    kpe = rope(k_rope[:, :, None, :], pos)[:, :, 0]
    q_lat = jnp.einsum('bthn,chn->bthc', q_nope, p['w_uk'])
    if ckv_past is None:
        o_lat = mla_attend_blocked(q_lat, q_pe, ckv, kpe, pos, pos)
    else:
        P = ckv_past.shape[1]
        keys_ckv = jnp.concatenate([ckv_past, ckv], axis=1)
        keys_kpe = jnp.concatenate([kpe_past, kpe], axis=1)
        k_pos = jnp.arange(P + T, dtype=jnp.int32)
        o_lat = mla_attend(q_lat, q_pe, keys_ckv, keys_kpe, pos, k_pos)
    o_mla = jnp.einsum('bthc,chv->bthv', o_lat, p['w_uv']).reshape(B, T, D_MLA)
    y_mla = (o_mla * jax.nn.silu(z_mla)) @ p['w_mla_out']

    conv_out, conv_state = causal_conv(cc * cx, conv_prev, p['conv_w'])
    y_conv = (jax.nn.silu(z_conv) * cb * conv_out) @ p['w_conv_out']

    rw_s, shift_state = token_shift(rw, shift_prev, p['rwkv_mu'])
    r = rw_s[..., :D_R]
    k = rw_s[..., D_R:2 * D_R]
    v = rw_s[..., 2 * D_R:3 * D_R]
    w_in = rw_s[..., 3 * D_R:3 * D_R + W_LORA]
    a_in = rw_s[..., 3 * D_R + W_LORA:]
    w_log = -jax.nn.softplus(-(p['rwkv_w0'] + jnp.tanh(w_in) @ p['rwkv_w2'])) - 0.5
    decay = jnp.exp(-jnp.exp(w_log.astype(jnp.float32)))
    a = jax.nn.sigmoid(p['rwkv_a0'] + a_in @ p['rwkv_a2'])
    hd = lambda t: t.reshape(B, T, R_HEADS, R_HEAD)
    kk = hd(k * p['rwkv_k_k']).astype(jnp.float32)
    kk = kk / jnp.maximum(jnp.sqrt(jnp.sum(kk * kk, axis=-1, keepdims=True)), 1e-12)
    k = k * (1.0 + (a - 1.0) * p['rwkv_k_a'])
    rh, kh, vh, ah, wh = hd(r), hd(k), hd(v), hd(a), hd(decay)
    y_r, S_new = rwkv7_scan(rh, wh, kh, vh, kk, ah, S0)
    mu = jnp.mean(y_r, axis=-1, keepdims=True)
    var = jnp.mean((y_r - mu) ** 2, axis=-1, keepdims=True)
    yn = ((y_r - mu) * lax.rsqrt(var + GN_EPS)).reshape(B, T, D_R)
    yn = yn * p['rwkv_gn_g'].astype(jnp.float32) + p['rwkv_gn_b'].astype(jnp.float32)
    bonus = (jnp.sum((rh * kh * p['rwkv_r_k']).astype(jnp.float32), axis=-1, keepdims=True)
             * vh.astype(jnp.float32)).reshape(B, T, D_R)
    o_rw = (yn + bonus).astype(dt)
    y_rw = (o_rw * jax.nn.silu(z_rw)) @ p['w_rwkv_out']

    gm = jax.nn.sigmoid(g_merge)
    merged = (gm[..., :D_MODEL] * y_mla + gm[..., D_MODEL:2 * D_MODEL] * y_conv
              + gm[..., 2 * D_MODEL:] * y_rw)
    x = x + gate[:, None] * (merged @ p['w_out'])
    return x, ckv, kpe, conv_state, shift_state, S_new.astype(dt)


def setup_inputs(seed: int = 0) -> dict:
    key = jax.random.key(seed)
    ks = iter(jax.random.split(key, 48))
    nrm = lambda shape, s=1.0: s * jax.random.normal(next(ks), shape, jnp.float32)
    n_pages = PAST_LEN // PAGE_SIZE
    n_used = DEC_BATCH * n_pages
    n_pool = n_used + n_used // 4
    page_table = jax.random.permutation(next(ks), n_pool)[:n_used].reshape(DEC_BATCH, n_pages).astype(jnp.int32)
    L = DEPTH
    return {
        'x_prompt': nrm((BATCH, SEQ, D_MODEL)),
        'x_sample': nrm((DEC_BATCH, DEC_SEQ, D_MODEL)),
        'cache_mla_ckv': nrm((L, n_pool, PAGE_SIZE, KV_LORA)),
        'cache_mla_kpe': nrm((L, n_pool, PAGE_SIZE, QK_ROPE)),
        'state_conv': nrm((L, DEC_BATCH, CONV_W - 1, D_CONV)),
        'state_rwkv_shift': nrm((L, DEC_BATCH, SHIFT_W)),
        'state_rwkv': nrm((L, DEC_BATCH, R_HEADS, R_HEAD, R_HEAD), 0.5),
        'page_table': page_table,
        'c_prompt': nrm((BATCH, D_MODEL)),
        'c_sample': nrm((DEC_BATCH, D_MODEL)),
        'norm_g': 1.0 + nrm((L, D_MODEL), 0.02),
        'w_ada': nrm((L, D_MODEL, 3 * D_MODEL), 0.3 * D_MODEL ** -0.5),
        'b_ada': nrm((L, 3 * D_MODEL), 0.02),
        'w_in': nrm((L, D_MODEL, PROJ_W), D_MODEL ** -0.5),
        'q_norm_g': 1.0 + nrm((L, Q_LORA), 0.02),
        'w_q_b': nrm((L, Q_LORA, MLA_HEADS * (QK_NOPE + QK_ROPE)), Q_LORA ** -0.5),
        'kv_norm_g': 1.0 + nrm((L, KV_LORA), 0.02),
        'w_uk': nrm((L, KV_LORA, MLA_HEADS, QK_NOPE), KV_LORA ** -0.5),
        'w_uv': nrm((L, KV_LORA, MLA_HEADS, V_HEAD), KV_LORA ** -0.5),
        'w_mla_out': nrm((L, D_MLA, D_MODEL), D_MLA ** -0.5),
        'conv_w': nrm((L, CONV_W, D_CONV), CONV_W ** -0.5),
        'w_conv_out': nrm((L, D_CONV, D_MODEL), D_CONV ** -0.5),
        'rwkv_mu': jax.random.uniform(next(ks), (L, SHIFT_W), jnp.float32),
        'rwkv_w0': nrm((L, D_R), 0.5),
        'rwkv_w2': nrm((L, W_LORA, D_R), W_LORA ** -0.5),
        'rwkv_a0': nrm((L, D_R), 0.5),
        'rwkv_a2': nrm((L, A_LORA, D_R), A_LORA ** -0.5),
        'rwkv_k_k': 0.85 + nrm((L, D_R), 0.05),
        'rwkv_k_a': 1.0 + nrm((L, D_R), 0.05),
        'rwkv_r_k': nrm((L, R_HEADS, R_HEAD), 0.1),
        'rwkv_gn_g': 1.0 + nrm((L, D_R), 0.02),
        'rwkv_gn_b': nrm((L, D_R), 0.02),
        'w_rwkv_out': nrm((L, D_R, D_MODEL), D_R ** -0.5),
        'w_out': nrm((L, D_MODEL, D_MODEL), D_MODEL ** -0.5),
        'final_norm_g': 1.0 + nrm((D_MODEL,), 0.02),
    }


def reference(x_prompt, x_sample, cache_mla_ckv, cache_mla_kpe, state_conv, state_rwkv_shift,
              state_rwkv, page_table, c_prompt, c_sample, norm_g, w_ada, b_ada, w_in, q_norm_g,
              w_q_b, kv_norm_g, w_uk, w_uv, w_mla_out, conv_w, w_conv_out, rwkv_mu, rwkv_w0,
              rwkv_w2, rwkv_a0, rwkv_a2, rwkv_k_k, rwkv_k_a, rwkv_r_k, rwkv_gn_g, rwkv_gn_b,
              w_rwkv_out, w_out, final_norm_g):
    bp, tp = x_prompt.shape[0], x_prompt.shape[1]
    bs, ts = x_sample.shape[0], x_sample.shape[1]
    past = page_table.shape[1] * PAGE_SIZE
    pos_p = jnp.arange(tp, dtype=jnp.int32)
    pos_s = past + jnp.arange(ts, dtype=jnp.int32)
    dt = x_prompt.dtype
    xp, xs = x_prompt, x_sample
    st_p = ([], [], [], [], [])
    st_s = ([], [], [], [], [])
    for l in range(DEPTH):
        p = {'norm_g': norm_g[l], 'w_ada': w_ada[l], 'b_ada': b_ada[l], 'w_in': w_in[l],
             'q_norm_g': q_norm_g[l], 'w_q_b': w_q_b[l], 'kv_norm_g': kv_norm_g[l],
             'w_uk': w_uk[l], 'w_uv': w_uv[l], 'w_mla_out': w_mla_out[l], 'conv_w': conv_w[l],
             'w_conv_out': w_conv_out[l], 'rwkv_mu': rwkv_mu[l], 'rwkv_w0': rwkv_w0[l],
             'rwkv_w2': rwkv_w2[l], 'rwkv_a0': rwkv_a0[l], 'rwkv_a2': rwkv_a2[l],
             'rwkv_k_k': rwkv_k_k[l], 'rwkv_k_a': rwkv_k_a[l], 'rwkv_r_k': rwkv_r_k[l],
             'rwkv_gn_g': rwkv_gn_g[l], 'rwkv_gn_b': rwkv_gn_b[l], 'w_rwkv_out': w_rwkv_out[l],
             'w_out': w_out[l]}
        xp, ckv, kpe, cst, sst, rst = hybrid_layer(
            xp, c_prompt, pos_p, None, None,
            jnp.zeros((bp, CONV_W - 1, D_CONV), dt), jnp.zeros((bp, SHIFT_W), dt),
            jnp.zeros((bp, R_HEADS, R_HEAD, R_HEAD), dt), p)
        for lst, val in zip(st_p, (ckv, kpe, cst, sst, rst)):
            lst.append(val)
        ckv_past = cache_mla_ckv[l][page_table].reshape(bs, past, KV_LORA)
        kpe_past = cache_mla_kpe[l][page_table].reshape(bs, past, QK_ROPE)
        xs, ckv, kpe, cst, sst, rst = hybrid_layer(
            xs, c_sample, pos_s, ckv_past, kpe_past,
            state_conv[l], state_rwkv_shift[l], state_rwkv[l], p)
        for lst, val in zip(st_s, (ckv, kpe, cst, sst, rst)):
            lst.append(val)
    y_prompt = rms_norm(xp, final_norm_g)
    y_sample = rms_norm(xs, final_norm_g)
    return (y_prompt, y_sample,
            jnp.stack(st_p[0]), jnp.stack(st_p[1]), jnp.stack(st_p[2]), jnp.stack(st_p[3]), jnp.stack(st_p[4]),
            jnp.stack(st_s[0]), jnp.stack(st_s[1]), jnp.stack(st_s[2]), jnp.stack(st_s[3]), jnp.stack(st_s[4]))
```

```python
import functools
import math

import jax
import jax.numpy as jnp
from jax import lax
from jax.experimental import pallas as pl
from jax.experimental.pallas import tpu as pltpu

F32 = jnp.float32
BF16 = jnp.bfloat16

D_MODEL = 1024
PAGE = 128
N_HEADS = 8
QK_NOPE = 64
QK_ROPE = 32
V_HEAD = 64
Q_LORA = 384
KV_LORA = 256
D_MLA = N_HEADS * V_HEAD
ROPE_THETA = 10000.0
ATTN_SCALE = (QK_NOPE + QK_ROPE) ** -0.5
D_CONV = 256
CONV_W = 3
R_HEADS = 4
R_HEAD = 64
D_R = R_HEADS * R_HEAD
W_LORA = 64
A_LORA = 64
SHIFT_W = 3 * D_R + W_LORA + A_LORA
GN_EPS = 64e-5
RMS_EPS = 1e-6
N_BRANCH = 3

LANE = 128
SUBLANE = 8
KR_W = LANE
SEG_W = (Q_LORA, KV_LORA, KR_W, D_MLA, 4 * D_CONV, SHIFT_W, D_R, N_BRANCH * D_MODEL)
SEG_OFF = tuple(sum(SEG_W[:i]) for i in range(len(SEG_W)))
PROJ_P = sum(SEG_W)
VMEM_LIMIT = 56 * 1024 * 1024
NEG_BIG = -1e30


def _cparams(sem):
    return pltpu.CompilerParams(dimension_semantics=sem, vmem_limit_bytes=VMEM_LIMIT)


def _dot(a, b):
    return jnp.dot(a, b, preferred_element_type=F32)


def _dot_nt(a, b):
    return lax.dot_general(a, b, (((1,), (1,)), ((), ())), preferred_element_type=F32)


def _sigmoid(x):
    return 1.0 / (1.0 + jnp.exp(-x))


def _silu(x):
    return x * _sigmoid(x)


def _softplus(x):
    return jnp.maximum(x, 0.0) + jnp.log1p(jnp.exp(-jnp.abs(x)))


def _seg_sum(x, bd):
    hi = x.astype(BF16)
    lo = (x - hi.astype(F32)).astype(BF16)
    return _dot(hi, bd) + _dot(lo, bd)


def _row_tiles(batch, seq, cap):
    if seq >= LANE:
        tt = min(seq, cap)
        assert seq % tt == 0
        return 1, tt
    assert seq % SUBLANE == 0
    return batch, seq


def _ada_kernel(c_ref, w_ref, b_ref, o_ref):
    s = _silu(c_ref[...])
    o_ref[...] = _dot(s.astype(BF16), w_ref[...]) + b_ref[...]


def _ada_mod(c, w_ada_bf, b_ada):
    bsz = c.shape[0]
    out = pl.pallas_call(
        _ada_kernel,
        out_shape=jax.ShapeDtypeStruct((bsz, 3 * D_MODEL), F32),
        compiler_params=pltpu.CompilerParams(vmem_limit_bytes=VMEM_LIMIT),
        name="ada_mod",
    )(c, w_ada_bf, b_ada.reshape(1, -1))
    return out.reshape(bsz, 1, 3 * D_MODEL)


def _in_proj_kernel(x_ref, mod_ref, g_ref, w_ref, *out_refs):
    nb, tt, d = x_ref.shape
    x = x_ref[...]
    ms = jnp.mean(x * x, axis=-1, keepdims=True)
    xn = x * lax.rsqrt(ms + RMS_EPS) * g_ref[...]
    shift = mod_ref[:, :, 0:D_MODEL]
    scale = mod_ref[:, :, D_MODEL:2 * D_MODEL]
    u = (xn * (1.0 + scale) + shift).reshape(nb * tt, d).astype(BF16)
    for o_ref, off, w in zip(out_refs, SEG_OFF, SEG_W):
        o_ref[...] = _dot(u, w_ref[:, off:off + w]).reshape(nb, tt, w)


def _in_proj(x, mod, norm_g, w_in_p):
    bsz, seq, d = x.shape
    nb, tt = _row_tiles(bsz, seq, 256)
    grid = (bsz // nb, seq // tt)
    row = lambda w: pl.BlockSpec((nb, tt, w), lambda b, t: (b, t, 0))
    return pl.pallas_call(
        _in_proj_kernel,
        grid=grid,
        in_specs=[row(d),
                  pl.BlockSpec((nb, 1, 3 * d), lambda b, t: (b, 0, 0)),
                  pl.BlockSpec((1, d), lambda b, t: (0, 0)),
                  pl.BlockSpec((d, PROJ_P), lambda b, t: (0, 0))],
        out_specs=[row(w) for w in SEG_W],
        out_shape=[jax.ShapeDtypeStruct((bsz, seq, w), F32) for w in SEG_W],
        compiler_params=_cparams(("parallel", "arbitrary")),
        name="in_proj",
    )(x, mod, norm_g.reshape(1, d), w_in_p)


def _mla_prep_kernel(qa_ref, kva_ref, kr_ref, cq_ref, sq_ref, ck_ref, sk_ref,
                     gq_ref, gkv_ref, wn_ref, wr_ref, wrs_ref, uk_ref,
                     qlat_ref, qpe_ref, ckv_ref, kpe_ref, ckvb_ref, kpeb_ref):
    nb, tt, _ = qa_ref.shape
    rows = nb * tt
    qa = qa_ref[...].reshape(rows, Q_LORA)
    cq = qa * lax.rsqrt(jnp.mean(qa * qa, axis=-1, keepdims=True) + RMS_EPS) * gq_ref[...]
    cqb = cq.astype(BF16)
    qn = _dot(cqb, wn_ref[...])
    cos_q = jnp.broadcast_to(cq_ref[...][None], (nb, tt, N_HEADS * QK_ROPE)).reshape(rows, -1)
    sin_q = jnp.broadcast_to(sq_ref[...][None], (nb, tt, N_HEADS * QK_ROPE)).reshape(rows, -1)
    qp = (_dot(cqb, wr_ref[...]) * cos_q + _dot(cqb, wrs_ref[...]) * sin_q) * ATTN_SCALE
    for h in range(N_HEADS):
        qn_h = qn[:, h * LANE:(h + 1) * LANE].astype(BF16)
        ql = _dot(qn_h, uk_ref[h]) * ATTN_SCALE
        qlat_ref[:, h] = ql.reshape(nb, tt, KV_LORA).astype(qlat_ref.dtype)
        qpe_ref[:, h] = qp[:, h * QK_ROPE:(h + 1) * QK_ROPE].reshape(nb, tt, QK_ROPE).astype(qpe_ref.dtype)
    kva = kva_ref[...]
    ckv = kva * lax.rsqrt(jnp.mean(kva * kva, axis=-1, keepdims=True) + RMS_EPS) * gkv_ref[...]
    ckv_ref[...] = ckv
    ckvb_ref[...] = ckv.astype(BF16)
    kr = kr_ref[...]
    kpe = kr[:, :, 0:QK_ROPE] * ck_ref[...][None] + kr[:, :, QK_ROPE:2 * QK_ROPE] * sk_ref[...][None]
    kpe_ref[...] = kpe
    kpeb_ref[...] = kpe.astype(BF16)


def _mla_prep(qa, kva, kr, tabs, q_norm_g, kv_norm_g, wn, wr, wrs, uk, q_dtype):
    bsz, seq, _ = qa.shape
    nb, tt = _row_tiles(bsz, seq, 256)
    grid = (bsz // nb, seq // tt)
    cos_q, sin_q, cos_k, sin_k = tabs
    row = lambda w: pl.BlockSpec((nb, tt, w), lambda b, t: (b, t, 0))
    tab = lambda w: pl.BlockSpec((tt, w), lambda b, t: (t, 0))
    full = lambda a: pl.BlockSpec(a.shape, lambda b, t: (0,) * a.ndim)
    hq = lambda w: pl.BlockSpec((nb, N_HEADS, tt, w), lambda b, t: (b, 0, t, 0))
    gq = q_norm_g.reshape(1, -1)
    gkv = kv_norm_g.reshape(1, -1)
    return pl.pallas_call(
        _mla_prep_kernel,
        grid=grid,
        in_specs=[row(Q_LORA), row(KV_LORA), row(KR_W),
                  tab(N_HEADS * QK_ROPE), tab(N_HEADS * QK_ROPE), tab(QK_ROPE), tab(QK_ROPE),
                  full(gq), full(gkv), full(wn), full(wr), full(wrs), full(uk)],
        out_specs=[hq(KV_LORA), hq(QK_ROPE), row(KV_LORA), row(QK_ROPE), row(KV_LORA), row(QK_ROPE)],
        out_shape=[jax.ShapeDtypeStruct((bsz, N_HEADS, seq, KV_LORA), q_dtype),
                   jax.ShapeDtypeStruct((bsz, N_HEADS, seq, QK_ROPE), q_dtype),
                   jax.ShapeDtypeStruct((bsz, seq, KV_LORA), F32),
                   jax.ShapeDtypeStruct((bsz, seq, QK_ROPE), F32),
                   jax.ShapeDtypeStruct((bsz, seq, KV_LORA), BF16),
                   jax.ShapeDtypeStruct((bsz, seq, QK_ROPE), BF16)],
        compiler_params=_cparams(("parallel", "arbitrary")),
        name="mla_prep",
    )(qa, kva, kr, cos_q, sin_q, cos_k, sin_k, gq, gkv, wn, wr, wrs, uk)


def _attn_prompt_kernel(ql_ref, qp_ref, ckv_ref, kpe_ref, wuv_ref, o_ref, m_scr, l_scr, acc_scr, *, tq, tk):
    qi = pl.program_id(1)
    rows = N_HEADS * tq
    ql = ql_ref[0].reshape(rows, KV_LORA)
    qp = qp_ref[0].reshape(rows, QK_ROPE)
    m_scr[...] = jnp.full(m_scr.shape, NEG_BIG, F32)
    l_scr[...] = jnp.zeros(l_scr.shape, F32)
    acc_scr[...] = jnp.zeros(acc_scr.shape, F32)
    q_pos = qi * tq + lax.broadcasted_iota(jnp.int32, (rows, tk), 0) % tq
    k_off = lax.broadcasted_iota(jnp.int32, (rows, tk), 1)
    n_kt = (qi * tq + tq + tk - 1) // tk

    def body(kt, carry):
        start = pl.multiple_of(kt * tk, tk)
        ck = ckv_ref[0, pl.ds(start, tk), :]
        kp = kpe_ref[0, pl.ds(start, tk), :]
        s = _dot_nt(ql, ck) + _dot_nt(qp, kp)
        s = jnp.where(k_off + start <= q_pos, s, NEG_BIG)
        m_old = m_scr[...]
        m_new = jnp.maximum(m_old, jnp.max(s, axis=-1, keepdims=True))
        alpha = jnp.exp(m_old - m_new)
        p = jnp.exp(s - m_new)
        l_scr[...] = alpha * l_scr[...] + jnp.sum(p, axis=-1, keepdims=True)
        acc_scr[...] = alpha * acc_scr[...] + _dot(p.astype(BF16), ck)
        m_scr[...] = m_new
        return carry

    lax.fori_loop(0, n_kt, body, 0)
    o_lat = acc_scr[...] / l_scr[...]
    out = jnp.zeros((tq, D_MLA), F32)
    for h in range(N_HEADS):
        out = out + _dot(o_lat[h * tq:(h + 1) * tq].astype(BF16), wuv_ref[h])
    o_ref[0] = out


def _attn_prompt(qlat, qpe, ckv_bf, kpe_bf, wuv_pad):
    bsz, _, seq, _ = qlat.shape
    tq = 128
    tk = min(512, seq)
    kern = functools.partial(_attn_prompt_kernel, tq=tq, tk=tk)
    rows = N_HEADS * tq
    return pl.pallas_call(
        kern,
        grid=(bsz, seq // tq),
        in_specs=[pl.BlockSpec((1, N_HEADS, tq, KV_LORA), lambda b, q: (b, 0, q, 0)),
                  pl.BlockSpec((1, N_HEADS, tq, QK_ROPE), lambda b, q: (b, 0, q, 0)),
                  pl.BlockSpec((1, seq, KV_LORA), lambda b, q: (b, 0, 0)),
                  pl.BlockSpec((1, seq, QK_ROPE), lambda b, q: (b, 0, 0)),
                  pl.BlockSpec(wuv_pad.shape, lambda b, q: (0, 0, 0))],
        out_specs=pl.BlockSpec((1, tq, D_MLA), lambda b, q: (b, q, 0)),
        out_shape=jax.ShapeDtypeStruct((bsz, seq, D_MLA), F32),
        scratch_shapes=[pltpu.VMEM((rows, 1), F32), pltpu.VMEM((rows, 1), F32),
                        pltpu.VMEM((rows, KV_LORA), F32)],
        compiler_params=_cparams(("parallel", "arbitrary")),
        name="attn_prompt",
    )(qlat, qpe, ckv_bf, kpe_bf, wuv_pad)


def _attn_sample_kernel(pt_ref, ql_ref, qp_ref, ckvn_ref, kpen_ref, wuv_ref, *rest, pp, ts):
    ckv_pages = rest[:pp]
    kpe_pages = rest[pp:2 * pp]
    o_ref = rest[2 * pp]
    m_scr, l_scr, acc_scr = rest[2 * pp + 1:]
    step = pl.program_id(1)
    rows = N_HEADS * ts
    ql = ql_ref[0].reshape(rows, KV_LORA).astype(BF16)
    qp = qp_ref[0].reshape(rows, QK_ROPE).astype(BF16)

    @pl.when(step == 0)
    def _():
        m_scr[...] = jnp.full(m_scr.shape, NEG_BIG, F32)
        l_scr[...] = jnp.zeros(l_scr.shape, F32)
        acc_scr[...] = jnp.zeros(acc_scr.shape, F32)

    def update(s, vals):
        m_old = m_scr[...]
        m_new = jnp.maximum(m_old, jnp.max(s, axis=-1, keepdims=True))
        alpha = jnp.exp(m_old - m_new)
        p = jnp.exp(s - m_new)
        l_scr[...] = alpha * l_scr[...] + jnp.sum(p, axis=-1, keepdims=True)
        acc_scr[...] = alpha * acc_scr[...] + _dot(p.astype(BF16), vals)
        m_scr[...] = m_new

    ck = jnp.concatenate([r[0, 0].astype(BF16) for r in ckv_pages], axis=0)
    kp = jnp.concatenate([r[0, 0].astype(BF16) for r in kpe_pages], axis=0)
    update(_dot_nt(ql, ck) + _dot_nt(qp, kp), ck)

    @pl.when(step == pl.num_programs(1) - 1)
    def _():
        ckn = ckvn_ref[0]
        kpn = kpen_ref[0]
        s = _dot_nt(ql, ckn) + _dot_nt(qp, kpn)
        t_q = lax.broadcasted_iota(jnp.int32, (rows, ts), 0) % ts
        t_k = lax.broadcasted_iota(jnp.int32, (rows, ts), 1)
        update(jnp.where(t_k <= t_q, s, NEG_BIG), ckn)
        o_lat = acc_scr[...] / l_scr[...]
        out = jnp.zeros((ts, D_MLA), F32)
        for h in range(N_HEADS):
            out = out + _dot(o_lat[h * ts:(h + 1) * ts].astype(BF16), wuv_ref[h])
        o_ref[0] = out


def _attn_sample(page_table, qlat, qpe, ckv_new_bf, kpe_new_bf, wuv_pad, cache_ckv, cache_kpe, layer):
    bsz, _, ts, _ = qlat.shape
    n_pages = page_table.shape[1]
    pp = 8
    while n_pages % pp:
        pp //= 2
    kern = functools.partial(_attn_sample_kernel, pp=pp, ts=ts)
    rows = N_HEADS * ts

    def page_spec(w, i):
        return pl.BlockSpec((1, 1, PAGE, w), lambda b, s, pt: (layer, pt[b, s * pp + i], 0, 0))

    grid_spec = pltpu.PrefetchScalarGridSpec(
        num_scalar_prefetch=1,
        grid=(bsz, n_pages // pp),
        in_specs=[pl.BlockSpec((1, N_HEADS, ts, KV_LORA), lambda b, s, pt: (b, 0, 0, 0)),
                  pl.BlockSpec((1, N_HEADS, ts, QK_ROPE), lambda b, s, pt: (b, 0, 0, 0)),
                  pl.BlockSpec((1, ts, KV_LORA), lambda b, s, pt: (b, 0, 0)),
                  pl.BlockSpec((1, ts, QK_ROPE), lambda b, s, pt: (b, 0, 0)),
                  pl.BlockSpec(wuv_pad.shape, lambda b, s, pt: (0, 0, 0))]
                 + [page_spec(KV_LORA, i) for i in range(pp)]
                 + [page_spec(QK_ROPE, i) for i in range(pp)],
        out_specs=pl.BlockSpec((1, ts, D_MLA), lambda b, s, pt: (b, 0, 0)),
        scratch_shapes=[pltpu.VMEM((rows, 1), F32), pltpu.VMEM((rows, 1), F32),
                        pltpu.VMEM((rows, KV_LORA), F32)],
    )
    return pl.pallas_call(
        kern,
        grid_spec=grid_spec,
        out_shape=jax.ShapeDtypeStruct((bsz, ts, D_MLA), F32),
        compiler_params=_cparams(("parallel", "arbitrary")),
        name="attn_sample",
    )(page_table, qlat, qpe, ckv_new_bf, kpe_new_bf, wuv_pad,
      *([cache_ckv] * pp), *([cache_kpe] * pp))


def _rwkv_prep_kernel(rw_ref, halo_ref, sprev_ref, mu_ref, w0_ref, w2_ref, a0_ref, a2_ref,
                      kk_ref, ka_ref, rk_ref, bd_ref,
                      r_out, w_out, k_out, v_out, kk_out, b_out, bonus_out, shift_out, buf):
    nb, tt, _ = rw_ref.shape
    ti = pl.program_id(1)
    rows = nb * tt

    @pl.when(ti == 0)
    def _():
        buf[:, SUBLANE - 1:SUBLANE, :] = sprev_ref[...]

    @pl.when(ti > 0)
    def _():
        buf[:, 0:SUBLANE, :] = halo_ref[...]

    rw = rw_ref[...]
    buf[:, SUBLANE:, :] = rw
    shift_out[...] = rw[:, tt - 1:tt, :]
    rw_prev = buf[:, SUBLANE - 1:SUBLANE - 1 + tt, :]
    rws = (rw + mu_ref[...] * (rw_prev - rw)).reshape(rows, SHIFT_W)
    r = rws[:, 0:D_R]
    k = rws[:, D_R:2 * D_R]
    v = rws[:, 2 * D_R:3 * D_R]
    wa = rws[:, 3 * D_R:]
    w_log = -_softplus(-(w0_ref[...] + _dot(jnp.tanh(wa).astype(BF16), w2_ref[...]))) - 0.5
    decay = jnp.exp(-jnp.exp(w_log))
    a = _sigmoid(a0_ref[...] + _dot(wa.astype(BF16), a2_ref[...]))
    bd = bd_ref[...]
    kk = k * kk_ref[...]
    kk = kk / jnp.maximum(jnp.sqrt(_seg_sum(kk * kk, bd)), 1e-12)
    k = k * (1.0 + (a - 1.0) * ka_ref[...])
    bonus = _seg_sum(r * k * rk_ref[...], bd) * v
    shp = (nb, tt, D_R)
    r_out[...] = r.reshape(shp)
    w_out[...] = decay.reshape(shp)
    k_out[...] = k.reshape(shp)
    v_out[...] = v.reshape(shp)
    kk_out[...] = kk.reshape(shp)
    b_out[...] = (kk * a).reshape(shp)
    bonus_out[...] = bonus.reshape(shp)


def _rwkv_prep(rw, shift_prev, mu, w0, w2p, a0, a2p, k_k, k_a, r_k, bd):
    bsz, seq, _ = rw.shape
    nb, tt = _row_tiles(bsz, seq, 256)
    grid = (bsz // nb, seq // tt)
    hb = tt // SUBLANE
    row = lambda w: pl.BlockSpec((nb, tt, w), lambda b, t: (b, t, 0))
    vec = lambda a: pl.BlockSpec(a.shape, lambda b, t: (0,) * a.ndim)
    vecs = [mu.reshape(1, -1), w0.reshape(1, -1), w2p, a0.reshape(1, -1), a2p,
            k_k.reshape(1, -1), k_a.reshape(1, -1), r_k.reshape(1, -1), bd]
    outs = pl.pallas_call(
        _rwkv_prep_kernel,
        grid=grid,
        in_specs=[row(SHIFT_W),
                  pl.BlockSpec((nb, SUBLANE, SHIFT_W), lambda b, t: (b, jnp.maximum(t * hb - 1, 0), 0)),
                  pl.BlockSpec((nb, 1, SHIFT_W), lambda b, t: (b, 0, 0))]
                 + [vec(a) for a in vecs],
        out_specs=[row(D_R)] * 7 + [pl.BlockSpec((nb, 1, SHIFT_W), lambda b, t: (b, 0, 0))],
        out_shape=[jax.ShapeDtypeStruct((bsz, seq, D_R), F32)] * 7
                  + [jax.ShapeDtypeStruct((bsz, 1, SHIFT_W), F32)],
        scratch_shapes=[pltpu.VMEM((nb, tt + SUBLANE, SHIFT_W), F32)],
        compiler_params=_cparams(("parallel", "arbitrary")),
        name="rwkv_prep",
    )(rw, rw, shift_prev.reshape(bsz, 1, SHIFT_W), *vecs)
    return outs


def _rwkv_scan_kernel(r_ref, w_ref, k_ref, v_ref, kk_ref, b_ref, s0_ref, bd_ref, eye_ref,
                      y_ref, sT_ref, s_scr):
    nb, tc, _ = r_ref.shape
    ci = pl.program_id(1)
    rows = nb * R_HEAD

    @pl.when(ci == 0)
    def _():
        s_scr[...] = s0_ref[...]

    bd = bd_ref[...]
    eye = eye_ref[...][None]

    def seg(x):
        return _seg_sum(x.reshape(rows, D_R), bd).reshape(nb, R_HEAD, D_R)

    def step(t, carry):
        row = lambda ref: ref[:, pl.ds(t, 1), :]
        s = s_scr[...]
        sa = -seg(s * row(kk_ref))
        v_col = seg(eye * row(v_ref))
        s_new = s * row(w_ref) + sa * row(b_ref) + v_col * row(k_ref)
        s_scr[...] = s_new
        y_col = seg(s_new * row(r_ref))
        y_ref[:, pl.ds(t, 1), :] = jnp.sum(y_col * eye, axis=1, keepdims=True)
        return carry

    lax.fori_loop(0, tc, step, 0)

    @pl.when(ci == pl.num_programs(1) - 1)
    def _():
        sT_ref[...] = s_scr[...]


def _rwkv_scan(r, w, k, v, kk, b, s0, bd, eye):
    bsz, seq, _ = r.shape
    nb = math.gcd(bsz, 8)
    tc = min(seq, 256)
    row = pl.BlockSpec((nb, tc, D_R), lambda bi, c: (bi, c, 0))
    st = pl.BlockSpec((nb, R_HEAD, D_R), lambda bi, c: (bi, 0, 0))
    return pl.pallas_call(
        _rwkv_scan_kernel,
        grid=(bsz // nb, seq // tc),
        in_specs=[row] * 6 + [st, pl.BlockSpec(bd.shape, lambda bi, c: (0, 0)),
                              pl.BlockSpec(eye.shape, lambda bi, c: (0, 0))],
        out_specs=[row, st],
        out_shape=[jax.ShapeDtypeStruct((bsz, seq, D_R), F32),
                   jax.ShapeDtypeStruct((bsz, R_HEAD, D_R), F32)],
        scratch_shapes=[pltpu.VMEM((nb, R_HEAD, D_R), F32)],
        compiler_params=_cparams(("parallel", "arbitrary")),
        name="rwkv_scan",
    )(r, w, k, v, kk, b, s0, bd, eye)


def _out_kernel(x_ref, mod_ref, omla_ref, zmla_ref, conv_ref, halo_ref, cprev_ref, zrw_ref,
                yr_ref, bonus_ref, gm_ref, wmla_ref, wconv_ref, wrw_ref, wout_ref, cw_ref,
                gng_ref, gnb_ref, bd_ref, fg_ref, xo_ref, yo_ref, cstate_ref, buf):
    nb, tt, d = x_ref.shape
    ti = pl.program_id(1)
    rows = nb * tt

    @pl.when(ti == 0)
    def _():
        buf[:, SUBLANE - 2:SUBLANE, :] = cprev_ref[...]

    @pl.when(ti > 0)
    def _():
        buf[:, 0:SUBLANE, :] = halo_ref[:, :, D_CONV:2 * D_CONV] * halo_ref[:, :, 2 * D_CONV:3 * D_CONV]

    cb = conv_ref[:, :, 0:D_CONV]
    buf[:, SUBLANE:, :] = conv_ref[:, :, D_CONV:2 * D_CONV] * conv_ref[:, :, 2 * D_CONV:3 * D_CONV]
    zc = conv_ref[:, :, 3 * D_CONV:]
    cstate_ref[...] = buf[:, tt + SUBLANE - 2:tt + SUBLANE, :]
    conv = (buf[:, SUBLANE - 2:SUBLANE - 2 + tt, :] * cw_ref[0:1, :]
            + buf[:, SUBLANE - 1:SUBLANE - 1 + tt, :] * cw_ref[1:2, :]
            + buf[:, SUBLANE:, :] * cw_ref[2:3, :])
    y_conv = _dot((_silu(zc) * cb * conv).reshape(rows, D_CONV).astype(BF16), wconv_ref[...])

    y_mla = _dot((omla_ref[...] * _silu(zmla_ref[...])).reshape(rows, D_MLA).astype(BF16), wmla_ref[...])

    bd = bd_ref[...]
    yr = yr_ref[...].reshape(rows, D_R)
    mu = _seg_sum(yr, bd) * (1.0 / R_HEAD)
    dy = yr - mu
    var = _seg_sum(dy * dy, bd) * (1.0 / R_HEAD)
    yn = dy * lax.rsqrt(var + GN_EPS) * gng_ref[...] + gnb_ref[...]
    o_rw = yn + bonus_ref[...].reshape(rows, D_R)
    y_rw = _dot((o_rw * _silu(zrw_ref[...].reshape(rows, D_R))).astype(BF16), wrw_ref[...])

    g = gm_ref[...].reshape(rows, N_BRANCH * d)
    merged = (_sigmoid(g[:, 0:d]) * y_mla + _sigmoid(g[:, d:2 * d]) * y_conv
              + _sigmoid(g[:, 2 * d:]) * y_rw)
    delta = _dot(merged.astype(BF16), wout_ref[...]).reshape(nb, tt, d)
    xo = x_ref[...] + mod_ref[:, :, 2 * d:] * delta
    xo_ref[...] = xo
    yo_ref[...] = xo * lax.rsqrt(jnp.mean(xo * xo, axis=-1, keepdims=True) + RMS_EPS) * fg_ref[...]


def _out_proj(x, mod, o_mla, z_mla, conv4, conv_prev, z_rw, y_r, bonus, g_merge,
              wmla, wconv, wrw, wout, conv_w, gn_g, gn_b, bd, final_g):
    bsz, seq, d = x.shape
    nb, tt = _row_tiles(bsz, seq, 256)
    grid = (bsz // nb, seq // tt)
    hb = tt // SUBLANE
    row = lambda w: pl.BlockSpec((nb, tt, w), lambda b, t: (b, t, 0))
    full = lambda a: pl.BlockSpec(a.shape, lambda b, t: (0,) * a.ndim)
    consts = [wmla, wconv, wrw, wout, conv_w, gn_g.reshape(1, -1), gn_b.reshape(1, -1), bd,
              final_g.reshape(1, -1)]
    return pl.pallas_call(
        _out_kernel,
        grid=grid,
        in_specs=[row(d), pl.BlockSpec((nb, 1, 3 * d), lambda b, t: (b, 0, 0)),
                  row(D_MLA), row(D_MLA), row(4 * D_CONV),
                  pl.BlockSpec((nb, SUBLANE, 4 * D_CONV), lambda b, t: (b, jnp.maximum(t * hb - 1, 0), 0)),
                  pl.BlockSpec((nb, CONV_W - 1, D_CONV), lambda b, t: (b, 0, 0)),
                  row(D_R), row(D_R), row(D_R), row(N_BRANCH * d)] + [full(a) for a in consts],
        out_specs=[row(d), row(d), pl.BlockSpec((nb, CONV_W - 1, D_CONV), lambda b, t: (b, 0, 0))],
        out_shape=[jax.ShapeDtypeStruct((bsz, seq, d), F32), jax.ShapeDtypeStruct((bsz, seq, d), F32),
                   jax.ShapeDtypeStruct((bsz, CONV_W - 1, D_CONV), F32)],
        scratch_shapes=[pltpu.VMEM((nb, tt + SUBLANE, D_CONV), F32)],
        compiler_params=_cparams(("parallel", "arbitrary")),
        name="out_proj",
    )(x, mod, o_mla, z_mla, conv4, conv4, conv_prev, z_rw, y_r, bonus, g_merge, *consts)


def _swap_halves(w):
    half = QK_ROPE // 2
    return jnp.concatenate([w[..., half:], w[..., :half]], axis=-1)


def _layer_params(l, w_ada, w_in, w_q_b, w_uk, w_uv, w_mla_out, w_conv_out, rwkv_w2, rwkv_a2,
                  w_rwkv_out, w_out):
    d = D_MODEL
    offs = [0]
    for s in (Q_LORA, KV_LORA, QK_ROPE, D_MLA, D_CONV, D_CONV, D_CONV, D_CONV, SHIFT_W, D_R, N_BRANCH * d):
        offs.append(offs[-1] + s)
    wi = w_in[l]
    k_rope = wi[:, offs[2]:offs[3]]
    kr = jnp.concatenate([k_rope, _swap_halves(k_rope), jnp.zeros((d, KR_W - 2 * QK_ROPE), F32)], axis=1)
    w_in_p = jnp.concatenate([wi[:, offs[0]:offs[2]], kr, wi[:, offs[3]:]], axis=1).astype(BF16)
    wq = w_q_b[l].reshape(Q_LORA, N_HEADS, QK_NOPE + QK_ROPE)
    wn = jnp.pad(wq[:, :, :QK_NOPE], ((0, 0), (0, 0), (0, LANE - QK_NOPE))).reshape(Q_LORA, N_HEADS * LANE)
    wr = wq[:, :, QK_NOPE:]
    wrs = _swap_halves(wr)
    uk = jnp.pad(jnp.transpose(w_uk[l], (1, 2, 0)), ((0, 0), (0, LANE - QK_NOPE), (0, 0)))
    wuv = jnp.transpose(w_uv[l], (1, 0, 2))
    wuv_pad = jnp.zeros((N_HEADS, KV_LORA, D_MLA), F32)
    for h in range(N_HEADS):
        wuv_pad = wuv_pad.at[h, :, h * V_HEAD:(h + 1) * V_HEAD].set(wuv[h])
    zeros = jnp.zeros((W_LORA, D_R), F32)
    return dict(
        w_ada=w_ada[l].astype(BF16), w_in_p=w_in_p,
        wn=wn.astype(BF16), wr=wr.reshape(Q_LORA, -1).astype(BF16), wrs=wrs.reshape(Q_LORA, -1).astype(BF16),
        uk=uk.astype(BF16), wuv_pad=wuv_pad.astype(BF16),
        wmla=w_mla_out[l].astype(BF16), wconv=w_conv_out[l].astype(BF16),
        wrw=w_rwkv_out[l].astype(BF16), wout=w_out[l].astype(BF16),
        w2p=jnp.concatenate([rwkv_w2[l], zeros], axis=0).astype(BF16),
        a2p=jnp.concatenate([zeros, rwkv_a2[l]], axis=0).astype(BF16),
    )


def _rope_tables(pos):
    half = QK_ROPE // 2
    inv = ROPE_THETA ** (-jnp.arange(half, dtype=F32) / half)
    ang = pos.astype(F32)[:, None] * inv[None, :]
    cos, sin = jnp.cos(ang), jnp.sin(ang)
    cos_k = jnp.concatenate([cos, cos], axis=1)
    sin_k = jnp.concatenate([-sin, sin], axis=1)
    return jnp.tile(cos_k, (1, N_HEADS)), jnp.tile(sin_k, (1, N_HEADS)), cos_k, sin_k


def _state_to_lanes(s):
    b = s.shape[0]
    return jnp.transpose(s, (0, 2, 1, 3)).reshape(b, R_HEAD, D_R)


def _state_from_lanes(s):
    b = s.shape[0]
    return jnp.transpose(s.reshape(b, R_HEAD, R_HEADS, R_HEAD), (0, 2, 1, 3))


def _layer(x, c_mod, tabs, p, vecs, conv_prev, shift_prev, s0, consts, final_g, attend, q_dtype):
    bd, eye = consts
    q_a, kv_a, kr, z_mla, conv4, rw, z_rw, g_merge = _in_proj(x, c_mod, vecs["norm_g"], p["w_in_p"])
    qlat, qpe, ckv, kpe, ckv_bf, kpe_bf = _mla_prep(
        q_a, kv_a, kr, tabs, vecs["q_norm_g"], vecs["kv_norm_g"], p["wn"], p["wr"], p["wrs"], p["uk"], q_dtype)
    o_mla = attend(qlat, qpe, ckv_bf, kpe_bf, p["wuv_pad"])
    r, w, k, v, kk, b, bonus, shift_state = _rwkv_prep(
        rw, shift_prev, vecs["mu"], vecs["w0"], p["w2p"], vecs["a0"], p["a2p"],
        vecs["k_k"], vecs["k_a"], vecs["r_k"], bd)
    y_r, s_new = _rwkv_scan(r, w, k, v, kk, b, s0, bd, eye)
    x_new, y_norm, conv_state = _out_proj(
        x, c_mod, o_mla, z_mla, conv4, conv_prev, z_rw, y_r, bonus, g_merge,
        p["wmla"], p["wconv"], p["wrw"], p["wout"], vecs["conv_w"], vecs["gn_g"], vecs["gn_b"], bd, final_g)
    return x_new, y_norm, ckv, kpe, conv_state, shift_state[:, 0], s_new


def kernel(x_prompt, x_sample, cache_mla_ckv, cache_mla_kpe, state_conv, state_rwkv_shift, state_rwkv, page_table, c_prompt, c_sample, norm_g, w_ada, b_ada, w_in, q_norm_g, w_q_b, kv_norm_g, w_uk, w_uv, w_mla_out, conv_w, w_conv_out, rwkv_mu, rwkv_w0, rwkv_w2, rwkv_a0, rwkv_a2, rwkv_k_k, rwkv_k_a, rwkv_r_k, rwkv_gn_g, rwkv_gn_b, w_rwkv_out, w_out, final_norm_g):
    depth = norm_g.shape[0]
    bp, tp, _ = x_prompt.shape
    bs, ts, _ = x_sample.shape
    past = page_table.shape[1] * PAGE
    tabs_p = _rope_tables(jnp.arange(tp, dtype=jnp.int32))
    tabs_s = _rope_tables(past + jnp.arange(ts, dtype=jnp.int32))
    seg = jnp.arange(D_R, dtype=jnp.int32) // R_HEAD
    bd = (seg[:, None] == seg[None, :]).astype(BF16)
    eye = (jnp.arange(R_HEAD, dtype=jnp.int32)[:, None] == (jnp.arange(D_R, dtype=jnp.int32) % R_HEAD)[None, :]).astype(F32)
    consts = (bd, eye)
    xp, xs = x_prompt, x_sample
    st_p = ([], [], [], [], [])
    st_s = ([], [], [], [], [])
    yp = ys = None
    for l in range(depth):
        p = _layer_params(l, w_ada, w_in, w_q_b, w_uk, w_uv, w_mla_out, w_conv_out, rwkv_w2, rwkv_a2,
                          w_rwkv_out, w_out)
        vecs = dict(norm_g=norm_g[l], q_norm_g=q_norm_g[l], kv_norm_g=kv_norm_g[l], mu=rwkv_mu[l],
                    w0=rwkv_w0[l], a0=rwkv_a0[l], k_k=rwkv_k_k[l], k_a=rwkv_k_a[l],
                    r_k=rwkv_r_k[l].reshape(-1), conv_w=conv_w[l], gn_g=rwkv_gn_g[l], gn_b=rwkv_gn_b[l])
        mod_p = _ada_mod(c_prompt, p["w_ada"], b_ada[l])
        mod_s = _ada_mod(c_sample, p["w_ada"], b_ada[l])
        xp, yp, ckv, kpe, cst, sst, rst = _layer(
            xp, mod_p, tabs_p, p, vecs,
            jnp.zeros((bp, CONV_W - 1, D_CONV), F32), jnp.zeros((bp, SHIFT_W), F32),
            jnp.zeros((bp, R_HEAD, D_R), F32), consts, final_norm_g, _attn_prompt, BF16)
        for lst, val in zip(st_p, (ckv, kpe, cst, sst, _state_from_lanes(rst))):
            lst.append(val)
        attend_s = functools.partial(_attn_sample_bound, page_table, cache_mla_ckv, cache_mla_kpe, l)
        xs, ys, ckv, kpe, cst, sst, rst = _layer(
            xs, mod_s, tabs_s, p, vecs, state_conv[l], state_rwkv_shift[l],
            _state_to_lanes(state_rwkv[l]), consts, final_norm_g, attend_s, F32)
        for lst, val in zip(st_s, (ckv, kpe, cst, sst, _state_from_lanes(rst))):
            lst.append(val)
    return (yp, ys,
            jnp.stack(st_p[0]), jnp.stack(st_p[1]), jnp.stack(st_p[2]), jnp.stack(st_p[3]), jnp.stack(st_p[4]),
            jnp.stack(st_s[0]), jnp.stack(st_s[1]), jnp.stack(st_s[2]), jnp.stack(st_s[3]), jnp.stack(st_s[4]))


def _attn_sample_bound(page_table, cache_ckv, cache_kpe, layer, qlat, qpe, ckv_bf, kpe_bf, wuv_pad):
    return _attn_sample(page_table, qlat, qpe, ckv_bf, kpe_bf, wuv_pad, cache_ckv, cache_kpe, layer)
```

```python
import functools
import math

import jax
import jax.numpy as jnp
from jax import lax
from jax.experimental import pallas as pl
from jax.experimental.pallas import tpu as pltpu

F32 = jnp.float32
BF16 = jnp.bfloat16

D_MODEL = 1024
PAGE = 128
N_HEADS = 8
QK_NOPE = 64
QK_ROPE = 32
V_HEAD = 64
Q_LORA = 384
KV_LORA = 256
D_MLA = N_HEADS * V_HEAD
ROPE_THETA = 10000.0
ATTN_SCALE = (QK_NOPE + QK_ROPE) ** -0.5
Q_SCALE = ATTN_SCALE * math.log2(math.e)
D_CONV = 256
CONV_W = 3
R_HEADS = 4
R_HEAD = 64
D_R = R_HEADS * R_HEAD
W_LORA = 64
A_LORA = 64
SHIFT_W = 3 * D_R + W_LORA + A_LORA
GN_EPS = 64e-5
RMS_EPS = 1e-6
N_BRANCH = 3

LANE = 128
SUBLANE = 8
KR_W = LANE
SEG_W = (Q_LORA, KV_LORA, KR_W, D_MLA, 4 * D_CONV, SHIFT_W, D_R, N_BRANCH * D_MODEL)
SEG_OFF = tuple(sum(SEG_W[:i]) for i in range(len(SEG_W)))
PROJ_P = sum(SEG_W)
VMEM_LIMIT = 56 * 1024 * 1024
NEG_BIG = -1e30
SCAN_UNROLL = 2


def _cparams(sem):
    return pltpu.CompilerParams(dimension_semantics=sem, vmem_limit_bytes=VMEM_LIMIT)


def _dot(a, b):
    return jnp.dot(a, b, preferred_element_type=F32)


def _dot_nt(a, b):
    return lax.dot_general(a, b, (((1,), (1,)), ((), ())), preferred_element_type=F32)


def _sigmoid(x):
    return 1.0 / (1.0 + jnp.exp(-x))


def _silu(x):
    return x * _sigmoid(x)


def _softplus(x):
    return jnp.maximum(x, 0.0) + jnp.log1p(jnp.exp(-jnp.abs(x)))


def _seg_sum(x, bd):
    hi = x.astype(BF16)
    lo = (x - hi.astype(F32)).astype(BF16)
    return _dot(hi, bd) + _dot(lo, bd)


def _row_tiles(batch, seq, cap):
    if seq >= LANE:
        tt = min(seq, cap)
        assert seq % tt == 0
        return 1, tt
    assert seq % SUBLANE == 0
    return batch, seq


def _ada_kernel(c_ref, w_ref, b_ref, o_ref):
    s = _silu(c_ref[...])
    o_ref[...] = _dot(s.astype(BF16), w_ref[...]) + b_ref[...]


def _ada_mod(c, w_ada_bf, b_ada):
    bsz = c.shape[0]
    out = pl.pallas_call(
        _ada_kernel,
        out_shape=jax.ShapeDtypeStruct((bsz, 3 * D_MODEL), F32),
        compiler_params=pltpu.CompilerParams(vmem_limit_bytes=VMEM_LIMIT),
        name="ada_mod",
    )(c, w_ada_bf, b_ada.reshape(1, -1))
    return out.reshape(bsz, 1, 3 * D_MODEL)


def _in_proj_kernel(x_ref, mod_ref, g_ref, w_ref, *out_refs):
    nb, tt, d = x_ref.shape
    x = x_ref[...]
    ms = jnp.mean(x * x, axis=-1, keepdims=True)
    xn = x * lax.rsqrt(ms + RMS_EPS) * g_ref[...]
    shift = mod_ref[:, :, 0:D_MODEL]
    scale = mod_ref[:, :, D_MODEL:2 * D_MODEL]
    u = (xn * (1.0 + scale) + shift).reshape(nb * tt, d).astype(BF16)
    for o_ref, off, w in zip(out_refs, SEG_OFF, SEG_W):
        o_ref[...] = _dot(u, w_ref[:, off:off + w]).reshape(nb, tt, w)


def _in_proj(x, mod, norm_g, w_in_p):
    bsz, seq, d = x.shape
    nb, tt = _row_tiles(bsz, seq, 256)
    grid = (bsz // nb, seq // tt)
    row = lambda w: pl.BlockSpec((nb, tt, w), lambda b, t: (b, t, 0))
    return pl.pallas_call(
        _in_proj_kernel,
        grid=grid,
        in_specs=[row(d),
                  pl.BlockSpec((nb, 1, 3 * d), lambda b, t: (b, 0, 0)),
                  pl.BlockSpec((1, d), lambda b, t: (0, 0)),
                  pl.BlockSpec((d, PROJ_P), lambda b, t: (0, 0))],
        out_specs=[row(w) for w in SEG_W],
        out_shape=[jax.ShapeDtypeStruct((bsz, seq, w), F32) for w in SEG_W],
        compiler_params=_cparams(("parallel", "arbitrary")),
        name="in_proj",
    )(x, mod, norm_g.reshape(1, d), w_in_p)


def _mla_prep_kernel(qa_ref, kva_ref, kr_ref, cq_ref, sq_ref, ck_ref, sk_ref,
                     gq_ref, gkv_ref, wn_ref, wr_ref, wrs_ref, uk_ref,
                     qlat_ref, qpe_ref, ckv_ref, kpe_ref, ckvb_ref, kpeb_ref):
    nb, tt, _ = qa_ref.shape
    rows = nb * tt
    qa = qa_ref[...].reshape(rows, Q_LORA)
    cq = qa * lax.rsqrt(jnp.mean(qa * qa, axis=-1, keepdims=True) + RMS_EPS) * gq_ref[...]
    cqb = cq.astype(BF16)
    qn = _dot(cqb, wn_ref[...])
    cos_q = jnp.broadcast_to(cq_ref[...][None], (nb, tt, N_HEADS * QK_ROPE)).reshape(rows, -1)
    sin_q = jnp.broadcast_to(sq_ref[...][None], (nb, tt, N_HEADS * QK_ROPE)).reshape(rows, -1)
    qp = (_dot(cqb, wr_ref[...]) * cos_q + _dot(cqb, wrs_ref[...]) * sin_q) * Q_SCALE
    for h in range(N_HEADS):
        qn_h = qn[:, h * LANE:(h + 1) * LANE].astype(BF16)
        ql = _dot(qn_h, uk_ref[h]) * Q_SCALE
        qlat_ref[:, h] = ql.reshape(nb, tt, KV_LORA).astype(qlat_ref.dtype)
        qpe_ref[:, h] = qp[:, h * QK_ROPE:(h + 1) * QK_ROPE].reshape(nb, tt, QK_ROPE).astype(qpe_ref.dtype)
    kva = kva_ref[...]
    ckv = kva * lax.rsqrt(jnp.mean(kva * kva, axis=-1, keepdims=True) + RMS_EPS) * gkv_ref[...]
    ckv_ref[...] = ckv
    ckvb_ref[...] = ckv.astype(BF16)
    kr = kr_ref[...]
    kpe = kr[:, :, 0:QK_ROPE] * ck_ref[...][None] + kr[:, :, QK_ROPE:2 * QK_ROPE] * sk_ref[...][None]
    kpe_ref[...] = kpe
    kpeb_ref[...] = kpe.astype(BF16)


def _mla_prep(qa, kva, kr, tabs, q_norm_g, kv_norm_g, wn, wr, wrs, uk, q_dtype):
    bsz, seq, _ = qa.shape
    nb, tt = _row_tiles(bsz, seq, 256)
    grid = (bsz // nb, seq // tt)
    cos_q, sin_q, cos_k, sin_k = tabs
    row = lambda w: pl.BlockSpec((nb, tt, w), lambda b, t: (b, t, 0))
    tab = lambda w: pl.BlockSpec((tt, w), lambda b, t: (t, 0))
    full = lambda a: pl.BlockSpec(a.shape, lambda b, t: (0,) * a.ndim)
    hq = lambda w: pl.BlockSpec((nb, N_HEADS, tt, w), lambda b, t: (b, 0, t, 0))
    gq = q_norm_g.reshape(1, -1)
    gkv = kv_norm_g.reshape(1, -1)
    return pl.pallas_call(
        _mla_prep_kernel,
        grid=grid,
        in_specs=[row(Q_LORA), row(KV_LORA), row(KR_W),
                  tab(N_HEADS * QK_ROPE), tab(N_HEADS * QK_ROPE), tab(QK_ROPE), tab(QK_ROPE),
                  full(gq), full(gkv), full(wn), full(wr), full(wrs), full(uk)],
        out_specs=[hq(KV_LORA), hq(QK_ROPE), row(KV_LORA), row(QK_ROPE), row(KV_LORA), row(QK_ROPE)],
        out_shape=[jax.ShapeDtypeStruct((bsz, N_HEADS, seq, KV_LORA), q_dtype),
                   jax.ShapeDtypeStruct((bsz, N_HEADS, seq, QK_ROPE), q_dtype),
                   jax.ShapeDtypeStruct((bsz, seq, KV_LORA), F32),
                   jax.ShapeDtypeStruct((bsz, seq, QK_ROPE), F32),
                   jax.ShapeDtypeStruct((bsz, seq, KV_LORA), BF16),
                   jax.ShapeDtypeStruct((bsz, seq, QK_ROPE), BF16)],
        compiler_params=_cparams(("parallel", "arbitrary")),
        name="mla_prep",
    )(qa, kva, kr, cos_q, sin_q, cos_k, sin_k, gq, gkv, wn, wr, wrs, uk)


def _value_up_proj(o_lat, wuv_ref, t):
    tiles = []
    for j in range(N_HEADS // 2):
        h0, h1 = 2 * j, 2 * j + 1
        tiles.append(_dot(o_lat[h0 * t:(h0 + 1) * t].astype(BF16), wuv_ref[h0])
                     + _dot(o_lat[h1 * t:(h1 + 1) * t].astype(BF16), wuv_ref[h1]))
    return jnp.concatenate(tiles, axis=1)


def _softmax_update(s, m_old, l_old):
    m_new = jnp.maximum(m_old, jnp.max(s, axis=-1, keepdims=True))
    alpha = jnp.exp2(m_old - m_new)
    p = jnp.exp2(s - jnp.tile(m_new, (1, s.shape[1] // LANE)))
    l_new = alpha * l_old + jnp.sum(p, axis=-1, keepdims=True)
    return p, m_new, l_new, alpha


def _attn_prompt_kernel(ql_ref, qp_ref, ckv_ref, kpe_ref, wuv_ref, o_ref,
                        m_scr, l_scr, acc_scr, s_scr, p_scr, *, tq, tk, rc):
    qi = pl.program_id(1)
    rows = N_HEADS * tq
    ql = ql_ref[0].reshape(rows, KV_LORA)
    qp = qp_ref[0].reshape(rows, QK_ROPE)
    m_scr[...] = jnp.full(m_scr.shape, NEG_BIG, F32)
    l_scr[...] = jnp.zeros(l_scr.shape, F32)
    acc_scr[...] = jnp.zeros(acc_scr.shape, F32)
    n_kt = (qi * tq + tq + tk - 1) // tk

    def scores(kt, slot):
        start = pl.multiple_of(kt * tk, tk)
        s_scr[slot] = (_dot_nt(ql, ckv_ref[0, pl.ds(start, tk), :])
                       + _dot_nt(qp, kpe_ref[0, pl.ds(start, tk), :]))

    def softmax(kt, slot, masked):
        for c in range(rows // rc):
            rs = slice(c * rc, (c + 1) * rc)
            s = s_scr[slot, rs, :]
            if masked:
                q_pos = qi * tq + (c * rc) % tq + lax.broadcasted_iota(jnp.int32, (rc, tk), 0)
                k_pos = kt * tk + lax.broadcasted_iota(jnp.int32, (rc, tk), 1)
                s = jnp.where(k_pos <= q_pos, s, NEG_BIG)
            p, m_new, l_new, alpha = _softmax_update(s, m_scr[rs, :], l_scr[rs, :])
            m_scr[rs, :] = m_new
            l_scr[rs, :] = l_new
            p_scr[rs, :] = p.astype(BF16)
            acc_scr[rs, :] = jnp.tile(alpha, (1, KV_LORA // LANE)) * acc_scr[rs, :]

    def weighted_values(kt):
        start = pl.multiple_of(kt * tk, tk)
        acc_scr[...] += _dot(p_scr[...], ckv_ref[0, pl.ds(start, tk), :])

    scores(0, 0)

    def body(kt, carry):
        slot = kt % 2
        softmax(kt, slot, False)
        scores(kt + 1, 1 - slot)
        weighted_values(kt)
        return carry

    lax.fori_loop(0, n_kt - 1, body, 0)
    last = n_kt - 1
    softmax(last, last % 2, True)
    weighted_values(last)
    o_lat = acc_scr[...] / jnp.tile(l_scr[...], (1, KV_LORA // LANE))
    o_ref[0] = _value_up_proj(o_lat, wuv_ref, tq)


def _attn_prompt(qlat, qpe, ckv_bf, kpe_bf, wuv_pad):
    bsz, _, seq, _ = qlat.shape
    tq = 128
    tk = min(256, seq)
    assert tk % tq == 0 and seq % tk == 0
    rc = 64
    kern = functools.partial(_attn_prompt_kernel, tq=tq, tk=tk, rc=rc)
    rows = N_HEADS * tq
    return pl.pallas_call(
        kern,
        grid=(bsz, seq // tq),
        in_specs=[pl.BlockSpec((1, N_HEADS, tq, KV_LORA), lambda b, q: (b, 0, q, 0)),
                  pl.BlockSpec((1, N_HEADS, tq, QK_ROPE), lambda b, q: (b, 0, q, 0)),
                  pl.BlockSpec((1, seq, KV_LORA), lambda b, q: (b, 0, 0)),
                  pl.BlockSpec((1, seq, QK_ROPE), lambda b, q: (b, 0, 0)),
                  pl.BlockSpec(wuv_pad.shape, lambda b, q: (0, 0, 0))],
        out_specs=pl.BlockSpec((1, tq, D_MLA), lambda b, q: (b, q, 0)),
        out_shape=jax.ShapeDtypeStruct((bsz, seq, D_MLA), F32),
        scratch_shapes=[pltpu.VMEM((rows, LANE), F32), pltpu.VMEM((rows, LANE), F32),
                        pltpu.VMEM((rows, KV_LORA), F32),
                        pltpu.VMEM((2, rows, tk), F32), pltpu.VMEM((rows, tk), BF16)],
        compiler_params=_cparams(("parallel", "arbitrary")),
        name="attn_prompt",
    )(qlat, qpe, ckv_bf, kpe_bf, wuv_pad)


def _attn_sample_kernel(pt_ref, ql_ref, qp_ref, ckvn_ref, kpen_ref, wuv_ref, *rest, pp, ts):
    ckv_pages = rest[:pp]
    kpe_pages = rest[pp:2 * pp]
    o_ref = rest[2 * pp]
    m_scr, l_scr, acc_scr = rest[2 * pp + 1:]
    step = pl.program_id(1)
    rows = N_HEADS * ts
    ql = ql_ref[0].reshape(rows, KV_LORA).astype(BF16)
    qp = qp_ref[0].reshape(rows, QK_ROPE).astype(BF16)

    @pl.when(step == 0)
    def _():
        m_scr[...] = jnp.full(m_scr.shape, NEG_BIG, F32)
        l_scr[...] = jnp.zeros(l_scr.shape, F32)
        acc_scr[...] = jnp.zeros(acc_scr.shape, F32)

    def update(s, vals):
        m_old = m_scr[...]
        m_new = jnp.maximum(m_old, jnp.max(s, axis=-1, keepdims=True))
        alpha = jnp.exp2(m_old - m_new)
        p = jnp.exp2(s - m_new)
        l_scr[...] = alpha * l_scr[...] + jnp.sum(p, axis=-1, keepdims=True)
        acc_scr[...] = alpha * acc_scr[...] + _dot(p.astype(BF16), vals)
        m_scr[...] = m_new

    ck = jnp.concatenate([r[0, 0].astype(BF16) for r in ckv_pages], axis=0)
    kp_t = jnp.concatenate([r[0, 0].astype(BF16) for r in kpe_pages], axis=1)
    update(_dot_nt(ql, ck) + _dot(qp, kp_t), ck)

    @pl.when(step == pl.num_programs(1) - 1)
    def _():
        ckn = ckvn_ref[0]
        kpn = kpen_ref[0]
        s = _dot_nt(ql, ckn) + _dot_nt(qp, kpn)
        t_q = lax.broadcasted_iota(jnp.int32, (rows, ts), 0) % ts
        t_k = lax.broadcasted_iota(jnp.int32, (rows, ts), 1)
        update(jnp.where(t_k <= t_q, s, NEG_BIG), ckn)
        o_lat = acc_scr[...] / l_scr[...]
        o_ref[0] = _value_up_proj(o_lat, wuv_ref, ts)


def _attn_sample(page_table, qlat, qpe, ckv_new_bf, kpe_new_bf, wuv_pad, cache_ckv, cache_kpe, layer):
    bsz, _, ts, _ = qlat.shape
    n_pages = page_table.shape[1]
    pp = 32
    while n_pages % pp:
        pp //= 2
    kern = functools.partial(_attn_sample_kernel, pp=pp, ts=ts)
    rows = N_HEADS * ts

    def page_spec(shape, i):
        return pl.BlockSpec((1, 1) + shape, lambda b, s, pt: (layer, pt[b, s * pp + i], 0, 0))

    cache_kpe_t = jnp.swapaxes(cache_kpe, 2, 3)

    grid_spec = pltpu.PrefetchScalarGridSpec(
        num_scalar_prefetch=1,
        grid=(bsz, n_pages // pp),
        in_specs=[pl.BlockSpec((1, N_HEADS, ts, KV_LORA), lambda b, s, pt: (b, 0, 0, 0)),
                  pl.BlockSpec((1, N_HEADS, ts, QK_ROPE), lambda b, s, pt: (b, 0, 0, 0)),
                  pl.BlockSpec((1, ts, KV_LORA), lambda b, s, pt: (b, 0, 0)),
                  pl.BlockSpec((1, ts, QK_ROPE), lambda b, s, pt: (b, 0, 0)),
                  pl.BlockSpec(wuv_pad.shape, lambda b, s, pt: (0, 0, 0))]
                 + [page_spec((PAGE, KV_LORA), i) for i in range(pp)]
                 + [page_spec((QK_ROPE, PAGE), i) for i in range(pp)],
        out_specs=pl.BlockSpec((1, ts, D_MLA), lambda b, s, pt: (b, 0, 0)),
        scratch_shapes=[pltpu.VMEM((rows, 1), F32), pltpu.VMEM((rows, 1), F32),
                        pltpu.VMEM((rows, KV_LORA), F32)],
    )
    return pl.pallas_call(
        kern,
        grid_spec=grid_spec,
        out_shape=jax.ShapeDtypeStruct((bsz, ts, D_MLA), F32),
        compiler_params=_cparams(("parallel", "arbitrary")),
        name="attn_sample",
    )(page_table, qlat, qpe, ckv_new_bf, kpe_new_bf, wuv_pad,
      *([cache_ckv] * pp), *([cache_kpe_t] * pp))


def _rwkv_prep_kernel(rw_ref, halo_ref, sprev_ref, mu_ref, w0_ref, w2_ref, a0_ref, a2_ref,
                      kk_ref, ka_ref, rk_ref, bd_ref,
                      q_out, w_out, k_out, v_out, kk_out, b_out, vkr_out, bonus_out, shift_out, buf):
    nb, tt, _ = rw_ref.shape
    ti = pl.program_id(1)
    rows = nb * tt

    @pl.when(ti == 0)
    def _():
        buf[:, SUBLANE - 1:SUBLANE, :] = sprev_ref[...]

    @pl.when(ti > 0)
    def _():
        buf[:, 0:SUBLANE, :] = halo_ref[...]

    rw = rw_ref[...]
    buf[:, SUBLANE:, :] = rw
    shift_out[...] = rw[:, tt - 1:tt, :]
    rw_prev = buf[:, SUBLANE - 1:SUBLANE - 1 + tt, :]
    rws = (rw + mu_ref[...] * (rw_prev - rw)).reshape(rows, SHIFT_W)
    r = rws[:, 0:D_R]
    k = rws[:, D_R:2 * D_R]
    v = rws[:, 2 * D_R:3 * D_R]
    wa = rws[:, 3 * D_R:]
    w_log = -_softplus(-(w0_ref[...] + _dot(jnp.tanh(wa).astype(BF16), w2_ref[...]))) - 0.5
    decay = jnp.exp(-jnp.exp(w_log))
    a = _sigmoid(a0_ref[...] + _dot(wa.astype(BF16), a2_ref[...]))
    bd = bd_ref[...]
    kk = k * kk_ref[...]
    kk = kk / jnp.maximum(jnp.sqrt(_seg_sum(kk * kk, bd)), 1e-12)
    k = k * (1.0 + (a - 1.0) * ka_ref[...])
    bonus = _seg_sum(r * k * rk_ref[...], bd) * v
    b = kk * a
    q = decay * r - kk * _seg_sum(b * r, bd)
    vkr = v * _seg_sum(k * r, bd)
    shp = (nb, tt, D_R)
    q_out[...] = q.reshape(shp)
    w_out[...] = decay.reshape(shp)
    k_out[...] = k.reshape(shp)
    v_out[...] = v.reshape(shp)
    kk_out[...] = kk.reshape(shp)
    b_out[...] = b.reshape(shp)
    vkr_out[...] = vkr.reshape(shp)
    bonus_out[...] = bonus.reshape(shp)


def _rwkv_prep(rw, shift_prev, mu, w0, w2p, a0, a2p, k_k, k_a, r_k, bd):
    bsz, seq, _ = rw.shape
    nb, tt = _row_tiles(bsz, seq, 256)
    grid = (bsz // nb, seq // tt)
    hb = tt // SUBLANE
    row = lambda w: pl.BlockSpec((nb, tt, w), lambda b, t: (b, t, 0))
    vec = lambda a: pl.BlockSpec(a.shape, lambda b, t: (0,) * a.ndim)
    vecs = [mu.reshape(1, -1), w0.reshape(1, -1), w2p, a0.reshape(1, -1), a2p,
            k_k.reshape(1, -1), k_a.reshape(1, -1), r_k.reshape(1, -1), bd]
    outs = pl.pallas_call(
        _rwkv_prep_kernel,
        grid=grid,
        in_specs=[row(SHIFT_W),
                  pl.BlockSpec((nb, SUBLANE, SHIFT_W), lambda b, t: (b, jnp.maximum(t * hb - 1, 0), 0)),
                  pl.BlockSpec((nb, 1, SHIFT_W), lambda b, t: (b, 0, 0))]
                 + [vec(a) for a in vecs],
        out_specs=[row(D_R)] * 8 + [pl.BlockSpec((nb, 1, SHIFT_W), lambda b, t: (b, 0, 0))],
        out_shape=[jax.ShapeDtypeStruct((bsz, seq, D_R), F32)] * 8
                  + [jax.ShapeDtypeStruct((bsz, 1, SHIFT_W), F32)],
        scratch_shapes=[pltpu.VMEM((nb, tt + SUBLANE, SHIFT_W), F32)],
        compiler_params=_cparams(("parallel", "arbitrary")),
        name="rwkv_prep",
    )(rw, rw, shift_prev.reshape(bsz, 1, SHIFT_W), *vecs)
    return outs


def _rwkv_scan_kernel(q_ref, w_ref, k_ref, v_ref, kk_ref, b_ref, vkr_ref, s0_ref, bd_ref, eye_ref,
                      y_ref, sT_ref, s_scr):
    nb, tc, _ = q_ref.shape
    ci = pl.program_id(1)
    rows = nb * R_HEAD

    @pl.when(ci == 0)
    def _():
        s_scr[...] = s0_ref[...]

    bd = bd_ref[...]
    eye = eye_ref[...][None]

    def seg(x):
        return _dot(x.reshape(rows, D_R).astype(BF16), bd).reshape(nb, R_HEAD, D_R)

    def step(t, carry):
        row = lambda ref: ref[:, pl.ds(t, 1), :]
        s = s_scr[...]
        sa = seg(s * row(kk_ref))
        y_col = seg(s * row(q_ref))
        v_col = seg(eye * row(v_ref))
        s_scr[...] = s * row(w_ref) - sa * row(b_ref) + v_col * row(k_ref)
        y_ref[:, pl.ds(t, 1), :] = jnp.sum(y_col * eye, axis=1, keepdims=True) + row(vkr_ref)
        return carry

    lax.fori_loop(0, tc, step, 0, unroll=SCAN_UNROLL)

    @pl.when(ci == pl.num_programs(1) - 1)
    def _():
        sT_ref[...] = s_scr[...]


def _rwkv_scan(q, w, k, v, kk, b, vkr, s0, bd, eye):
    bsz, seq, _ = q.shape
    nb = math.gcd(bsz, 8)
    tc = min(seq, 256)
    row = pl.BlockSpec((nb, tc, D_R), lambda bi, c: (bi, c, 0))
    st = pl.BlockSpec((nb, R_HEAD, D_R), lambda bi, c: (bi, 0, 0))
    return pl.pallas_call(
        _rwkv_scan_kernel,
        grid=(bsz // nb, seq // tc),
        in_specs=[row] * 7 + [st, pl.BlockSpec(bd.shape, lambda bi, c: (0, 0)),
                              pl.BlockSpec(eye.shape, lambda bi, c: (0, 0))],
        out_specs=[row, st],
        out_shape=[jax.ShapeDtypeStruct((bsz, seq, D_R), F32),
                   jax.ShapeDtypeStruct((bsz, R_HEAD, D_R), F32)],
        scratch_shapes=[pltpu.VMEM((nb, R_HEAD, D_R), F32)],
        compiler_params=_cparams(("parallel", "arbitrary")),
        name="rwkv_scan",
    )(q, w, k, v, kk, b, vkr, s0, bd, eye)


def _out_kernel(x_ref, mod_ref, omla_ref, zmla_ref, conv_ref, halo_ref, cprev_ref, zrw_ref,
                yr_ref, bonus_ref, gm_ref, wmla_ref, wconv_ref, wrw_ref, wout_ref, cw_ref,
                gng_ref, gnb_ref, bd_ref, fg_ref, xo_ref, yo_ref, cstate_ref, buf):
    nb, tt, d = x_ref.shape
    ti = pl.program_id(1)
    rows = nb * tt

    @pl.when(ti == 0)
    def _():
        buf[:, SUBLANE - 2:SUBLANE, :] = cprev_ref[...]

    @pl.when(ti > 0)
    def _():
        buf[:, 0:SUBLANE, :] = halo_ref[:, :, D_CONV:2 * D_CONV] * halo_ref[:, :, 2 * D_CONV:3 * D_CONV]

    cb = conv_ref[:, :, 0:D_CONV]
    buf[:, SUBLANE:, :] = conv_ref[:, :, D_CONV:2 * D_CONV] * conv_ref[:, :, 2 * D_CONV:3 * D_CONV]
    zc = conv_ref[:, :, 3 * D_CONV:]
    cstate_ref[...] = buf[:, tt + SUBLANE - 2:tt + SUBLANE, :]
    conv = (buf[:, SUBLANE - 2:SUBLANE - 2 + tt, :] * cw_ref[0:1, :]
            + buf[:, SUBLANE - 1:SUBLANE - 1 + tt, :] * cw_ref[1:2, :]
            + buf[:, SUBLANE:, :] * cw_ref[2:3, :])
    y_conv = _dot((_silu(zc) * cb * conv).reshape(rows, D_CONV).astype(BF16), wconv_ref[...])

    y_mla = _dot((omla_ref[...] * _silu(zmla_ref[...])).reshape(rows, D_MLA).astype(BF16), wmla_ref[...])

    bd = bd_ref[...]
    yr = yr_ref[...].reshape(rows, D_R)
    mu = _seg_sum(yr, bd) * (1.0 / R_HEAD)
    dy = yr - mu
    var = _seg_sum(dy * dy, bd) * (1.0 / R_HEAD)
    yn = dy * lax.rsqrt(var + GN_EPS) * gng_ref[...] + gnb_ref[...]
    o_rw = yn + bonus_ref[...].reshape(rows, D_R)
    y_rw = _dot((o_rw * _silu(zrw_ref[...].reshape(rows, D_R))).astype(BF16), wrw_ref[...])

    g = gm_ref[...].reshape(rows, N_BRANCH * d)
    merged = (_sigmoid(g[:, 0:d]) * y_mla + _sigmoid(g[:, d:2 * d]) * y_conv
              + _sigmoid(g[:, 2 * d:]) * y_rw)
    delta = _dot(merged.astype(BF16), wout_ref[...]).reshape(nb, tt, d)
    xo = x_ref[...] + mod_ref[:, :, 2 * d:] * delta
    xo_ref[...] = xo
    yo_ref[...] = xo * lax.rsqrt(jnp.mean(xo * xo, axis=-1, keepdims=True) + RMS_EPS) * fg_ref[...]


def _out_proj(x, mod, o_mla, z_mla, conv4, conv_prev, z_rw, y_r, bonus, g_merge,
              wmla, wconv, wrw, wout, conv_w, gn_g, gn_b, bd, final_g):
    bsz, seq, d = x.shape
    nb, tt = _row_tiles(bsz, seq, 256)
    grid = (bsz // nb, seq // tt)
    hb = tt // SUBLANE
    row = lambda w: pl.BlockSpec((nb, tt, w), lambda b, t: (b, t, 0))
    full = lambda a: pl.BlockSpec(a.shape, lambda b, t: (0,) * a.ndim)
    consts = [wmla, wconv, wrw, wout, conv_w, gn_g.reshape(1, -1), gn_b.reshape(1, -1), bd,
              final_g.reshape(1, -1)]
    return pl.pallas_call(
        _out_kernel,
        grid=grid,
        in_specs=[row(d), pl.BlockSpec((nb, 1, 3 * d), lambda b, t: (b, 0, 0)),
                  row(D_MLA), row(D_MLA), row(4 * D_CONV),
                  pl.BlockSpec((nb, SUBLANE, 4 * D_CONV), lambda b, t: (b, jnp.maximum(t * hb - 1, 0), 0)),
                  pl.BlockSpec((nb, CONV_W - 1, D_CONV), lambda b, t: (b, 0, 0)),
                  row(D_R), row(D_R), row(D_R), row(N_BRANCH * d)] + [full(a) for a in consts],
        out_specs=[row(d), row(d), pl.BlockSpec((nb, CONV_W - 1, D_CONV), lambda b, t: (b, 0, 0))],
        out_shape=[jax.ShapeDtypeStruct((bsz, seq, d), F32), jax.ShapeDtypeStruct((bsz, seq, d), F32),
                   jax.ShapeDtypeStruct((bsz, CONV_W - 1, D_CONV), F32)],
        scratch_shapes=[pltpu.VMEM((nb, tt + SUBLANE, D_CONV), F32)],
        compiler_params=_cparams(("parallel", "arbitrary")),
        name="out_proj",
    )(x, mod, o_mla, z_mla, conv4, conv4, conv_prev, z_rw, y_r, bonus, g_merge, *consts)


def _swap_halves(w):
    half = QK_ROPE // 2
    return jnp.concatenate([w[..., half:], w[..., :half]], axis=-1)


def _layer_params(l, w_ada, w_in, w_q_b, w_uk, w_uv, w_mla_out, w_conv_out, rwkv_w2, rwkv_a2,
                  w_rwkv_out, w_out):
    d = D_MODEL
    offs = [0]
    for s in (Q_LORA, KV_LORA, QK_ROPE, D_MLA, D_CONV, D_CONV, D_CONV, D_CONV, SHIFT_W, D_R, N_BRANCH * d):
        offs.append(offs[-1] + s)
    wi = w_in[l]
    k_rope = wi[:, offs[2]:offs[3]]
    kr = jnp.concatenate([k_rope, _swap_halves(k_rope), jnp.zeros((d, KR_W - 2 * QK_ROPE), F32)], axis=1)
    w_in_p = jnp.concatenate([wi[:, offs[0]:offs[2]], kr, wi[:, offs[3]:]], axis=1).astype(BF16)
    wq = w_q_b[l].reshape(Q_LORA, N_HEADS, QK_NOPE + QK_ROPE)
    wn = jnp.pad(wq[:, :, :QK_NOPE], ((0, 0), (0, 0), (0, LANE - QK_NOPE))).reshape(Q_LORA, N_HEADS * LANE)
    wr = wq[:, :, QK_NOPE:]
    wrs = _swap_halves(wr)
    uk = jnp.pad(jnp.transpose(w_uk[l], (1, 2, 0)), ((0, 0), (0, LANE - QK_NOPE), (0, 0)))
    wuv = jnp.transpose(w_uv[l], (1, 0, 2))
    zv = jnp.zeros_like(wuv)
    even = (jnp.arange(N_HEADS) % 2 == 0)[:, None, None]
    wuv_pad = jnp.where(even, jnp.concatenate([wuv, zv], axis=2), jnp.concatenate([zv, wuv], axis=2))
    zeros = jnp.zeros((W_LORA, D_R), F32)
    return dict(
        w_ada=w_ada[l].astype(BF16), w_in_p=w_in_p,
        wn=wn.astype(BF16), wr=wr.reshape(Q_LORA, -1).astype(BF16), wrs=wrs.reshape(Q_LORA, -1).astype(BF16),
        uk=uk.astype(BF16), wuv_pad=wuv_pad.astype(BF16),
        wmla=w_mla_out[l].astype(BF16), wconv=w_conv_out[l].astype(BF16),
        wrw=w_rwkv_out[l].astype(BF16), wout=w_out[l].astype(BF16),
        w2p=jnp.concatenate([rwkv_w2[l], zeros], axis=0).astype(BF16),
        a2p=jnp.concatenate([zeros, rwkv_a2[l]], axis=0).astype(BF16),
    )


def _rope_tables(pos):
    half = QK_ROPE // 2
    inv = ROPE_THETA ** (-jnp.arange(half, dtype=F32) / half)
    ang = pos.astype(F32)[:, None] * inv[None, :]
    cos, sin = jnp.cos(ang), jnp.sin(ang)
    cos_k = jnp.concatenate([cos, cos], axis=1)
    sin_k = jnp.concatenate([-sin, sin], axis=1)
    return jnp.tile(cos_k, (1, N_HEADS)), jnp.tile(sin_k, (1, N_HEADS)), cos_k, sin_k


def _state_to_lanes(s):
    b = s.shape[0]
    return jnp.transpose(s, (0, 2, 1, 3)).reshape(b, R_HEAD, D_R)


def _state_from_lanes(s):
    b = s.shape[0]
    return jnp.transpose(s.reshape(b, R_HEAD, R_HEADS, R_HEAD), (0, 2, 1, 3))


def _layer(x, c_mod, tabs, p, vecs, conv_prev, shift_prev, s0, consts, final_g, attend, q_dtype):
    bd, eye = consts
    q_a, kv_a, kr, z_mla, conv4, rw, z_rw, g_merge = _in_proj(x, c_mod, vecs["norm_g"], p["w_in_p"])
    qlat, qpe, ckv, kpe, ckv_bf, kpe_bf = _mla_prep(
        q_a, kv_a, kr, tabs, vecs["q_norm_g"], vecs["kv_norm_g"], p["wn"], p["wr"], p["wrs"], p["uk"], q_dtype)
    o_mla = attend(qlat, qpe, ckv_bf, kpe_bf, p["wuv_pad"])
    q, w, k, v, kk, b, vkr, bonus, shift_state = _rwkv_prep(
        rw, shift_prev, vecs["mu"], vecs["w0"], p["w2p"], vecs["a0"], p["a2p"],
        vecs["k_k"], vecs["k_a"], vecs["r_k"], bd)
    y_r, s_new = _rwkv_scan(q, w, k, v, kk, b, vkr, s0, bd, eye)
    x_new, y_norm, conv_state = _out_proj(
        x, c_mod, o_mla, z_mla, conv4, conv_prev, z_rw, y_r, bonus, g_merge,
        p["wmla"], p["wconv"], p["wrw"], p["wout"], vecs["conv_w"], vecs["gn_g"], vecs["gn_b"], bd, final_g)
    return x_new, y_norm, ckv, kpe, conv_state, shift_state[:, 0], s_new


def kernel(x_prompt, x_sample, cache_mla_ckv, cache_mla_kpe, state_conv, state_rwkv_shift, state_rwkv, page_table, c_prompt, c_sample, norm_g, w_ada, b_ada, w_in, q_norm_g, w_q_b, kv_norm_g, w_uk, w_uv, w_mla_out, conv_w, w_conv_out, rwkv_mu, rwkv_w0, rwkv_w2, rwkv_a0, rwkv_a2, rwkv_k_k, rwkv_k_a, rwkv_r_k, rwkv_gn_g, rwkv_gn_b, w_rwkv_out, w_out, final_norm_g):
    depth = norm_g.shape[0]
    bp, tp, _ = x_prompt.shape
    bs, ts, _ = x_sample.shape
    past = page_table.shape[1] * PAGE
    tabs_p = _rope_tables(jnp.arange(tp, dtype=jnp.int32))
    tabs_s = _rope_tables(past + jnp.arange(ts, dtype=jnp.int32))
    seg = jnp.arange(D_R, dtype=jnp.int32) // R_HEAD
    bd = (seg[:, None] == seg[None, :]).astype(BF16)
    eye = (jnp.arange(R_HEAD, dtype=jnp.int32)[:, None] == (jnp.arange(D_R, dtype=jnp.int32) % R_HEAD)[None, :]).astype(F32)
    consts = (bd, eye)
    xp, xs = x_prompt, x_sample
    st_p = ([], [], [], [], [])
    st_s = ([], [], [], [], [])
    yp = ys = None
    for l in range(depth):
        p = _layer_params(l, w_ada, w_in, w_q_b, w_uk, w_uv, w_mla_out, w_conv_out, rwkv_w2, rwkv_a2,
                          w_rwkv_out, w_out)
        vecs = dict(norm_g=norm_g[l], q_norm_g=q_norm_g[l], kv_norm_g=kv_norm_g[l], mu=rwkv_mu[l],
                    w0=rwkv_w0[l], a0=rwkv_a0[l], k_k=rwkv_k_k[l], k_a=rwkv_k_a[l],
                    r_k=rwkv_r_k[l].reshape(-1), conv_w=conv_w[l], gn_g=rwkv_gn_g[l], gn_b=rwkv_gn_b[l])
        mod_p = _ada_mod(c_prompt, p["w_ada"], b_ada[l])
        mod_s = _ada_mod(c_sample, p["w_ada"], b_ada[l])
        xp, yp, ckv, kpe, cst, sst, rst = _layer(
            xp, mod_p, tabs_p, p, vecs,
            jnp.zeros((bp, CONV_W - 1, D_CONV), F32), jnp.zeros((bp, SHIFT_W), F32),
            jnp.zeros((bp, R_HEAD, D_R), F32), consts, final_norm_g, _attn_prompt, BF16)
        for lst, val in zip(st_p, (ckv, kpe, cst, sst, _state_from_lanes(rst))):
            lst.append(val)
        attend_s = functools.partial(_attn_sample_bound, page_table, cache_mla_ckv, cache_mla_kpe, l)
        xs, ys, ckv, kpe, cst, sst, rst = _layer(
            xs, mod_s, tabs_s, p, vecs, state_conv[l], state_rwkv_shift[l],
            _state_to_lanes(state_rwkv[l]), consts, final_norm_g, attend_s, F32)
        for lst, val in zip(st_s, (ckv, kpe, cst, sst, _state_from_lanes(rst))):
            lst.append(val)
    return (yp, ys,
            jnp.stack(st_p[0]), jnp.stack(st_p[1]), jnp.stack(st_p[2]), jnp.stack(st_p[3]), jnp.stack(st_p[4]),
            jnp.stack(st_s[0]), jnp.stack(st_s[1]), jnp.stack(st_s[2]), jnp.stack(st_s[3]), jnp.stack(st_s[4]))


def _attn_sample_bound(page_table, cache_ckv, cache_kpe, layer, qlat, qpe, ckv_bf, kpe_bf, wuv_pad):
    return _attn_sample(page_table, qlat, qpe, ckv_bf, kpe_bf, wuv_pad, cache_ckv, cache_kpe, layer)
```

```python
import functools
import math

import jax
import jax.numpy as jnp
from jax import lax
from jax.experimental import pallas as pl
from jax.experimental.pallas import tpu as pltpu

F32 = jnp.float32
BF16 = jnp.bfloat16

D_MODEL = 1024
PAGE = 128
N_HEADS = 8
QK_NOPE = 64
QK_ROPE = 32
V_HEAD = 64
Q_LORA = 384
KV_LORA = 256
D_MLA = N_HEADS * V_HEAD
ROPE_THETA = 10000.0
ATTN_SCALE = (QK_NOPE + QK_ROPE) ** -0.5
Q_SCALE = ATTN_SCALE * math.log2(math.e)
D_CONV = 256
CONV_W = 3
R_HEADS = 4
R_HEAD = 64
D_R = R_HEADS * R_HEAD
W_LORA = 64
A_LORA = 64
SHIFT_W = 3 * D_R + W_LORA + A_LORA
GN_EPS = 64e-5
RMS_EPS = 1e-6
N_BRANCH = 3

LANE = 128
SUBLANE = 8
KR_W = LANE
SEG_W = (Q_LORA, KV_LORA, KR_W, D_MLA, 4 * D_CONV, SHIFT_W, D_R, N_BRANCH * D_MODEL)
SEG_OFF = tuple(sum(SEG_W[:i]) for i in range(len(SEG_W)))
PROJ_P = sum(SEG_W)
VMEM_LIMIT = 56 * 1024 * 1024
NEG_BIG = -1e30
SCAN_UNROLL = 4


def _cparams(sem):
    return pltpu.CompilerParams(dimension_semantics=sem, vmem_limit_bytes=VMEM_LIMIT)


def _dot(a, b):
    return jnp.dot(a, b, preferred_element_type=F32)


def _dot_nt(a, b):
    return lax.dot_general(a, b, (((1,), (1,)), ((), ())), preferred_element_type=F32)


def _sigmoid(x):
    return 1.0 / (1.0 + jnp.exp(-x))


def _silu(x):
    return x * _sigmoid(x)


def _softplus(x):
    return jnp.maximum(x, 0.0) + jnp.log1p(jnp.exp(-jnp.abs(x)))


def _seg_sum(x, bd):
    hi = x.astype(BF16)
    lo = (x - hi.astype(F32)).astype(BF16)
    return _dot(hi, bd) + _dot(lo, bd)


def _row_tiles(batch, seq, cap):
    if seq >= LANE:
        tt = min(seq, cap)
        assert seq % tt == 0
        return 1, tt
    assert seq % SUBLANE == 0
    return batch, seq


def _ada_kernel(c_ref, w_ref, b_ref, o_ref):
    s = _silu(c_ref[...])
    o_ref[...] = _dot(s.astype(BF16), w_ref[...]) + b_ref[...]


def _ada_mod(c, w_ada_bf, b_ada):
    bsz = c.shape[0]
    out = pl.pallas_call(
        _ada_kernel,
        out_shape=jax.ShapeDtypeStruct((bsz, 3 * D_MODEL), F32),
        compiler_params=pltpu.CompilerParams(vmem_limit_bytes=VMEM_LIMIT),
        name="ada_mod",
    )(c, w_ada_bf, b_ada.reshape(1, -1))
    return out.reshape(bsz, 1, 3 * D_MODEL)


def _in_proj_kernel(x_ref, mod_ref, g_ref, w_ref, *out_refs):
    nb, tt, d = x_ref.shape
    x = x_ref[...]
    ms = jnp.mean(x * x, axis=-1, keepdims=True)
    xn = x * lax.rsqrt(ms + RMS_EPS) * g_ref[...]
    shift = mod_ref[:, :, 0:D_MODEL]
    scale = mod_ref[:, :, D_MODEL:2 * D_MODEL]
    u = (xn * (1.0 + scale) + shift).reshape(nb * tt, d).astype(BF16)
    for o_ref, off, w in zip(out_refs, SEG_OFF, SEG_W):
        o_ref[...] = _dot(u, w_ref[:, off:off + w]).reshape(nb, tt, w)


def _in_proj(x, mod, norm_g, w_in_p):
    bsz, seq, d = x.shape
    nb, tt = _row_tiles(bsz, seq, 256)
    grid = (bsz // nb, seq // tt)
    row = lambda w: pl.BlockSpec((nb, tt, w), lambda b, t: (b, t, 0))
    return pl.pallas_call(
        _in_proj_kernel,
        grid=grid,
        in_specs=[row(d),
                  pl.BlockSpec((nb, 1, 3 * d), lambda b, t: (b, 0, 0)),
                  pl.BlockSpec((1, d), lambda b, t: (0, 0)),
                  pl.BlockSpec((d, PROJ_P), lambda b, t: (0, 0))],
        out_specs=[row(w) for w in SEG_W],
        out_shape=[jax.ShapeDtypeStruct((bsz, seq, w), F32) for w in SEG_W],
        compiler_params=_cparams(("parallel", "arbitrary")),
        name="in_proj",
    )(x, mod, norm_g.reshape(1, d), w_in_p)


def _mla_prep_kernel(qa_ref, kva_ref, kr_ref, cq_ref, sq_ref, ck_ref, sk_ref,
                     gq_ref, gkv_ref, wn_ref, wr_ref, wrs_ref, uk_ref,
                     qlat_ref, qpe_ref, ckv_ref, kpe_ref, ckvb_ref, kpeb_ref):
    nb, tt, _ = qa_ref.shape
    rows = nb * tt
    qa = qa_ref[...].reshape(rows, Q_LORA)
    cq = qa * lax.rsqrt(jnp.mean(qa * qa, axis=-1, keepdims=True) + RMS_EPS) * gq_ref[...]
    cqb = cq.astype(BF16)
    qn = _dot(cqb, wn_ref[...])
    cos_q = jnp.broadcast_to(cq_ref[...][None], (nb, tt, N_HEADS * QK_ROPE)).reshape(rows, -1)
    sin_q = jnp.broadcast_to(sq_ref[...][None], (nb, tt, N_HEADS * QK_ROPE)).reshape(rows, -1)
    qp = (_dot(cqb, wr_ref[...]) * cos_q + _dot(cqb, wrs_ref[...]) * sin_q) * Q_SCALE
    for h in range(N_HEADS):
        qn_h = qn[:, h * LANE:(h + 1) * LANE].astype(BF16)
        ql = _dot(qn_h, uk_ref[h]) * Q_SCALE
        qlat_ref[:, h] = ql.reshape(nb, tt, KV_LORA).astype(qlat_ref.dtype)
        qpe_ref[:, h] = qp[:, h * QK_ROPE:(h + 1) * QK_ROPE].reshape(nb, tt, QK_ROPE).astype(qpe_ref.dtype)
    kva = kva_ref[...]
    ckv = kva * lax.rsqrt(jnp.mean(kva * kva, axis=-1, keepdims=True) + RMS_EPS) * gkv_ref[...]
    ckv_ref[...] = ckv
    ckvb_ref[...] = ckv.astype(BF16)
    kr = kr_ref[...]
    kpe = kr[:, :, 0:QK_ROPE] * ck_ref[...][None] + kr[:, :, QK_ROPE:2 * QK_ROPE] * sk_ref[...][None]
    kpe_ref[...] = kpe
    kpeb_ref[...] = kpe.astype(BF16)


def _mla_prep(qa, kva, kr, tabs, q_norm_g, kv_norm_g, wn, wr, wrs, uk, q_dtype):
    bsz, seq, _ = qa.shape
    nb, tt = _row_tiles(bsz, seq, 256)
    grid = (bsz // nb, seq // tt)
    cos_q, sin_q, cos_k, sin_k = tabs
    row = lambda w: pl.BlockSpec((nb, tt, w), lambda b, t: (b, t, 0))
    tab = lambda w: pl.BlockSpec((tt, w), lambda b, t: (t, 0))
    full = lambda a: pl.BlockSpec(a.shape, lambda b, t: (0,) * a.ndim)
    hq = lambda w: pl.BlockSpec((nb, N_HEADS, tt, w), lambda b, t: (b, 0, t, 0))
    gq = q_norm_g.reshape(1, -1)
    gkv = kv_norm_g.reshape(1, -1)
    return pl.pallas_call(
        _mla_prep_kernel,
        grid=grid,
        in_specs=[row(Q_LORA), row(KV_LORA), row(KR_W),
                  tab(N_HEADS * QK_ROPE), tab(N_HEADS * QK_ROPE), tab(QK_ROPE), tab(QK_ROPE),
                  full(gq), full(gkv), full(wn), full(wr), full(wrs), full(uk)],
        out_specs=[hq(KV_LORA), hq(QK_ROPE), row(KV_LORA), row(QK_ROPE), row(KV_LORA), row(QK_ROPE)],
        out_shape=[jax.ShapeDtypeStruct((bsz, N_HEADS, seq, KV_LORA), q_dtype),
                   jax.ShapeDtypeStruct((bsz, N_HEADS, seq, QK_ROPE), q_dtype),
                   jax.ShapeDtypeStruct((bsz, seq, KV_LORA), F32),
                   jax.ShapeDtypeStruct((bsz, seq, QK_ROPE), F32),
                   jax.ShapeDtypeStruct((bsz, seq, KV_LORA), BF16),
                   jax.ShapeDtypeStruct((bsz, seq, QK_ROPE), BF16)],
        compiler_params=_cparams(("parallel", "arbitrary")),
        name="mla_prep",
    )(qa, kva, kr, cos_q, sin_q, cos_k, sin_k, gq, gkv, wn, wr, wrs, uk)


def _value_up_proj(o_lat, wuv_ref, t):
    tiles = []
    for j in range(N_HEADS // 2):
        h0, h1 = 2 * j, 2 * j + 1
        tiles.append(_dot(o_lat[h0 * t:(h0 + 1) * t].astype(BF16), wuv_ref[h0])
                     + _dot(o_lat[h1 * t:(h1 + 1) * t].astype(BF16), wuv_ref[h1]))
    return jnp.concatenate(tiles, axis=1)


def _softmax_update(s, m_old, l_old):
    m_new = jnp.maximum(m_old, jnp.max(s, axis=-1, keepdims=True))
    alpha = jnp.exp2(m_old - m_new)
    p = jnp.exp2(s - jnp.tile(m_new, (1, s.shape[1] // LANE)))
    l_new = alpha * l_old + jnp.sum(p, axis=-1, keepdims=True)
    return p, m_new, l_new, alpha


def _attn_prompt_kernel(ql_ref, qp_ref, ckv_ref, kpe_ref, wuv_ref, o_ref,
                        m_scr, l_scr, acc_scr, s_scr, p_scr, *, tq, tk, rc):
    qi = pl.program_id(1)
    rows = N_HEADS * tq
    ql = ql_ref[0].reshape(rows, KV_LORA)
    qp = qp_ref[0].reshape(rows, QK_ROPE)
    m_scr[...] = jnp.full(m_scr.shape, NEG_BIG, F32)
    l_scr[...] = jnp.zeros(l_scr.shape, F32)
    acc_scr[...] = jnp.zeros(acc_scr.shape, F32)
    n_kt = (qi * tq + tq + tk - 1) // tk

    def scores(kt, slot):
        start = pl.multiple_of(kt * tk, tk)
        s_scr[slot] = (_dot_nt(ql, ckv_ref[0, pl.ds(start, tk), :])
                       + _dot_nt(qp, kpe_ref[0, pl.ds(start, tk), :]))

    def softmax(kt, slot, masked):
        for c in range(rows // rc):
            rs = slice(c * rc, (c + 1) * rc)
            s = s_scr[slot, rs, :]
            if masked:
                q_pos = qi * tq + (c * rc) % tq + lax.broadcasted_iota(jnp.int32, (rc, tk), 0)
                k_pos = kt * tk + lax.broadcasted_iota(jnp.int32, (rc, tk), 1)
                s = jnp.where(k_pos <= q_pos, s, NEG_BIG)
            p, m_new, l_new, alpha = _softmax_update(s, m_scr[rs, :], l_scr[rs, :])
            m_scr[rs, :] = m_new
            l_scr[rs, :] = l_new
            p_scr[rs, :] = p.astype(BF16)
            acc_scr[rs, :] = jnp.tile(alpha, (1, KV_LORA // LANE)) * acc_scr[rs, :]

    def weighted_values(kt):
        start = pl.multiple_of(kt * tk, tk)
        acc_scr[...] += _dot(p_scr[...], ckv_ref[0, pl.ds(start, tk), :])

    scores(0, 0)

    def body(kt, carry):
        slot = kt % 2
        softmax(kt, slot, False)
        scores(kt + 1, 1 - slot)
        weighted_values(kt)
        return carry

    lax.fori_loop(0, n_kt - 1, body, 0)
    last = n_kt - 1
    softmax(last, last % 2, True)
    weighted_values(last)
    o_lat = acc_scr[...] / jnp.tile(l_scr[...], (1, KV_LORA // LANE))
    o_ref[0] = _value_up_proj(o_lat, wuv_ref, tq)


def _attn_prompt(qlat, qpe, ckv_bf, kpe_bf, wuv_pad):
    bsz, _, seq, _ = qlat.shape
    tq = 128
    tk = min(256, seq)
    assert tk % tq == 0 and seq % tk == 0
    rc = 64
    kern = functools.partial(_attn_prompt_kernel, tq=tq, tk=tk, rc=rc)
    rows = N_HEADS * tq
    return pl.pallas_call(
        kern,
        grid=(bsz, seq // tq),
        in_specs=[pl.BlockSpec((1, N_HEADS, tq, KV_LORA), lambda b, q: (b, 0, q, 0)),
                  pl.BlockSpec((1, N_HEADS, tq, QK_ROPE), lambda b, q: (b, 0, q, 0)),
                  pl.BlockSpec((1, seq, KV_LORA), lambda b, q: (b, 0, 0)),
                  pl.BlockSpec((1, seq, QK_ROPE), lambda b, q: (b, 0, 0)),
                  pl.BlockSpec(wuv_pad.shape, lambda b, q: (0, 0, 0))],
        out_specs=pl.BlockSpec((1, tq, D_MLA), lambda b, q: (b, q, 0)),
        out_shape=jax.ShapeDtypeStruct((bsz, seq, D_MLA), F32),
        scratch_shapes=[pltpu.VMEM((rows, LANE), F32), pltpu.VMEM((rows, LANE), F32),
                        pltpu.VMEM((rows, KV_LORA), F32),
                        pltpu.VMEM((2, rows, tk), F32), pltpu.VMEM((rows, tk), BF16)],
        compiler_params=_cparams(("parallel", "arbitrary")),
        name="attn_prompt",
    )(qlat, qpe, ckv_bf, kpe_bf, wuv_pad)


def _attn_sample_kernel(pt_ref, ql_ref, qp_ref, ckvn_ref, kpen_ref, wuv_ref, cache_ckv, cache_kpe_t,
                        o_ref, ckv_buf, kpe_buf, sem, *, layer, pp, cp, n_steps, ts):
    b = pl.program_id(0)
    nb = pl.num_programs(0)
    rows = N_HEADS * ts
    total = nb * n_steps
    ql = ql_ref[0].reshape(rows, KV_LORA).astype(BF16)
    qp = qp_ref[0].reshape(rows, QK_ROPE).astype(BF16)

    def page_copies(seq, grp, slot):
        out = []
        for i in range(pp):
            page = pt_ref[seq, grp * pp + i]
            out.append(pltpu.make_async_copy(cache_ckv.at[layer, page], ckv_buf.at[slot, i], sem.at[slot]))
            out.append(pltpu.make_async_copy(cache_kpe_t.at[layer, page], kpe_buf.at[slot, i], sem.at[slot]))
        return out

    def start_group(g, slot):
        g = jnp.minimum(g, total - 1)
        seq = lax.shift_right_logical(g, n_steps.bit_length() - 1)
        for c in page_copies(seq, jnp.bitwise_and(g, n_steps - 1), slot):
            c.start()

    def wait_group(slot):
        for c in page_copies(0, 0, slot):
            c.wait()

    @pl.when(b == 0)
    def _():
        start_group(0, 0)

    def partial_softmax(s, vals):
        m = jnp.max(s, axis=-1, keepdims=True)
        p = jnp.exp2(s - m)
        return m, jnp.sum(p, axis=-1, keepdims=True), _dot(p.astype(BF16), vals)

    def merge(parts):
        m = functools.reduce(jnp.maximum, [pm for pm, _, _ in parts])
        l = sum(pl_ * jnp.exp2(pm - m) for pm, pl_, _ in parts)
        acc = sum(pa * jnp.exp2(pm - m) for pm, _, pa in parts)
        return m, l, acc

    def group_parts(slot):
        n_c = pp // cp
        cks = [ckv_buf[slot, c * cp:(c + 1) * cp].reshape(cp * PAGE, KV_LORA).astype(BF16) for c in range(n_c)]
        kps = [jnp.concatenate([kpe_buf[slot, c * cp + i].astype(BF16) for i in range(cp)], axis=1)
               for c in range(n_c)]
        ss = [_dot_nt(ql, ck) + _dot(qp, kp_t) for ck, kp_t in zip(cks, kps)]
        ms = [jnp.max(s, axis=-1, keepdims=True) for s in ss]
        ps = [jnp.exp2(s - m) for s, m in zip(ss, ms)]
        ls = [jnp.sum(p, axis=-1, keepdims=True) for p in ps]
        accs = [_dot(p.astype(BF16), ck) for p, ck in zip(ps, cks)]
        return list(zip(ms, ls, accs))

    ckn = ckvn_ref[0]
    kpn = kpen_ref[0]
    s_new = _dot_nt(ql, ckn) + _dot_nt(qp, kpn)
    t_q = lax.broadcasted_iota(jnp.int32, (rows, ts), 0) % ts
    t_k = lax.broadcasted_iota(jnp.int32, (rows, ts), 1)
    state = partial_softmax(jnp.where(t_k <= t_q, s_new, NEG_BIG), ckn)

    for st in range(n_steps):
        slot = st % 2
        start_group(b * n_steps + st + 1, 1 - slot)
        wait_group(slot)
        state = merge([state] + group_parts(slot))

    @pl.when(b == nb - 1)
    def _():
        wait_group(n_steps % 2)

    _, l, acc = state
    o_ref[0] = _value_up_proj(acc / l, wuv_ref, ts)


def _attn_sample(page_table, qlat, qpe, ckv_new_bf, kpe_new_bf, wuv_pad, cache_ckv, cache_kpe, layer):
    bsz, _, ts, _ = qlat.shape
    n_pages = page_table.shape[1]
    pp = min(32, n_pages // 2)
    cp = min(8, pp)
    n_steps = n_pages // pp
    assert n_pages % pp == 0 and pp % cp == 0
    assert n_steps % 2 == 0 and n_steps & (n_steps - 1) == 0
    kern = functools.partial(_attn_sample_kernel, layer=layer, pp=pp, cp=cp, n_steps=n_steps, ts=ts)

    cache_kpe_t = jnp.swapaxes(cache_kpe, 2, 3)

    grid_spec = pltpu.PrefetchScalarGridSpec(
        num_scalar_prefetch=1,
        grid=(bsz,),
        in_specs=[pl.BlockSpec((1, N_HEADS, ts, KV_LORA), lambda b, pt: (b, 0, 0, 0)),
                  pl.BlockSpec((1, N_HEADS, ts, QK_ROPE), lambda b, pt: (b, 0, 0, 0)),
                  pl.BlockSpec((1, ts, KV_LORA), lambda b, pt: (b, 0, 0)),
                  pl.BlockSpec((1, ts, QK_ROPE), lambda b, pt: (b, 0, 0)),
                  pl.BlockSpec(wuv_pad.shape, lambda b, pt: (0, 0, 0)),
                  pl.BlockSpec(memory_space=pl.ANY),
                  pl.BlockSpec(memory_space=pl.ANY)],
        out_specs=pl.BlockSpec((1, ts, D_MLA), lambda b, pt: (b, 0, 0)),
        scratch_shapes=[pltpu.VMEM((2, pp, PAGE, KV_LORA), F32),
                        pltpu.VMEM((2, pp, QK_ROPE, PAGE), F32),
                        pltpu.SemaphoreType.DMA((2,))],
    )
    return pl.pallas_call(
        kern,
        grid_spec=grid_spec,
        out_shape=jax.ShapeDtypeStruct((bsz, ts, D_MLA), F32),
        compiler_params=_cparams(("arbitrary",)),
        name="attn_sample",
    )(page_table, qlat, qpe, ckv_new_bf, kpe_new_bf, wuv_pad, cache_ckv, cache_kpe_t)


def _rwkv_prep_kernel(rw_ref, halo_ref, sprev_ref, mu_ref, w0_ref, w2_ref, a0_ref, a2_ref,
                      kk_ref, ka_ref, rk_ref, bd_ref,
                      q_out, w_out, k_out, v_out, kk_out, b_out, vkr_out, bonus_out, shift_out, buf):
    nb, tt, _ = rw_ref.shape
    ti = pl.program_id(1)
    rows = nb * tt

    @pl.when(ti == 0)
    def _():
        buf[:, SUBLANE - 1:SUBLANE, :] = sprev_ref[...]

    @pl.when(ti > 0)
    def _():
        buf[:, 0:SUBLANE, :] = halo_ref[...]

    rw = rw_ref[...]
    buf[:, SUBLANE:, :] = rw
    shift_out[...] = rw[:, tt - 1:tt, :]
    rw_prev = buf[:, SUBLANE - 1:SUBLANE - 1 + tt, :]
    rws = (rw + mu_ref[...] * (rw_prev - rw)).reshape(rows, SHIFT_W)
    r = rws[:, 0:D_R]
    k = rws[:, D_R:2 * D_R]
    v = rws[:, 2 * D_R:3 * D_R]
    wa = rws[:, 3 * D_R:]
    w_log = -_softplus(-(w0_ref[...] + _dot(jnp.tanh(wa).astype(BF16), w2_ref[...]))) - 0.5
    decay = jnp.exp(-jnp.exp(w_log))
    a = _sigmoid(a0_ref[...] + _dot(wa.astype(BF16), a2_ref[...]))
    bd = bd_ref[...]
    kk = k * kk_ref[...]
    kk = kk / jnp.maximum(jnp.sqrt(_seg_sum(kk * kk, bd)), 1e-12)
    k = k * (1.0 + (a - 1.0) * ka_ref[...])
    bonus = _seg_sum(r * k * rk_ref[...], bd) * v
    b = kk * a
    q = decay * r - kk * _seg_sum(b * r, bd)
    vkr = v * _seg_sum(k * r, bd)
    shp = (nb, tt, D_R)
    q_out[...] = q.reshape(shp)
    w_out[...] = decay.reshape(shp)
    k_out[...] = k.reshape(shp)
    v_out[...] = v.reshape(shp)
    kk_out[...] = kk.reshape(shp)
    b_out[...] = b.reshape(shp)
    vkr_out[...] = vkr.reshape(shp)
    bonus_out[...] = bonus.reshape(shp)


def _rwkv_prep(rw, shift_prev, mu, w0, w2p, a0, a2p, k_k, k_a, r_k, bd):
    bsz, seq, _ = rw.shape
    nb, tt = _row_tiles(bsz, seq, 256)
    grid = (bsz // nb, seq // tt)
    hb = tt // SUBLANE
    row = lambda w: pl.BlockSpec((nb, tt, w), lambda b, t: (b, t, 0))
    vec = lambda a: pl.BlockSpec(a.shape, lambda b, t: (0,) * a.ndim)
    vecs = [mu.reshape(1, -1), w0.reshape(1, -1), w2p, a0.reshape(1, -1), a2p,
            k_k.reshape(1, -1), k_a.reshape(1, -1), r_k.reshape(1, -1), bd]
    outs = pl.pallas_call(
        _rwkv_prep_kernel,
        grid=grid,
        in_specs=[row(SHIFT_W),
                  pl.BlockSpec((nb, SUBLANE, SHIFT_W), lambda b, t: (b, jnp.maximum(t * hb - 1, 0), 0)),
                  pl.BlockSpec((nb, 1, SHIFT_W), lambda b, t: (b, 0, 0))]
                 + [vec(a) for a in vecs],
        out_specs=[row(D_R)] * 8 + [pl.BlockSpec((nb, 1, SHIFT_W), lambda b, t: (b, 0, 0))],
        out_shape=[jax.ShapeDtypeStruct((bsz, seq, D_R), F32)] * 8
                  + [jax.ShapeDtypeStruct((bsz, 1, SHIFT_W), F32)],
        scratch_shapes=[pltpu.VMEM((nb, tt + SUBLANE, SHIFT_W), F32)],
        compiler_params=_cparams(("parallel", "arbitrary")),
        name="rwkv_prep",
    )(rw, rw, shift_prev.reshape(bsz, 1, SHIFT_W), *vecs)
    return outs


def _rwkv_scan_kernel(q_ref, w_ref, k_ref, v_ref, kk_ref, b_ref, vkr_ref, s0_ref, bd_ref, eye_ref,
                      y_ref, sT_ref, s_scr):
    nb, tc, _ = q_ref.shape
    ci = pl.program_id(1)
    rows = nb * R_HEAD

    @pl.when(ci == 0)
    def _():
        s_scr[...] = s0_ref[...]

    bd = bd_ref[...]
    eye = eye_ref[...][None]
    eye_bf = eye.astype(BF16)

    def seg(x):
        return _dot(x.reshape(rows, D_R), bd).reshape(nb, R_HEAD, D_R)

    def step(t, carry):
        row = lambda ref: ref[:, pl.ds(t, 1), :]
        s = s_scr[...]
        s_bf = s.astype(BF16)
        sa = seg(s_bf * row(kk_ref).astype(BF16))
        y_col = seg(s_bf * row(q_ref).astype(BF16))
        v_col = seg(eye_bf * row(v_ref).astype(BF16))
        s_scr[...] = s * row(w_ref) - sa * row(b_ref) + v_col * row(k_ref)
        y_ref[:, pl.ds(t, 1), :] = jnp.sum(y_col * eye, axis=1, keepdims=True) + row(vkr_ref)
        return carry

    lax.fori_loop(0, tc, step, 0, unroll=SCAN_UNROLL)

    @pl.when(ci == pl.num_programs(1) - 1)
    def _():
        sT_ref[...] = s_scr[...]


def _rwkv_scan(q, w, k, v, kk, b, vkr, s0, bd, eye):
    bsz, seq, _ = q.shape
    nb = math.gcd(bsz, 8)
    tc = min(seq, 256)
    row = pl.BlockSpec((nb, tc, D_R), lambda bi, c: (bi, c, 0))
    st = pl.BlockSpec((nb, R_HEAD, D_R), lambda bi, c: (bi, 0, 0))
    return pl.pallas_call(
        _rwkv_scan_kernel,
        grid=(bsz // nb, seq // tc),
        in_specs=[row] * 7 + [st, pl.BlockSpec(bd.shape, lambda bi, c: (0, 0)),
                              pl.BlockSpec(eye.shape, lambda bi, c: (0, 0))],
        out_specs=[row, st],
        out_shape=[jax.ShapeDtypeStruct((bsz, seq, D_R), F32),
                   jax.ShapeDtypeStruct((bsz, R_HEAD, D_R), F32)],
        scratch_shapes=[pltpu.VMEM((nb, R_HEAD, D_R), F32)],
        compiler_params=_cparams(("parallel", "arbitrary")),
        name="rwkv_scan",
    )(q, w, k, v, kk, b, vkr, s0, bd, eye)


def _out_kernel(x_ref, mod_ref, omla_ref, zmla_ref, conv_ref, halo_ref, cprev_ref, zrw_ref,
                yr_ref, bonus_ref, gm_ref, wmla_ref, wconv_ref, wrw_ref, wout_ref, cw_ref,
                gng_ref, gnb_ref, bd_ref, fg_ref, xo_ref, cstate_ref, buf, *, final):
    nb, tt, d = x_ref.shape
    ti = pl.program_id(1)
    rows = nb * tt

    @pl.when(ti == 0)
    def _():
        buf[:, SUBLANE - 2:SUBLANE, :] = cprev_ref[...]

    @pl.when(ti > 0)
    def _():
        buf[:, 0:SUBLANE, :] = halo_ref[:, :, D_CONV:2 * D_CONV] * halo_ref[:, :, 2 * D_CONV:3 * D_CONV]

    cb = conv_ref[:, :, 0:D_CONV]
    buf[:, SUBLANE:, :] = conv_ref[:, :, D_CONV:2 * D_CONV] * conv_ref[:, :, 2 * D_CONV:3 * D_CONV]
    zc = conv_ref[:, :, 3 * D_CONV:]
    cstate_ref[...] = buf[:, tt + SUBLANE - 2:tt + SUBLANE, :]
    conv = (buf[:, SUBLANE - 2:SUBLANE - 2 + tt, :] * cw_ref[0:1, :]
            + buf[:, SUBLANE - 1:SUBLANE - 1 + tt, :] * cw_ref[1:2, :]
            + buf[:, SUBLANE:, :] * cw_ref[2:3, :])
    y_conv = _dot((_silu(zc) * cb * conv).reshape(rows, D_CONV).astype(BF16), wconv_ref[...])

    y_mla = _dot((omla_ref[...] * _silu(zmla_ref[...])).reshape(rows, D_MLA).astype(BF16), wmla_ref[...])

    bd = bd_ref[...]
    yr = yr_ref[...].reshape(rows, D_R)
    mu = _seg_sum(yr, bd) * (1.0 / R_HEAD)
    dy = yr - mu
    var = _seg_sum(dy * dy, bd) * (1.0 / R_HEAD)
    yn = dy * lax.rsqrt(var + GN_EPS) * gng_ref[...] + gnb_ref[...]
    o_rw = yn + bonus_ref[...].reshape(rows, D_R)
    y_rw = _dot((o_rw * _silu(zrw_ref[...].reshape(rows, D_R))).astype(BF16), wrw_ref[...])

    g = gm_ref[...].reshape(rows, N_BRANCH * d)
    merged = (_sigmoid(g[:, 0:d]) * y_mla + _sigmoid(g[:, d:2 * d]) * y_conv
              + _sigmoid(g[:, 2 * d:]) * y_rw)
    delta = _dot(merged.astype(BF16), wout_ref[...]).reshape(nb, tt, d)
    xo = x_ref[...] + mod_ref[:, :, 2 * d:] * delta
    if final:
        xo = xo * lax.rsqrt(jnp.mean(xo * xo, axis=-1, keepdims=True) + RMS_EPS) * fg_ref[...]
    xo_ref[...] = xo


def _out_proj(x, mod, o_mla, z_mla, conv4, conv_prev, z_rw, y_r, bonus, g_merge,
              wmla, wconv, wrw, wout, conv_w, gn_g, gn_b, bd, final_g, final):
    bsz, seq, d = x.shape
    nb, tt = _row_tiles(bsz, seq, 256)
    grid = (bsz // nb, seq // tt)
    hb = tt // SUBLANE
    row = lambda w: pl.BlockSpec((nb, tt, w), lambda b, t: (b, t, 0))
    full = lambda a: pl.BlockSpec(a.shape, lambda b, t: (0,) * a.ndim)
    consts = [wmla, wconv, wrw, wout, conv_w, gn_g.reshape(1, -1), gn_b.reshape(1, -1), bd,
              final_g.reshape(1, -1)]
    return pl.pallas_call(
        functools.partial(_out_kernel, final=final),
        grid=grid,
        in_specs=[row(d), pl.BlockSpec((nb, 1, 3 * d), lambda b, t: (b, 0, 0)),
                  row(D_MLA), row(D_MLA), row(4 * D_CONV),
                  pl.BlockSpec((nb, SUBLANE, 4 * D_CONV), lambda b, t: (b, jnp.maximum(t * hb - 1, 0), 0)),
                  pl.BlockSpec((nb, CONV_W - 1, D_CONV), lambda b, t: (b, 0, 0)),
                  row(D_R), row(D_R), row(D_R), row(N_BRANCH * d)] + [full(a) for a in consts],
        out_specs=[row(d), pl.BlockSpec((nb, CONV_W - 1, D_CONV), lambda b, t: (b, 0, 0))],
        out_shape=[jax.ShapeDtypeStruct((bsz, seq, d), F32),
                   jax.ShapeDtypeStruct((bsz, CONV_W - 1, D_CONV), F32)],
        scratch_shapes=[pltpu.VMEM((nb, tt + SUBLANE, D_CONV), F32)],
        compiler_params=_cparams(("parallel", "arbitrary")),
        name="out_proj",
    )(x, mod, o_mla, z_mla, conv4, conv4, conv_prev, z_rw, y_r, bonus, g_merge, *consts)


def _swap_halves(w):
    half = QK_ROPE // 2
    return jnp.concatenate([w[..., half:], w[..., :half]], axis=-1)


def _layer_params(l, w_ada, w_in, w_q_b, w_uk, w_uv, w_mla_out, w_conv_out, rwkv_w2, rwkv_a2,
                  w_rwkv_out, w_out):
    d = D_MODEL
    offs = [0]
    for s in (Q_LORA, KV_LORA, QK_ROPE, D_MLA, D_CONV, D_CONV, D_CONV, D_CONV, SHIFT_W, D_R, N_BRANCH * d):
        offs.append(offs[-1] + s)
    wi = w_in[l]
    k_rope = wi[:, offs[2]:offs[3]]
    kr = jnp.concatenate([k_rope, _swap_halves(k_rope), jnp.zeros((d, KR_W - 2 * QK_ROPE), F32)], axis=1)
    w_in_p = jnp.concatenate([wi[:, offs[0]:offs[2]], kr, wi[:, offs[3]:]], axis=1).astype(BF16)
    wq = w_q_b[l].reshape(Q_LORA, N_HEADS, QK_NOPE + QK_ROPE)
    wn = jnp.pad(wq[:, :, :QK_NOPE], ((0, 0), (0, 0), (0, LANE - QK_NOPE))).reshape(Q_LORA, N_HEADS * LANE)
    wr = wq[:, :, QK_NOPE:]
    wrs = _swap_halves(wr)
    uk = jnp.pad(jnp.transpose(w_uk[l], (1, 2, 0)), ((0, 0), (0, LANE - QK_NOPE), (0, 0)))
    wuv = jnp.transpose(w_uv[l], (1, 0, 2))
    zv = jnp.zeros_like(wuv)
    even = (jnp.arange(N_HEADS) % 2 == 0)[:, None, None]
    wuv_pad = jnp.where(even, jnp.concatenate([wuv, zv], axis=2), jnp.concatenate([zv, wuv], axis=2))
    zeros = jnp.zeros((W_LORA, D_R), F32)
    return dict(
        w_ada=w_ada[l].astype(BF16), w_in_p=w_in_p,
        wn=wn.astype(BF16), wr=wr.reshape(Q_LORA, -1).astype(BF16), wrs=wrs.reshape(Q_LORA, -1).astype(BF16),
        uk=uk.astype(BF16), wuv_pad=wuv_pad.astype(BF16),
        wmla=w_mla_out[l].astype(BF16), wconv=w_conv_out[l].astype(BF16),
        wrw=w_rwkv_out[l].astype(BF16), wout=w_out[l].astype(BF16),
        w2p=jnp.concatenate([rwkv_w2[l], zeros], axis=0).astype(BF16),
        a2p=jnp.concatenate([zeros, rwkv_a2[l]], axis=0).astype(BF16),
    )


def _rope_tables(pos):
    half = QK_ROPE // 2
    inv = ROPE_THETA ** (-jnp.arange(half, dtype=F32) / half)
    ang = pos.astype(F32)[:, None] * inv[None, :]
    cos, sin = jnp.cos(ang), jnp.sin(ang)
    cos_k = jnp.concatenate([cos, cos], axis=1)
    sin_k = jnp.concatenate([-sin, sin], axis=1)
    return jnp.tile(cos_k, (1, N_HEADS)), jnp.tile(sin_k, (1, N_HEADS)), cos_k, sin_k


def _state_to_lanes(s):
    b = s.shape[0]
    return jnp.transpose(s, (0, 2, 1, 3)).reshape(b, R_HEAD, D_R)


def _state_from_lanes(s):
    b = s.shape[0]
    return jnp.transpose(s.reshape(b, R_HEAD, R_HEADS, R_HEAD), (0, 2, 1, 3))


def _layer(x, c_mod, tabs, p, vecs, conv_prev, shift_prev, s0, consts, final_g, final, attend, q_dtype):
    bd, eye = consts
    q_a, kv_a, kr, z_mla, conv4, rw, z_rw, g_merge = _in_proj(x, c_mod, vecs["norm_g"], p["w_in_p"])
    qlat, qpe, ckv, kpe, ckv_bf, kpe_bf = _mla_prep(
        q_a, kv_a, kr, tabs, vecs["q_norm_g"], vecs["kv_norm_g"], p["wn"], p["wr"], p["wrs"], p["uk"], q_dtype)
    o_mla = attend(qlat, qpe, ckv_bf, kpe_bf, p["wuv_pad"])
    q, w, k, v, kk, b, vkr, bonus, shift_state = _rwkv_prep(
        rw, shift_prev, vecs["mu"], vecs["w0"], p["w2p"], vecs["a0"], p["a2p"],
        vecs["k_k"], vecs["k_a"], vecs["r_k"], bd)
    y_r, s_new = _rwkv_scan(q, w, k, v, kk, b, vkr, s0, bd, eye)
    x_new, conv_state = _out_proj(
        x, c_mod, o_mla, z_mla, conv4, conv_prev, z_rw, y_r, bonus, g_merge,
        p["wmla"], p["wconv"], p["wrw"], p["wout"], vecs["conv_w"], vecs["gn_g"], vecs["gn_b"], bd,
        final_g, final)
    return x_new, ckv, kpe, conv_state, shift_state[:, 0], s_new


def kernel(x_prompt, x_sample, cache_mla_ckv, cache_mla_kpe, state_conv, state_rwkv_shift, state_rwkv, page_table, c_prompt, c_sample, norm_g, w_ada, b_ada, w_in, q_norm_g, w_q_b, kv_norm_g, w_uk, w_uv, w_mla_out, conv_w, w_conv_out, rwkv_mu, rwkv_w0, rwkv_w2, rwkv_a0, rwkv_a2, rwkv_k_k, rwkv_k_a, rwkv_r_k, rwkv_gn_g, rwkv_gn_b, w_rwkv_out, w_out, final_norm_g):
    depth = norm_g.shape[0]
    bp, tp, _ = x_prompt.shape
    bs, ts, _ = x_sample.shape
    past = page_table.shape[1] * PAGE
    tabs_p = _rope_tables(jnp.arange(tp, dtype=jnp.int32))
    tabs_s = _rope_tables(past + jnp.arange(ts, dtype=jnp.int32))
    seg = jnp.arange(D_R, dtype=jnp.int32) // R_HEAD
    bd = (seg[:, None] == seg[None, :]).astype(BF16)
    eye = (jnp.arange(R_HEAD, dtype=jnp.int32)[:, None] == (jnp.arange(D_R, dtype=jnp.int32) % R_HEAD)[None, :]).astype(F32)
    consts = (bd, eye)
    xp, xs = x_prompt, x_sample
    st_p = ([], [], [], [], [])
    st_s = ([], [], [], [], [])
    for l in range(depth):
        final = l == depth - 1
        p = _layer_params(l, w_ada, w_in, w_q_b, w_uk, w_uv, w_mla_out, w_conv_out, rwkv_w2, rwkv_a2,
                          w_rwkv_out, w_out)
        vecs = dict(norm_g=norm_g[l], q_norm_g=q_norm_g[l], kv_norm_g=kv_norm_g[l], mu=rwkv_mu[l],
                    w0=rwkv_w0[l], a0=rwkv_a0[l], k_k=rwkv_k_k[l], k_a=rwkv_k_a[l],
                    r_k=rwkv_r_k[l].reshape(-1), conv_w=conv_w[l], gn_g=rwkv_gn_g[l], gn_b=rwkv_gn_b[l])
        mod_p = _ada_mod(c_prompt, p["w_ada"], b_ada[l])
        mod_s = _ada_mod(c_sample, p["w_ada"], b_ada[l])
        xp, ckv, kpe, cst, sst, rst = _layer(
            xp, mod_p, tabs_p, p, vecs,
            jnp.zeros((bp, CONV_W - 1, D_CONV), F32), jnp.zeros((bp, SHIFT_W), F32),
            jnp.zeros((bp, R_HEAD, D_R), F32), consts, final_norm_g, final, _attn_prompt, BF16)
        for lst, val in zip(st_p, (ckv, kpe, cst, sst, _state_from_lanes(rst))):
            lst.append(val)
        attend_s = functools.partial(_attn_sample_bound, page_table, cache_mla_ckv, cache_mla_kpe, l)
        xs, ckv, kpe, cst, sst, rst = _layer(
            xs, mod_s, tabs_s, p, vecs, state_conv[l], state_rwkv_shift[l],
            _state_to_lanes(state_rwkv[l]), consts, final_norm_g, final, attend_s, F32)
        for lst, val in zip(st_s, (ckv, kpe, cst, sst, _state_from_lanes(rst))):
            lst.append(val)
    return (xp, xs,
            jnp.stack(st_p[0]), jnp.stack(st_p[1]), jnp.stack(st_p[2]), jnp.stack(st_p[3]), jnp.stack(st_p[4]),
            jnp.stack(st_s[0]), jnp.stack(st_s[1]), jnp.stack(st_s[2]), jnp.stack(st_s[3]), jnp.stack(st_s[4]))


def _attn_sample_bound(page_table, cache_ckv, cache_kpe, layer, qlat, qpe, ckv_bf, kpe_bf, wuv_pad):
    return _attn_sample(page_table, qlat, qpe, ckv_bf, kpe_bf, wuv_pad, cache_ckv, cache_kpe, layer)
```

```python
import functools
import math

import jax
import jax.numpy as jnp
from jax import lax
from jax.experimental import pallas as pl
from jax.experimental.pallas import tpu as pltpu

F32 = jnp.float32
BF16 = jnp.bfloat16

D_MODEL = 1024
PAGE = 128
N_HEADS = 8
QK_NOPE = 64
QK_ROPE = 32
V_HEAD = 64
Q_LORA = 384
KV_LORA = 256
D_MLA = N_HEADS * V_HEAD
ROPE_THETA = 10000.0
ATTN_SCALE = (QK_NOPE + QK_ROPE) ** -0.5
Q_SCALE = ATTN_SCALE * math.log2(math.e)
D_CONV = 256
CONV_W = 3
R_HEADS = 4
R_HEAD = 64
D_R = R_HEADS * R_HEAD
W_LORA = 64
A_LORA = 64
SHIFT_W = 3 * D_R + W_LORA + A_LORA
GN_EPS = 64e-5
RMS_EPS = 1e-6
N_BRANCH = 3

LANE = 128
SUBLANE = 8
KR_W = LANE
SEG_W = (Q_LORA, KV_LORA, KR_W, D_MLA, 4 * D_CONV, SHIFT_W, D_R, N_BRANCH * D_MODEL)
SEG_OFF = tuple(sum(SEG_W[:i]) for i in range(len(SEG_W)))
PROJ_P = sum(SEG_W)
VMEM_LIMIT = 56 * 1024 * 1024
NEG_BIG = -1e30
SCAN_UNROLL = 8


def _cparams(sem):
    return pltpu.CompilerParams(dimension_semantics=sem, vmem_limit_bytes=VMEM_LIMIT)


def _dot(a, b):
    return jnp.dot(a, b, preferred_element_type=F32)


def _dot_nt(a, b):
    return lax.dot_general(a, b, (((1,), (1,)), ((), ())), preferred_element_type=F32)


def _sigmoid(x):
    return 1.0 / (1.0 + jnp.exp(-x))


def _silu(x):
    return x * _sigmoid(x)


def _softplus(x):
    return jnp.maximum(x, 0.0) + jnp.log1p(jnp.exp(-jnp.abs(x)))


def _seg_sum(x, bd):
    hi = x.astype(BF16)
    lo = (x - hi.astype(F32)).astype(BF16)
    return _dot(hi, bd) + _dot(lo, bd)


def _row_tiles(batch, seq, cap):
    if seq >= LANE:
        tt = min(seq, cap)
        assert seq % tt == 0
        return 1, tt
    assert seq % SUBLANE == 0
    return batch, seq


def _ada_kernel(c_ref, w_ref, b_ref, o_ref):
    s = _silu(c_ref[...])
    o_ref[...] = _dot(s.astype(BF16), w_ref[...]) + b_ref[...]


def _ada_mod(c, w_ada_bf, b_ada):
    bsz = c.shape[0]
    out = pl.pallas_call(
        _ada_kernel,
        out_shape=jax.ShapeDtypeStruct((bsz, 3 * D_MODEL), F32),
        compiler_params=pltpu.CompilerParams(vmem_limit_bytes=VMEM_LIMIT),
        name="ada_mod",
    )(c, w_ada_bf, b_ada.reshape(1, -1))
    return out.reshape(bsz, 1, 3 * D_MODEL)


N_MLA_IN, N_RWKV_IN = 10, 10
N_GATE_OUT = 5


def _front_kernel(x_ref, mod_ref, g_ref, w_ref, *refs):
    nb, tt, d = x_ref.shape
    mla_in = refs[:N_MLA_IN]
    rwkv_in = refs[N_MLA_IN:N_MLA_IN + N_RWKV_IN]
    outs = refs[N_MLA_IN + N_RWKV_IN:]
    zmla_out, cbz_out, ccx_out, zrw_out, gm_out = outs[:N_GATE_OUT]
    mla_out = outs[N_GATE_OUT:N_GATE_OUT + 6]
    rwkv_out = outs[N_GATE_OUT + 6:N_GATE_OUT + 15]
    buf = outs[N_GATE_OUT + 15]
    x = x_ref[...]
    ms = jnp.mean(x * x, axis=-1, keepdims=True)
    xn = x * lax.rsqrt(ms + RMS_EPS) * g_ref[...]
    shift = mod_ref[:, :, 0:D_MODEL]
    scale = mod_ref[:, :, D_MODEL:2 * D_MODEL]
    u = (xn * (1.0 + scale) + shift).reshape(nb * tt, d).astype(BF16)

    def seg(i):
        return _dot(u, w_ref[:, SEG_OFF[i]:SEG_OFF[i] + SEG_W[i]])

    zmla_out[...] = _silu(seg(3)).reshape(nb, tt, SEG_W[3]).astype(zmla_out.dtype)
    c4 = seg(4)
    cbz_out[...] = (c4[:, 0:D_CONV] * _silu(c4[:, 3 * D_CONV:])).reshape(nb, tt, D_CONV)
    ccx_out[...] = (c4[:, D_CONV:2 * D_CONV] * c4[:, 2 * D_CONV:3 * D_CONV]).reshape(nb, tt, D_CONV)
    zrw_out[...] = _silu(seg(6)).reshape(nb, tt, SEG_W[6]).astype(zrw_out.dtype)
    gm_out[...] = _sigmoid(seg(7)).reshape(nb, tt, SEG_W[7]).astype(gm_out.dtype)
    _mla_prep_body(seg(0), seg(1).reshape(nb, tt, KV_LORA), seg(2).reshape(nb, tt, KR_W), nb, tt,
                   *mla_in, *mla_out)
    _rwkv_prep_body(seg(5).reshape(nb, tt, SHIFT_W), *rwkv_in, *rwkv_out, buf)


def _front(x, mod, norm_g, w_in_p, tabs, q_norm_g, kv_norm_g, wn, wr, wrs, uk, q_dtype,
           shift_prev, mu, w0, w2p, a0, a2p, k_k, k_a, r_k, bd):
    bsz, seq, d = x.shape
    nb, tt = _row_tiles(bsz, seq, 256)
    grid = (bsz // nb, seq // tt)
    cos_q, sin_q, cos_k, sin_k = tabs
    row = lambda w: pl.BlockSpec((nb, tt, w), lambda b, t: (b, t, 0))
    tab = lambda w: pl.BlockSpec((tt, w), lambda b, t: (t, 0))
    full = lambda a: pl.BlockSpec(a.shape, lambda b, t: (0,) * a.ndim)
    hq = lambda w: pl.BlockSpec((nb, N_HEADS, tt, w), lambda b, t: (b, 0, t, 0))
    state = pl.BlockSpec((nb, 1, SHIFT_W), lambda b, t: (b, 0, 0))
    mla_consts = [q_norm_g.reshape(1, -1), kv_norm_g.reshape(1, -1), wn, wr, wrs, uk]
    rwkv_consts = [mu.reshape(1, -1), w0.reshape(1, -1), w2p, a0.reshape(1, -1), a2p,
                   k_k.reshape(1, -1), k_a.reshape(1, -1), r_k.reshape(1, -1), bd]
    assert 4 + len(mla_consts) == N_MLA_IN and len(rwkv_consts) + 1 == N_RWKV_IN
    g = norm_g.reshape(1, d)
    outs = pl.pallas_call(
        _front_kernel,
        grid=grid,
        in_specs=[row(d), pl.BlockSpec((nb, 1, 3 * d), lambda b, t: (b, 0, 0)), full(g), full(w_in_p),
                  tab(N_HEADS * QK_ROPE), tab(N_HEADS * QK_ROPE), tab(QK_ROPE), tab(QK_ROPE)]
                 + [full(a) for a in mla_consts] + [full(a) for a in rwkv_consts[:-1]] + [full(bd), state],
        out_specs=[row(SEG_W[3]), row(D_CONV), row(D_CONV), row(SEG_W[6]), row(SEG_W[7]),
                   hq(KV_LORA), hq(QK_ROPE), row(KV_LORA), row(QK_ROPE), row(KV_LORA), row(QK_ROPE)]
                  + [row(D_R)] * 8 + [state],
        out_shape=[jax.ShapeDtypeStruct((bsz, seq, SEG_W[3]), BF16),
                   jax.ShapeDtypeStruct((bsz, seq, D_CONV), F32),
                   jax.ShapeDtypeStruct((bsz, seq, D_CONV), F32),
                   jax.ShapeDtypeStruct((bsz, seq, SEG_W[6]), BF16),
                   jax.ShapeDtypeStruct((bsz, seq, SEG_W[7]), BF16)]
                  + [jax.ShapeDtypeStruct((bsz, N_HEADS, seq, KV_LORA), q_dtype),
                     jax.ShapeDtypeStruct((bsz, N_HEADS, seq, QK_ROPE), q_dtype),
                     jax.ShapeDtypeStruct((bsz, seq, KV_LORA), F32),
                     jax.ShapeDtypeStruct((bsz, seq, QK_ROPE), F32),
                     jax.ShapeDtypeStruct((bsz, seq, KV_LORA), BF16),
                     jax.ShapeDtypeStruct((bsz, seq, QK_ROPE), BF16)]
                  + [jax.ShapeDtypeStruct((bsz, seq, D_R), F32)] * 8
                  + [jax.ShapeDtypeStruct((bsz, 1, SHIFT_W), F32)],
        scratch_shapes=[pltpu.VMEM((nb, tt + SUBLANE, SHIFT_W), F32)],
        compiler_params=_cparams(("parallel", "arbitrary")),
        name="front",
    )(x, mod, g, w_in_p, cos_q, sin_q, cos_k, sin_k, *mla_consts, *rwkv_consts,
      shift_prev.reshape(bsz, 1, SHIFT_W))
    return outs


def _mla_prep_body(qa, kva, kr, nb, tt, cq_ref, sq_ref, ck_ref, sk_ref,
                   gq_ref, gkv_ref, wn_ref, wr_ref, wrs_ref, uk_ref,
                   qlat_ref, qpe_ref, ckv_ref, kpe_ref, ckvb_ref, kpeb_ref):
    rows = nb * tt
    cq = qa * lax.rsqrt(jnp.mean(qa * qa, axis=-1, keepdims=True) + RMS_EPS) * gq_ref[...]
    cqb = cq.astype(BF16)
    qn = _dot(cqb, wn_ref[...])
    cos_q = jnp.broadcast_to(cq_ref[...][None], (nb, tt, N_HEADS * QK_ROPE)).reshape(rows, -1)
    sin_q = jnp.broadcast_to(sq_ref[...][None], (nb, tt, N_HEADS * QK_ROPE)).reshape(rows, -1)
    qp = (_dot(cqb, wr_ref[...]) * cos_q + _dot(cqb, wrs_ref[...]) * sin_q) * Q_SCALE
    for h in range(N_HEADS):
        qn_h = qn[:, h * LANE:(h + 1) * LANE].astype(BF16)
        ql = _dot(qn_h, uk_ref[h]) * Q_SCALE
        qlat_ref[:, h] = ql.reshape(nb, tt, KV_LORA).astype(qlat_ref.dtype)
        qpe_ref[:, h] = qp[:, h * QK_ROPE:(h + 1) * QK_ROPE].reshape(nb, tt, QK_ROPE).astype(qpe_ref.dtype)
    ckv = kva * lax.rsqrt(jnp.mean(kva * kva, axis=-1, keepdims=True) + RMS_EPS) * gkv_ref[...]
    ckv_ref[...] = ckv
    ckvb_ref[...] = ckv.astype(BF16)
    kpe = kr[:, :, 0:QK_ROPE] * ck_ref[...][None] + kr[:, :, QK_ROPE:2 * QK_ROPE] * sk_ref[...][None]
    kpe_ref[...] = kpe
    kpeb_ref[...] = kpe.astype(BF16)


def _value_up_proj(o_lat, wuv_ref, t):
    tiles = []
    for j in range(N_HEADS // 2):
        h0, h1 = 2 * j, 2 * j + 1
        tiles.append(_dot(o_lat[h0 * t:(h0 + 1) * t].astype(BF16), wuv_ref[h0])
                     + _dot(o_lat[h1 * t:(h1 + 1) * t].astype(BF16), wuv_ref[h1]))
    return jnp.concatenate(tiles, axis=1)


def _softmax_update(s, m_old, l_old):
    m_new = jnp.maximum(m_old, jnp.max(s, axis=-1, keepdims=True))
    alpha = jnp.exp2(m_old - m_new)
    p = jnp.exp2(s - jnp.tile(m_new, (1, s.shape[1] // LANE)))
    l_new = alpha * l_old + jnp.sum(p, axis=-1, keepdims=True)
    return p, m_new, l_new, alpha


def _attn_prompt_kernel(ql_ref, qp_ref, ckv_ref, kpe_ref, wuv_ref, o_ref,
                        m_scr, l_scr, acc_scr, s_scr, p_scr, *, tq, tk, rc):
    qi = pl.program_id(1)
    rows = N_HEADS * tq
    ql = ql_ref[0].reshape(rows, KV_LORA)
    qp = qp_ref[0].reshape(rows, QK_ROPE)
    m_scr[...] = jnp.full(m_scr.shape, NEG_BIG, F32)
    l_scr[...] = jnp.zeros(l_scr.shape, F32)
    acc_scr[...] = jnp.zeros(acc_scr.shape, F32)
    n_kt = (qi * tq + tq + tk - 1) // tk

    def scores(kt, slot):
        start = pl.multiple_of(kt * tk, tk)
        s_scr[slot] = (_dot_nt(ql, ckv_ref[0, pl.ds(start, tk), :])
                       + _dot_nt(qp, kpe_ref[0, pl.ds(start, tk), :]))

    def softmax(kt, slot, masked):
        for c in range(rows // rc):
            rs = slice(c * rc, (c + 1) * rc)
            s = s_scr[slot, rs, :]
            if masked:
                q_pos = qi * tq + (c * rc) % tq + lax.broadcasted_iota(jnp.int32, (rc, tk), 0)
                k_pos = kt * tk + lax.broadcasted_iota(jnp.int32, (rc, tk), 1)
                s = jnp.where(k_pos <= q_pos, s, NEG_BIG)
            p, m_new, l_new, alpha = _softmax_update(s, m_scr[rs, :], l_scr[rs, :])
            m_scr[rs, :] = m_new
            l_scr[rs, :] = l_new
            p_scr[rs, :] = p.astype(BF16)
            acc_scr[rs, :] = jnp.tile(alpha, (1, KV_LORA // LANE)) * acc_scr[rs, :]

    def weighted_values(kt):
        start = pl.multiple_of(kt * tk, tk)
        acc_scr[...] += _dot(p_scr[...], ckv_ref[0, pl.ds(start, tk), :])

    last = n_kt - 1
    scores(last, 0)
    softmax(last, 0, True)
    scores(0, 1)
    weighted_values(last)

    def body(kt, carry):
        slot = (kt + 1) % 2
        softmax(kt, slot, False)
        scores(kt + 1, 1 - slot)
        weighted_values(kt)
        return carry

    lax.fori_loop(0, last - 1, body, 0)

    @pl.when(last >= 1)
    def _():
        softmax(last - 1, last % 2, False)
        weighted_values(last - 1)
    o_lat = acc_scr[...] / jnp.tile(l_scr[...], (1, KV_LORA // LANE))
    o_ref[0] = _value_up_proj(o_lat, wuv_ref, tq)


def _attn_prompt(qlat, qpe, ckv_bf, kpe_bf, wuv_pad):
    bsz, _, seq, _ = qlat.shape
    tq = 128
    tk = min(256, seq)
    assert tk % tq == 0 and seq % tk == 0
    rc = 64
    kern = functools.partial(_attn_prompt_kernel, tq=tq, tk=tk, rc=rc)
    rows = N_HEADS * tq
    return pl.pallas_call(
        kern,
        grid=(bsz, seq // tq),
        in_specs=[pl.BlockSpec((1, N_HEADS, tq, KV_LORA), lambda b, q: (b, 0, q, 0)),
                  pl.BlockSpec((1, N_HEADS, tq, QK_ROPE), lambda b, q: (b, 0, q, 0)),
                  pl.BlockSpec((1, seq, KV_LORA), lambda b, q: (b, 0, 0)),
                  pl.BlockSpec((1, seq, QK_ROPE), lambda b, q: (b, 0, 0)),
                  pl.BlockSpec(wuv_pad.shape, lambda b, q: (0, 0, 0))],
        out_specs=pl.BlockSpec((1, tq, D_MLA), lambda b, q: (b, q, 0)),
        out_shape=jax.ShapeDtypeStruct((bsz, seq, D_MLA), F32),
        scratch_shapes=[pltpu.VMEM((rows, LANE), F32), pltpu.VMEM((rows, LANE), F32),
                        pltpu.VMEM((rows, KV_LORA), F32),
                        pltpu.VMEM((2, rows, tk), F32), pltpu.VMEM((rows, tk), BF16)],
        compiler_params=_cparams(("parallel", "arbitrary")),
        name="attn_prompt",
    )(qlat, qpe, ckv_bf, kpe_bf, wuv_pad)


def _attn_sample_kernel(pt_ref, ql_ref, qp_ref, ckvn_ref, kpen_ref, wuv_ref, cache_ckv, cache_kpe_t,
                        o_ref, ckv_buf, kpe_buf, sem, *, layer, pp, cp, n_steps, ts):
    b = pl.program_id(0)
    nb = pl.num_programs(0)
    rows = N_HEADS * ts
    total = nb * n_steps
    ql = ql_ref[0].reshape(rows, KV_LORA).astype(BF16)
    qp = qp_ref[0].reshape(rows, QK_ROPE).astype(BF16)

    def page_copies(seq, grp, slot):
        out = []
        for i in range(pp):
            page = pt_ref[seq, grp * pp + i]
            out.append(pltpu.make_async_copy(cache_ckv.at[layer, page], ckv_buf.at[slot, i], sem.at[slot]))
            out.append(pltpu.make_async_copy(cache_kpe_t.at[layer, page], kpe_buf.at[slot, i], sem.at[slot]))
        return out

    def start_group(g, slot):
        g = jnp.minimum(g, total - 1)
        seq = lax.shift_right_logical(g, n_steps.bit_length() - 1)
        for c in page_copies(seq, jnp.bitwise_and(g, n_steps - 1), slot):
            c.start()

    def wait_group(slot):
        for c in page_copies(0, 0, slot):
            c.wait()

    @pl.when(b == 0)
    def _():
        start_group(0, 0)

    def partial_softmax(s, vals):
        m = jnp.max(s, axis=-1, keepdims=True)
        p = jnp.exp2(s - m)
        return m, jnp.sum(p, axis=-1, keepdims=True), _dot(p.astype(BF16), vals)

    def merge(parts):
        m = functools.reduce(jnp.maximum, [pm for pm, _, _ in parts])
        l = sum(pl_ * jnp.exp2(pm - m) for pm, pl_, _ in parts)
        acc = sum(pa * jnp.exp2(pm - m) for pm, _, pa in parts)
        return m, l, acc

    def group_parts(slot):
        n_c = pp // cp
        cks = [ckv_buf[slot, c * cp:(c + 1) * cp].reshape(cp * PAGE, KV_LORA).astype(BF16) for c in range(n_c)]
        kps = [jnp.concatenate([kpe_buf[slot, c * cp + i].astype(BF16) for i in range(cp)], axis=1)
               for c in range(n_c)]
        ss = [_dot_nt(ql, ck) + _dot(qp, kp_t) for ck, kp_t in zip(cks, kps)]
        ms = [jnp.max(s, axis=-1, keepdims=True) for s in ss]
        ps = [jnp.exp2(s - m) for s, m in zip(ss, ms)]
        ls = [jnp.sum(p, axis=-1, keepdims=True) for p in ps]
        accs = [_dot(p.astype(BF16), ck) for p, ck in zip(ps, cks)]
        return list(zip(ms, ls, accs))

    ckn = ckvn_ref[0]
    kpn = kpen_ref[0]
    s_new = _dot_nt(ql, ckn) + _dot_nt(qp, kpn)
    t_q = lax.broadcasted_iota(jnp.int32, (rows, ts), 0) % ts
    t_k = lax.broadcasted_iota(jnp.int32, (rows, ts), 1)
    state = partial_softmax(jnp.where(t_k <= t_q, s_new, NEG_BIG), ckn)

    for st in range(n_steps):
        slot = st % 2
        start_group(b * n_steps + st + 1, 1 - slot)
        wait_group(slot)
        state = merge([state] + group_parts(slot))

    @pl.when(b == nb - 1)
    def _():
        wait_group(n_steps % 2)

    _, l, acc = state
    o_ref[0] = _value_up_proj(acc / l, wuv_ref, ts)


def _attn_sample(page_table, qlat, qpe, ckv_new_bf, kpe_new_bf, wuv_pad, cache_ckv, cache_kpe, layer):
    bsz, _, ts, _ = qlat.shape
    n_pages = page_table.shape[1]
    pp = min(32, n_pages // 2)
    cp = min(8, pp)
    n_steps = n_pages // pp
    assert n_pages % pp == 0 and pp % cp == 0
    assert n_steps % 2 == 0 and n_steps & (n_steps - 1) == 0
    kern = functools.partial(_attn_sample_kernel, layer=layer, pp=pp, cp=cp, n_steps=n_steps, ts=ts)

    cache_kpe_t = jnp.swapaxes(cache_kpe, 2, 3)

    grid_spec = pltpu.PrefetchScalarGridSpec(
        num_scalar_prefetch=1,
        grid=(bsz,),
        in_specs=[pl.BlockSpec((1, N_HEADS, ts, KV_LORA), lambda b, pt: (b, 0, 0, 0)),
                  pl.BlockSpec((1, N_HEADS, ts, QK_ROPE), lambda b, pt: (b, 0, 0, 0)),
                  pl.BlockSpec((1, ts, KV_LORA), lambda b, pt: (b, 0, 0)),
                  pl.BlockSpec((1, ts, QK_ROPE), lambda b, pt: (b, 0, 0)),
                  pl.BlockSpec(wuv_pad.shape, lambda b, pt: (0, 0, 0)),
                  pl.BlockSpec(memory_space=pl.ANY),
                  pl.BlockSpec(memory_space=pl.ANY)],
        out_specs=pl.BlockSpec((1, ts, D_MLA), lambda b, pt: (b, 0, 0)),
        scratch_shapes=[pltpu.VMEM((2, pp, PAGE, KV_LORA), F32),
                        pltpu.VMEM((2, pp, QK_ROPE, PAGE), F32),
                        pltpu.SemaphoreType.DMA((2,))],
    )
    return pl.pallas_call(
        kern,
        grid_spec=grid_spec,
        out_shape=jax.ShapeDtypeStruct((bsz, ts, D_MLA), F32),
        compiler_params=_cparams(("arbitrary",)),
        name="attn_sample",
    )(page_table, qlat, qpe, ckv_new_bf, kpe_new_bf, wuv_pad, cache_ckv, cache_kpe_t)


def _rwkv_prep_body(rw, mu_ref, w0_ref, w2_ref, a0_ref, a2_ref, kk_ref, ka_ref, rk_ref, bd_ref, sprev_ref,
                    q_out, w_out, k_out, v_out, kk_out, b_out, vkr_out, bonus_out, shift_out, buf):
    nb, tt, _ = rw.shape
    ti = pl.program_id(1)
    rows = nb * tt

    @pl.when(ti == 0)
    def _():
        buf[:, SUBLANE - 1:SUBLANE, :] = sprev_ref[...]

    @pl.when(ti > 0)
    def _():
        buf[:, SUBLANE - 1:SUBLANE, :] = buf[:, tt + SUBLANE - 1:tt + SUBLANE, :]

    buf[:, SUBLANE:, :] = rw
    shift_out[...] = rw[:, tt - 1:tt, :]
    rw_prev = buf[:, SUBLANE - 1:SUBLANE - 1 + tt, :]
    rws = (rw + mu_ref[...] * (rw_prev - rw)).reshape(rows, SHIFT_W)
    r = rws[:, 0:D_R]
    k = rws[:, D_R:2 * D_R]
    v = rws[:, 2 * D_R:3 * D_R]
    wa = rws[:, 3 * D_R:]
    w_log = -_softplus(-(w0_ref[...] + _dot(jnp.tanh(wa).astype(BF16), w2_ref[...]))) - 0.5
    decay = jnp.exp(-jnp.exp(w_log))
    a = _sigmoid(a0_ref[...] + _dot(wa.astype(BF16), a2_ref[...]))
    bd = bd_ref[...]
    kk = k * kk_ref[...]
    kk = kk / jnp.maximum(jnp.sqrt(_seg_sum(kk * kk, bd)), 1e-12)
    k = k * (1.0 + (a - 1.0) * ka_ref[...])
    bonus = _seg_sum(r * k * rk_ref[...], bd) * v
    b = kk * a
    q = decay * r - kk * _seg_sum(b * r, bd)
    vkr = v * _seg_sum(k * r, bd)
    shp = (nb, tt, D_R)
    q_out[...] = q.reshape(shp)
    w_out[...] = decay.reshape(shp)
    k_out[...] = k.reshape(shp)
    v_out[...] = v.reshape(shp)
    kk_out[...] = kk.reshape(shp)
    b_out[...] = b.reshape(shp)
    vkr_out[...] = vkr.reshape(shp)
    bonus_out[...] = bonus.reshape(shp)


def _rwkv_scan_kernel(q_ref, w_ref, k_ref, v_ref, kk_ref, b_ref, vkr_ref, s0_ref, bd_ref, eye_ref,
                      y_ref, sT_ref, s_scr):
    nb, tc, _ = q_ref.shape
    ci = pl.program_id(1)
    rows = nb * R_HEAD

    @pl.when(ci == 0)
    def _():
        s_scr[...] = s0_ref[...]

    bd = bd_ref[...]
    eye = eye_ref[...][None]
    eye_bf = eye.astype(BF16)

    def seg(x):
        return _dot(x.reshape(rows, D_R), bd).reshape(nb, R_HEAD, D_R)

    def step(t, carry):
        row = lambda ref: ref[:, pl.ds(t, 1), :]
        s = s_scr[...]
        s_bf = s.astype(BF16)
        sa = seg(s_bf * row(kk_ref).astype(BF16))
        y_col = seg(s_bf * row(q_ref).astype(BF16))
        v_col = seg(eye_bf * row(v_ref).astype(BF16))
        s_scr[...] = s * row(w_ref) - sa * row(b_ref) + v_col * row(k_ref)
        y_ref[:, pl.ds(t, 1), :] = jnp.sum(y_col * eye, axis=1, keepdims=True) + row(vkr_ref)
        return carry

    lax.fori_loop(0, tc, step, 0, unroll=SCAN_UNROLL)

    @pl.when(ci == pl.num_programs(1) - 1)
    def _():
        sT_ref[...] = s_scr[...]


def _rwkv_scan(q, w, k, v, kk, b, vkr, s0, bd, eye):
    bsz, seq, _ = q.shape
    nb = math.gcd(bsz, 8)
    tc = min(seq, 256)
    row = pl.BlockSpec((nb, tc, D_R), lambda bi, c: (bi, c, 0))
    st = pl.BlockSpec((nb, R_HEAD, D_R), lambda bi, c: (bi, 0, 0))
    return pl.pallas_call(
        _rwkv_scan_kernel,
        grid=(bsz // nb, seq // tc),
        in_specs=[row] * 7 + [st, pl.BlockSpec(bd.shape, lambda bi, c: (0, 0)),
                              pl.BlockSpec(eye.shape, lambda bi, c: (0, 0))],
        out_specs=[row, st],
        out_shape=[jax.ShapeDtypeStruct((bsz, seq, D_R), F32),
                   jax.ShapeDtypeStruct((bsz, R_HEAD, D_R), F32)],
        scratch_shapes=[pltpu.VMEM((nb, R_HEAD, D_R), F32)],
        compiler_params=_cparams(("parallel", "arbitrary")),
        name="rwkv_scan",
    )(q, w, k, v, kk, b, vkr, s0, bd, eye)


def _out_kernel(x_ref, mod_ref, omla_ref, szmla_ref, cbz_ref, ccx_ref, cprev_ref, szrw_ref,
                yr_ref, bonus_ref, gm_ref, wmla_ref, wconv_ref, wrw_ref, wout_ref, cw_ref,
                gng_ref, gnb_ref, bd_ref, fg_ref, xo_ref, cstate_ref, buf, *, final):
    nb, tt, d = x_ref.shape
    ti = pl.program_id(1)
    rows = nb * tt

    @pl.when(ti == 0)
    def _():
        buf[:, SUBLANE - 2:SUBLANE, :] = cprev_ref[...]

    @pl.when(ti > 0)
    def _():
        buf[:, SUBLANE - 2:SUBLANE, :] = buf[:, tt + SUBLANE - 2:tt + SUBLANE, :]

    buf[:, SUBLANE:, :] = ccx_ref[...]
    cstate_ref[...] = buf[:, tt + SUBLANE - 2:tt + SUBLANE, :]
    conv = (buf[:, SUBLANE - 2:SUBLANE - 2 + tt, :] * cw_ref[0:1, :]
            + buf[:, SUBLANE - 1:SUBLANE - 1 + tt, :] * cw_ref[1:2, :]
            + buf[:, SUBLANE:, :] * cw_ref[2:3, :])
    y_conv = _dot((cbz_ref[...] * conv).reshape(rows, D_CONV).astype(BF16), wconv_ref[...])

    y_mla = _dot((omla_ref[...] * szmla_ref[...].astype(F32)).reshape(rows, D_MLA).astype(BF16), wmla_ref[...])

    bd = bd_ref[...]
    yr = yr_ref[...].reshape(rows, D_R)
    mu = _seg_sum(yr, bd) * (1.0 / R_HEAD)
    dy = yr - mu
    var = _seg_sum(dy * dy, bd) * (1.0 / R_HEAD)
    yn = dy * lax.rsqrt(var + GN_EPS) * gng_ref[...] + gnb_ref[...]
    o_rw = yn + bonus_ref[...].reshape(rows, D_R)
    y_rw = _dot((o_rw * szrw_ref[...].astype(F32).reshape(rows, D_R)).astype(BF16), wrw_ref[...])

    g = gm_ref[...].astype(F32).reshape(rows, N_BRANCH * d)
    merged = g[:, 0:d] * y_mla + g[:, d:2 * d] * y_conv + g[:, 2 * d:] * y_rw
    delta = _dot(merged.astype(BF16), wout_ref[...]).reshape(nb, tt, d)
    xo = x_ref[...] + mod_ref[:, :, 2 * d:] * delta
    if final:
        xo = xo * lax.rsqrt(jnp.mean(xo * xo, axis=-1, keepdims=True) + RMS_EPS) * fg_ref[...]
    xo_ref[...] = xo


def _out_proj(x, mod, o_mla, sz_mla, cbz, ccx, conv_prev, sz_rw, y_r, bonus, gm,
              wmla, wconv, wrw, wout, conv_w, gn_g, gn_b, bd, final_g, final):
    bsz, seq, d = x.shape
    nb, tt = _row_tiles(bsz, seq, 256)
    grid = (bsz // nb, seq // tt)
    row = lambda w: pl.BlockSpec((nb, tt, w), lambda b, t: (b, t, 0))
    full = lambda a: pl.BlockSpec(a.shape, lambda b, t: (0,) * a.ndim)
    consts = [wmla, wconv, wrw, wout, conv_w, gn_g.reshape(1, -1), gn_b.reshape(1, -1), bd,
              final_g.reshape(1, -1)]
    return pl.pallas_call(
        functools.partial(_out_kernel, final=final),
        grid=grid,
        in_specs=[row(d), pl.BlockSpec((nb, 1, 3 * d), lambda b, t: (b, 0, 0)),
                  row(D_MLA), row(D_MLA), row(D_CONV), row(D_CONV),
                  pl.BlockSpec((nb, CONV_W - 1, D_CONV), lambda b, t: (b, 0, 0)),
                  row(D_R), row(D_R), row(D_R), row(N_BRANCH * d)] + [full(a) for a in consts],
        out_specs=[row(d), pl.BlockSpec((nb, CONV_W - 1, D_CONV), lambda b, t: (b, 0, 0))],
        out_shape=[jax.ShapeDtypeStruct((bsz, seq, d), F32),
                   jax.ShapeDtypeStruct((bsz, CONV_W - 1, D_CONV), F32)],
        scratch_shapes=[pltpu.VMEM((nb, tt + SUBLANE, D_CONV), F32)],
        compiler_params=_cparams(("parallel", "arbitrary")),
        name="out_proj",
    )(x, mod, o_mla, sz_mla, cbz, ccx, conv_prev, sz_rw, y_r, bonus, gm, *consts)


def _swap_halves(w):
    half = QK_ROPE // 2
    return jnp.concatenate([w[..., half:], w[..., :half]], axis=-1)


def _layer_params(l, w_ada, w_in, w_q_b, w_uk, w_uv, w_mla_out, w_conv_out, rwkv_w2, rwkv_a2,
                  w_rwkv_out, w_out):
    d = D_MODEL
    offs = [0]
    for s in (Q_LORA, KV_LORA, QK_ROPE, D_MLA, D_CONV, D_CONV, D_CONV, D_CONV, SHIFT_W, D_R, N_BRANCH * d):
        offs.append(offs[-1] + s)
    wi = w_in[l]
    k_rope = wi[:, offs[2]:offs[3]]
    kr = jnp.concatenate([k_rope, _swap_halves(k_rope), jnp.zeros((d, KR_W - 2 * QK_ROPE), BF16)], axis=1)
    w_in_p = jnp.concatenate([wi[:, offs[0]:offs[2]], kr, wi[:, offs[3]:]], axis=1)
    wq = w_q_b[l].reshape(Q_LORA, N_HEADS, QK_NOPE + QK_ROPE)
    wn = jnp.pad(wq[:, :, :QK_NOPE], ((0, 0), (0, 0), (0, LANE - QK_NOPE))).reshape(Q_LORA, N_HEADS * LANE)
    wr = wq[:, :, QK_NOPE:]
    wrs = _swap_halves(wr)
    uk = jnp.pad(jnp.transpose(w_uk[l], (1, 2, 0)), ((0, 0), (0, LANE - QK_NOPE), (0, 0)))
    wuv = jnp.transpose(w_uv[l], (1, 0, 2))
    zv = jnp.zeros_like(wuv)
    even = (jnp.arange(N_HEADS) % 2 == 0)[:, None, None]
    wuv_pad = jnp.where(even, jnp.concatenate([wuv, zv], axis=2), jnp.concatenate([zv, wuv], axis=2))
    zeros = jnp.zeros((W_LORA, D_R), F32)
    return dict(
        w_ada=w_ada[l].astype(BF16), w_in_p=w_in_p,
        wn=wn.astype(BF16), wr=wr.reshape(Q_LORA, -1).astype(BF16), wrs=wrs.reshape(Q_LORA, -1).astype(BF16),
        uk=uk.astype(BF16), wuv_pad=wuv_pad.astype(BF16),
        wmla=w_mla_out[l].astype(BF16), wconv=w_conv_out[l].astype(BF16),
        wrw=w_rwkv_out[l].astype(BF16), wout=w_out[l].astype(BF16),
        w2p=jnp.concatenate([rwkv_w2[l], zeros], axis=0).astype(BF16),
        a2p=jnp.concatenate([zeros, rwkv_a2[l]], axis=0).astype(BF16),
    )


def _rope_tables(pos):
    half = QK_ROPE // 2
    inv = ROPE_THETA ** (-jnp.arange(half, dtype=F32) / half)
    ang = pos.astype(F32)[:, None] * inv[None, :]
    cos, sin = jnp.cos(ang), jnp.sin(ang)
    cos_k = jnp.concatenate([cos, cos], axis=1)
    sin_k = jnp.concatenate([-sin, sin], axis=1)
    return jnp.tile(cos_k, (1, N_HEADS)), jnp.tile(sin_k, (1, N_HEADS)), cos_k, sin_k


def _state_to_lanes(s):
    b = s.shape[0]
    return jnp.transpose(s, (0, 2, 1, 3)).reshape(b, R_HEAD, D_R)


def _state_from_lanes(s):
    b = s.shape[0]
    return jnp.transpose(s.reshape(b, R_HEAD, R_HEADS, R_HEAD), (0, 2, 1, 3))


def _layer(x, c_mod, tabs, p, vecs, conv_prev, shift_prev, s0, consts, final_g, final, attend, q_dtype):
    bd, eye = consts
    (sz_mla, cbz, ccx, sz_rw, gm, qlat, qpe, ckv, kpe, ckv_bf, kpe_bf,
     q, w, k, v, kk, b, vkr, bonus, shift_state) = _front(
        x, c_mod, vecs["norm_g"], p["w_in_p"], tabs, vecs["q_norm_g"], vecs["kv_norm_g"],
        p["wn"], p["wr"], p["wrs"], p["uk"], q_dtype,
        shift_prev, vecs["mu"], vecs["w0"], p["w2p"], vecs["a0"], p["a2p"],
        vecs["k_k"], vecs["k_a"], vecs["r_k"], bd)
    o_mla = attend(qlat, qpe, ckv_bf, kpe_bf, p["wuv_pad"])
    y_r, s_new = _rwkv_scan(q, w, k, v, kk, b, vkr, s0, bd, eye)
    x_new, conv_state = _out_proj(
        x, c_mod, o_mla, sz_mla, cbz, ccx, conv_prev, sz_rw, y_r, bonus, gm,
        p["wmla"], p["wconv"], p["wrw"], p["wout"], vecs["conv_w"], vecs["gn_g"], vecs["gn_b"], bd,
        final_g, final)
    return x_new, ckv, kpe, conv_state, shift_state[:, 0], s_new


def kernel(x_prompt, x_sample, cache_mla_ckv, cache_mla_kpe, state_conv, state_rwkv_shift, state_rwkv, page_table, c_prompt, c_sample, norm_g, w_ada, b_ada, w_in, q_norm_g, w_q_b, kv_norm_g, w_uk, w_uv, w_mla_out, conv_w, w_conv_out, rwkv_mu, rwkv_w0, rwkv_w2, rwkv_a0, rwkv_a2, rwkv_k_k, rwkv_k_a, rwkv_r_k, rwkv_gn_g, rwkv_gn_b, w_rwkv_out, w_out, final_norm_g):
    depth = norm_g.shape[0]
    bp, tp, _ = x_prompt.shape
    bs, ts, _ = x_sample.shape
    past = page_table.shape[1] * PAGE
    tabs_p = _rope_tables(jnp.arange(tp, dtype=jnp.int32))
    tabs_s = _rope_tables(past + jnp.arange(ts, dtype=jnp.int32))
    seg = jnp.arange(D_R, dtype=jnp.int32) // R_HEAD
    bd = (seg[:, None] == seg[None, :]).astype(BF16)
    eye = (jnp.arange(R_HEAD, dtype=jnp.int32)[:, None] == (jnp.arange(D_R, dtype=jnp.int32) % R_HEAD)[None, :]).astype(F32)
    consts = (bd, eye)
    w_in = w_in.astype(BF16)
    xp, xs = x_prompt, x_sample
    st_p = ([], [], [], [], [])
    st_s = ([], [], [], [], [])
    for l in range(depth):
        final = l == depth - 1
        p = _layer_params(l, w_ada, w_in, w_q_b, w_uk, w_uv, w_mla_out, w_conv_out, rwkv_w2, rwkv_a2,
                          w_rwkv_out, w_out)
        vecs = dict(norm_g=norm_g[l], q_norm_g=q_norm_g[l], kv_norm_g=kv_norm_g[l], mu=rwkv_mu[l],
                    w0=rwkv_w0[l], a0=rwkv_a0[l], k_k=rwkv_k_k[l], k_a=rwkv_k_a[l],
                    r_k=rwkv_r_k[l].reshape(-1), conv_w=conv_w[l], gn_g=rwkv_gn_g[l], gn_b=rwkv_gn_b[l])
        mod_p = _ada_mod(c_prompt, p["w_ada"], b_ada[l])
        mod_s = _ada_mod(c_sample, p["w_ada"], b_ada[l])
        xp, ckv, kpe, cst, sst, rst = _layer(
            xp, mod_p, tabs_p, p, vecs,
            jnp.zeros((bp, CONV_W - 1, D_CONV), F32), jnp.zeros((bp, SHIFT_W), F32),
            jnp.zeros((bp, R_HEAD, D_R), F32), consts, final_norm_g, final, _attn_prompt, BF16)
        for lst, val in zip(st_p, (ckv, kpe, cst, sst, _state_from_lanes(rst))):
            lst.append(val)
        attend_s = functools.partial(_attn_sample_bound, page_table, cache_mla_ckv, cache_mla_kpe, l)
        xs, ckv, kpe, cst, sst, rst = _layer(
            xs, mod_s, tabs_s, p, vecs, state_conv[l], state_rwkv_shift[l],
            _state_to_lanes(state_rwkv[l]), consts, final_norm_g, final, attend_s, F32)
        for lst, val in zip(st_s, (ckv, kpe, cst, sst, _state_from_lanes(rst))):
            lst.append(val)
    return (xp, xs,
            jnp.stack(st_p[0]), jnp.stack(st_p[1]), jnp.stack(st_p[2]), jnp.stack(st_p[3]), jnp.stack(st_p[4]),
            jnp.stack(st_s[0]), jnp.stack(st_s[1]), jnp.stack(st_s[2]), jnp.stack(st_s[3]), jnp.stack(st_s[4]))


def _attn_sample_bound(page_table, cache_ckv, cache_kpe, layer, qlat, qpe, ckv_bf, kpe_bf, wuv_pad):
    return _attn_sample(page_table, qlat, qpe, ckv_bf, kpe_bf, wuv_pad, cache_ckv, cache_kpe, layer)
```

```python
import functools
import math

import jax
import jax.numpy as jnp
from jax import lax
from jax.experimental import pallas as pl
from jax.experimental.pallas import tpu as pltpu

F32 = jnp.float32
BF16 = jnp.bfloat16

D_MODEL = 1024
PAGE = 128
N_HEADS = 8
QK_NOPE = 64
QK_ROPE = 32
V_HEAD = 64
Q_LORA = 384
KV_LORA = 256
D_MLA = N_HEADS * V_HEAD
ROPE_THETA = 10000.0
ATTN_SCALE = (QK_NOPE + QK_ROPE) ** -0.5
Q_SCALE = ATTN_SCALE * math.log2(math.e)
D_CONV = 256
CONV_W = 3
R_HEADS = 4
R_HEAD = 64
D_R = R_HEADS * R_HEAD
W_LORA = 64
A_LORA = 64
SHIFT_W = 3 * D_R + W_LORA + A_LORA
GN_EPS = 64e-5
RMS_EPS = 1e-6
N_BRANCH = 3

LANE = 128
SUBLANE = 8
KR_W = LANE
SEG_W = (Q_LORA, KV_LORA, KR_W, D_MLA, 4 * D_CONV, SHIFT_W, D_R, N_BRANCH * D_MODEL)
SEG_OFF = tuple(sum(SEG_W[:i]) for i in range(len(SEG_W)))
PROJ_P = sum(SEG_W)
VMEM_LIMIT = 56 * 1024 * 1024
NEG_BIG = -1e30
SCAN_UNROLL = 8
SCAN_GROUPS = 2
PAGE_SLOTS = 3


def _cparams(sem):
    return pltpu.CompilerParams(dimension_semantics=sem, vmem_limit_bytes=VMEM_LIMIT)


def _dot(a, b):
    return jnp.dot(a, b, preferred_element_type=F32)


def _dot_nt(a, b):
    return lax.dot_general(a, b, (((1,), (1,)), ((), ())), preferred_element_type=F32)


def _sigmoid(x):
    return 1.0 / (1.0 + jnp.exp(-x))


def _silu(x):
    return x * _sigmoid(x)


def _softplus(x):
    return jnp.maximum(x, 0.0) + jnp.log1p(jnp.exp(-jnp.abs(x)))


def _seg_sum(x, bd):
    hi = x.astype(BF16)
    lo = (x - hi.astype(F32)).astype(BF16)
    return _dot(hi, bd) + _dot(lo, bd)


def _row_tiles(batch, seq, cap):
    if seq >= LANE:
        tt = min(seq, cap)
        assert seq % tt == 0
        return 1, tt
    assert seq % SUBLANE == 0
    return batch, seq


def _ada_kernel(c_ref, w_ref, b_ref, o_ref):
    s = _silu(c_ref[...])
    o_ref[...] = _dot(s.astype(BF16), w_ref[...]) + b_ref[...]


def _ada_mod(c, w_ada_bf, b_ada):
    bsz = c.shape[0]
    out = pl.pallas_call(
        _ada_kernel,
        out_shape=jax.ShapeDtypeStruct((bsz, 3 * D_MODEL), F32),
        compiler_params=pltpu.CompilerParams(vmem_limit_bytes=VMEM_LIMIT),
        name="ada_mod",
    )(c, w_ada_bf, b_ada.reshape(1, -1))
    return out.reshape(bsz, 1, 3 * D_MODEL)


N_MLA_IN, N_RWKV_IN = 10, 10
N_GATE_OUT = 5


def _front_kernel(x_ref, mod_ref, g_ref, w_ref, *refs):
    nb, tt, d = x_ref.shape
    mla_in = refs[:N_MLA_IN]
    rwkv_in = refs[N_MLA_IN:N_MLA_IN + N_RWKV_IN]
    outs = refs[N_MLA_IN + N_RWKV_IN:]
    zmla_out, cbz_out, ccx_out, zrw_out, gm_out = outs[:N_GATE_OUT]
    mla_out = outs[N_GATE_OUT:N_GATE_OUT + 6]
    rwkv_out = outs[N_GATE_OUT + 6:N_GATE_OUT + 15]
    buf = outs[N_GATE_OUT + 15]
    x = x_ref[...]
    ms = jnp.mean(x * x, axis=-1, keepdims=True)
    xn = x * lax.rsqrt(ms + RMS_EPS) * g_ref[...]
    shift = mod_ref[:, :, 0:D_MODEL]
    scale = mod_ref[:, :, D_MODEL:2 * D_MODEL]
    u = (xn * (1.0 + scale) + shift).reshape(nb * tt, d).astype(BF16)

    def seg(i):
        return _dot(u, w_ref[:, SEG_OFF[i]:SEG_OFF[i] + SEG_W[i]])

    zmla_out[...] = _silu(seg(3)).reshape(nb, tt, SEG_W[3]).astype(zmla_out.dtype)
    c4 = seg(4)
    cbz_out[...] = (c4[:, 0:D_CONV] * _silu(c4[:, 3 * D_CONV:])).reshape(nb, tt, D_CONV)
    ccx_out[...] = (c4[:, D_CONV:2 * D_CONV] * c4[:, 2 * D_CONV:3 * D_CONV]).reshape(nb, tt, D_CONV)
    zrw_out[...] = _silu(seg(6)).reshape(nb, tt, SEG_W[6]).astype(zrw_out.dtype)
    gm_out[...] = _sigmoid(seg(7)).reshape(nb, tt, SEG_W[7]).astype(gm_out.dtype)
    _mla_prep_body(seg(0), seg(1).reshape(nb, tt, KV_LORA), seg(2).reshape(nb, tt, KR_W), nb, tt,
                   *mla_in, *mla_out)
    _rwkv_prep_body(seg(5).reshape(nb, tt, SHIFT_W), *rwkv_in, *rwkv_out, buf)


def _front(x, mod, norm_g, w_in_p, tabs, q_norm_g, kv_norm_g, wn, wr, wrs, uk, q_dtype,
           shift_prev, mu, w0, w2p, a0, a2p, k_k, k_a, r_k, bd):
    bsz, seq, d = x.shape
    nb, tt = _row_tiles(bsz, seq, 256)
    grid = (bsz // nb, seq // tt)
    cos_q, sin_q, cos_k, sin_k = tabs
    row = lambda w: pl.BlockSpec((nb, tt, w), lambda b, t: (b, t, 0))
    tab = lambda w: pl.BlockSpec((tt, w), lambda b, t: (t, 0))
    full = lambda a: pl.BlockSpec(a.shape, lambda b, t: (0,) * a.ndim)
    hq = lambda w: pl.BlockSpec((nb, N_HEADS, tt, w), lambda b, t: (b, 0, t, 0))
    state = pl.BlockSpec((nb, 1, SHIFT_W), lambda b, t: (b, 0, 0))
    mla_consts = [q_norm_g.reshape(1, -1), kv_norm_g.reshape(1, -1), wn, wr, wrs, uk]
    rwkv_consts = [mu.reshape(1, -1), w0.reshape(1, -1), w2p, a0.reshape(1, -1), a2p,
                   k_k.reshape(1, -1), k_a.reshape(1, -1), r_k.reshape(1, -1), bd]
    assert 4 + len(mla_consts) == N_MLA_IN and len(rwkv_consts) + 1 == N_RWKV_IN
    g = norm_g.reshape(1, d)
    outs = pl.pallas_call(
        _front_kernel,
        grid=grid,
        in_specs=[row(d), pl.BlockSpec((nb, 1, 3 * d), lambda b, t: (b, 0, 0)), full(g), full(w_in_p),
                  tab(N_HEADS * QK_ROPE), tab(N_HEADS * QK_ROPE), tab(QK_ROPE), tab(QK_ROPE)]
                 + [full(a) for a in mla_consts] + [full(a) for a in rwkv_consts[:-1]] + [full(bd), state],
        out_specs=[row(SEG_W[3]), row(D_CONV), row(D_CONV), row(SEG_W[6]), row(SEG_W[7]),
                   hq(KV_LORA), hq(QK_ROPE), row(KV_LORA), row(QK_ROPE), row(KV_LORA), row(QK_ROPE)]
                  + [row(D_R)] * 8 + [state],
        out_shape=[jax.ShapeDtypeStruct((bsz, seq, SEG_W[3]), BF16),
                   jax.ShapeDtypeStruct((bsz, seq, D_CONV), F32),
                   jax.ShapeDtypeStruct((bsz, seq, D_CONV), F32),
                   jax.ShapeDtypeStruct((bsz, seq, SEG_W[6]), BF16),
                   jax.ShapeDtypeStruct((bsz, seq, SEG_W[7]), BF16)]
                  + [jax.ShapeDtypeStruct((bsz, N_HEADS, seq, KV_LORA), q_dtype),
                     jax.ShapeDtypeStruct((bsz, N_HEADS, seq, QK_ROPE), q_dtype),
                     jax.ShapeDtypeStruct((bsz, seq, KV_LORA), F32),
                     jax.ShapeDtypeStruct((bsz, seq, QK_ROPE), F32),
                     jax.ShapeDtypeStruct((bsz, seq, KV_LORA), BF16),
                     jax.ShapeDtypeStruct((bsz, seq, QK_ROPE), BF16)]
                  + [jax.ShapeDtypeStruct((bsz, seq, D_R), F32)] * 8
                  + [jax.ShapeDtypeStruct((bsz, 1, SHIFT_W), F32)],
        scratch_shapes=[pltpu.VMEM((nb, tt + SUBLANE, SHIFT_W), F32)],
        compiler_params=_cparams(("parallel", "arbitrary")),
        name="front",
    )(x, mod, g, w_in_p, cos_q, sin_q, cos_k, sin_k, *mla_consts, *rwkv_consts,
      shift_prev.reshape(bsz, 1, SHIFT_W))
    return outs


def _mla_prep_body(qa, kva, kr, nb, tt, cq_ref, sq_ref, ck_ref, sk_ref,
                   gq_ref, gkv_ref, wn_ref, wr_ref, wrs_ref, uk_ref,
                   qlat_ref, qpe_ref, ckv_ref, kpe_ref, ckvb_ref, kpeb_ref):
    rows = nb * tt
    cq = qa * lax.rsqrt(jnp.mean(qa * qa, axis=-1, keepdims=True) + RMS_EPS) * gq_ref[...]
    cqb = cq.astype(BF16)
    qn = _dot(cqb, wn_ref[...])
    cos_q = jnp.broadcast_to(cq_ref[...][None], (nb, tt, N_HEADS * QK_ROPE)).reshape(rows, -1)
    sin_q = jnp.broadcast_to(sq_ref[...][None], (nb, tt, N_HEADS * QK_ROPE)).reshape(rows, -1)
    qp = (_dot(cqb, wr_ref[...]) * cos_q + _dot(cqb, wrs_ref[...]) * sin_q) * Q_SCALE
    for h in range(N_HEADS):
        qn_h = qn[:, h * LANE:(h + 1) * LANE].astype(BF16)
        ql = _dot(qn_h, uk_ref[h]) * Q_SCALE
        qlat_ref[:, h] = ql.reshape(nb, tt, KV_LORA).astype(qlat_ref.dtype)
        qpe_ref[:, h] = qp[:, h * QK_ROPE:(h + 1) * QK_ROPE].reshape(nb, tt, QK_ROPE).astype(qpe_ref.dtype)
    ckv = kva * lax.rsqrt(jnp.mean(kva * kva, axis=-1, keepdims=True) + RMS_EPS) * gkv_ref[...]
    ckv_ref[...] = ckv
    ckvb_ref[...] = ckv.astype(BF16)
    kpe = kr[:, :, 0:QK_ROPE] * ck_ref[...][None] + kr[:, :, QK_ROPE:2 * QK_ROPE] * sk_ref[...][None]
    kpe_ref[...] = kpe
    kpeb_ref[...] = kpe.astype(BF16)


def _value_up_proj(o_lat, wuv_ref, t):
    tiles = []
    for j in range(N_HEADS // 2):
        h0, h1 = 2 * j, 2 * j + 1
        tiles.append(_dot(o_lat[h0 * t:(h0 + 1) * t].astype(BF16), wuv_ref[h0])
                     + _dot(o_lat[h1 * t:(h1 + 1) * t].astype(BF16), wuv_ref[h1]))
    return jnp.concatenate(tiles, axis=1)


def _softmax_update(s, m_old, l_old):
    m_new = jnp.maximum(m_old, jnp.max(s, axis=-1, keepdims=True))
    alpha = jnp.exp2(m_old - m_new)
    p = jnp.exp2(s - jnp.tile(m_new, (1, s.shape[1] // LANE)))
    l_new = alpha * l_old + jnp.sum(p, axis=-1, keepdims=True)
    return p, m_new, l_new, alpha


def _attn_prompt_kernel(ql_ref, qp_ref, ckv_ref, kpe_ref, wuv_ref, o_ref,
                        m_scr, l_scr, acc_scr, s_scr, p_scr, *, tq, tk, rc):
    qi = pl.program_id(1)
    rows = N_HEADS * tq
    ql = ql_ref[0].reshape(rows, KV_LORA)
    qp = qp_ref[0].reshape(rows, QK_ROPE)
    m_scr[...] = jnp.full(m_scr.shape, NEG_BIG, F32)
    l_scr[...] = jnp.zeros(l_scr.shape, F32)
    acc_scr[...] = jnp.zeros(acc_scr.shape, F32)
    n_kt = (qi * tq + tq + tk - 1) // tk

    def scores(kt, slot):
        start = pl.multiple_of(kt * tk, tk)
        s_scr[slot] = (_dot_nt(ql, ckv_ref[0, pl.ds(start, tk), :])
                       + _dot_nt(qp, kpe_ref[0, pl.ds(start, tk), :]))

    def softmax(kt, slot, masked):
        for c in range(rows // rc):
            rs = slice(c * rc, (c + 1) * rc)
            s = s_scr[slot, rs, :]
            if masked:
                q_pos = qi * tq + (c * rc) % tq + lax.broadcasted_iota(jnp.int32, (rc, tk), 0)
                k_pos = kt * tk + lax.broadcasted_iota(jnp.int32, (rc, tk), 1)
                s = jnp.where(k_pos <= q_pos, s, NEG_BIG)
            p, m_new, l_new, alpha = _softmax_update(s, m_scr[rs, :], l_scr[rs, :])
            m_scr[rs, :] = m_new
            l_scr[rs, :] = l_new
            p_scr[rs, :] = p.astype(BF16)
            acc_scr[rs, :] = jnp.tile(alpha, (1, KV_LORA // LANE)) * acc_scr[rs, :]

    def weighted_values(kt):
        start = pl.multiple_of(kt * tk, tk)
        acc_scr[...] += _dot(p_scr[...], ckv_ref[0, pl.ds(start, tk), :])

    last = n_kt - 1
    scores(last, 0)
    softmax(last, 0, True)
    scores(0, 1)
    weighted_values(last)

    def body(kt, carry):
        slot = (kt + 1) % 2
        softmax(kt, slot, False)
        scores(kt + 1, 1 - slot)
        weighted_values(kt)
        return carry

    lax.fori_loop(0, last - 1, body, 0)

    @pl.when(last >= 1)
    def _():
        softmax(last - 1, last % 2, False)
        weighted_values(last - 1)
    o_lat = acc_scr[...] / jnp.tile(l_scr[...], (1, KV_LORA // LANE))
    o_ref[0] = _value_up_proj(o_lat, wuv_ref, tq)


def _attn_prompt(qlat, qpe, ckv_bf, kpe_bf, wuv_pad):
    bsz, _, seq, _ = qlat.shape
    tq = 128
    tk = min(256, seq)
    assert tk % tq == 0 and seq % tk == 0
    rc = 64
    kern = functools.partial(_attn_prompt_kernel, tq=tq, tk=tk, rc=rc)
    rows = N_HEADS * tq
    return pl.pallas_call(
        kern,
        grid=(bsz, seq // tq),
        in_specs=[pl.BlockSpec((1, N_HEADS, tq, KV_LORA), lambda b, q: (b, 0, q, 0)),
                  pl.BlockSpec((1, N_HEADS, tq, QK_ROPE), lambda b, q: (b, 0, q, 0)),
                  pl.BlockSpec((1, seq, KV_LORA), lambda b, q: (b, 0, 0)),
                  pl.BlockSpec((1, seq, QK_ROPE), lambda b, q: (b, 0, 0)),
                  pl.BlockSpec(wuv_pad.shape, lambda b, q: (0, 0, 0))],
        out_specs=pl.BlockSpec((1, tq, D_MLA), lambda b, q: (b, q, 0)),
        out_shape=jax.ShapeDtypeStruct((bsz, seq, D_MLA), F32),
        scratch_shapes=[pltpu.VMEM((rows, LANE), F32), pltpu.VMEM((rows, LANE), F32),
                        pltpu.VMEM((rows, KV_LORA), F32),
                        pltpu.VMEM((2, rows, tk), F32), pltpu.VMEM((rows, tk), BF16)],
        compiler_params=_cparams(("parallel", "arbitrary")),
        name="attn_prompt",
    )(qlat, qpe, ckv_bf, kpe_bf, wuv_pad)


def _attn_sample_kernel(pt_ref, ql_ref, qp_ref, ckvn_ref, kpen_ref, wuv_ref, cache_ckv, cache_kpe_t,
                        o_ref, ckv_buf, kpe_buf, sem, *, layer, pp, cp, n_steps, ts):
    ahead = PAGE_SLOTS - 1
    b = pl.program_id(0)
    nb = pl.num_programs(0)
    rows = N_HEADS * ts
    total = nb * n_steps
    ql = ql_ref[0].reshape(rows, KV_LORA).astype(BF16)
    qp = qp_ref[0].reshape(rows, QK_ROPE).astype(BF16)

    def page_copies(seq, grp, slot):
        out = []
        for i in range(pp):
            page = pt_ref[seq, grp * pp + i]
            out.append(pltpu.make_async_copy(cache_ckv.at[layer, page], ckv_buf.at[slot, i], sem.at[slot]))
            out.append(pltpu.make_async_copy(cache_kpe_t.at[layer, page], kpe_buf.at[slot, i], sem.at[slot]))
        return out

    def start_group(g):
        slot = lax.rem(g, PAGE_SLOTS)
        g = jnp.minimum(g, total - 1)
        seq = lax.shift_right_logical(g, n_steps.bit_length() - 1)
        for c in page_copies(seq, jnp.bitwise_and(g, n_steps - 1), slot):
            c.start()

    def wait_group(g):
        for c in page_copies(0, 0, lax.rem(g, PAGE_SLOTS)):
            c.wait()

    @pl.when(b == 0)
    def _():
        for g in range(ahead):
            start_group(jnp.int32(g))

    def partial_softmax(s, vals):
        m = jnp.max(s, axis=-1, keepdims=True)
        p = jnp.exp2(s - m)
        return m, jnp.sum(p, axis=-1, keepdims=True), _dot(p.astype(BF16), vals)

    def merge(parts):
        m = functools.reduce(jnp.maximum, [pm for pm, _, _ in parts])
        l = sum(pl_ * jnp.exp2(pm - m) for pm, pl_, _ in parts)
        acc = sum(pa * jnp.exp2(pm - m) for pm, _, pa in parts)
        return m, l, acc

    def group_parts(slot):
        n_c = pp // cp
        cks = [ckv_buf[slot, c * cp:(c + 1) * cp].reshape(cp * PAGE, KV_LORA).astype(BF16) for c in range(n_c)]
        kps = [jnp.concatenate([kpe_buf[slot, c * cp + i].astype(BF16) for i in range(cp)], axis=1)
               for c in range(n_c)]
        ss = [_dot_nt(ql, ck) + _dot(qp, kp_t) for ck, kp_t in zip(cks, kps)]
        ms = [jnp.max(s, axis=-1, keepdims=True) for s in ss]
        ps = [jnp.exp2(s - m) for s, m in zip(ss, ms)]
        ls = [jnp.sum(p, axis=-1, keepdims=True) for p in ps]
        accs = [_dot(p.astype(BF16), ck) for p, ck in zip(ps, cks)]
        return list(zip(ms, ls, accs))

    ckn = ckvn_ref[0]
    kpn = kpen_ref[0]
    s_new = _dot_nt(ql, ckn) + _dot_nt(qp, kpn)
    t_q = lax.broadcasted_iota(jnp.int32, (rows, ts), 0) % ts
    t_k = lax.broadcasted_iota(jnp.int32, (rows, ts), 1)
    state = partial_softmax(jnp.where(t_k <= t_q, s_new, NEG_BIG), ckn)

    for st in range(n_steps):
        g = b * n_steps + st
        start_group(g + ahead)
        wait_group(g)
        state = merge([state] + group_parts(lax.rem(g, PAGE_SLOTS)))

    @pl.when(b == nb - 1)
    def _():
        for g in range(ahead):
            wait_group(total + g)

    _, l, acc = state
    o_ref[0] = _value_up_proj(acc / l, wuv_ref, ts)


def _attn_sample(page_table, qlat, qpe, ckv_new_bf, kpe_new_bf, wuv_pad, cache_ckv, cache_kpe, layer):
    bsz, _, ts, _ = qlat.shape
    n_pages = page_table.shape[1]
    pp = min(32, n_pages // 2)
    cp = min(8, pp)
    n_steps = n_pages // pp
    assert n_pages % pp == 0 and pp % cp == 0
    assert n_steps & (n_steps - 1) == 0
    kern = functools.partial(_attn_sample_kernel, layer=layer, pp=pp, cp=cp, n_steps=n_steps, ts=ts)

    cache_kpe_t = jnp.swapaxes(cache_kpe, 2, 3)

    grid_spec = pltpu.PrefetchScalarGridSpec(
        num_scalar_prefetch=1,
        grid=(bsz,),
        in_specs=[pl.BlockSpec((1, N_HEADS, ts, KV_LORA), lambda b, pt: (b, 0, 0, 0)),
                  pl.BlockSpec((1, N_HEADS, ts, QK_ROPE), lambda b, pt: (b, 0, 0, 0)),
                  pl.BlockSpec((1, ts, KV_LORA), lambda b, pt: (b, 0, 0)),
                  pl.BlockSpec((1, ts, QK_ROPE), lambda b, pt: (b, 0, 0)),
                  pl.BlockSpec(wuv_pad.shape, lambda b, pt: (0, 0, 0)),
                  pl.BlockSpec(memory_space=pl.ANY),
                  pl.BlockSpec(memory_space=pl.ANY)],
        out_specs=pl.BlockSpec((1, ts, D_MLA), lambda b, pt: (b, 0, 0)),
        scratch_shapes=[pltpu.VMEM((PAGE_SLOTS, pp, PAGE, KV_LORA), F32),
                        pltpu.VMEM((PAGE_SLOTS, pp, QK_ROPE, PAGE), F32),
                        pltpu.SemaphoreType.DMA((PAGE_SLOTS,))],
    )
    return pl.pallas_call(
        kern,
        grid_spec=grid_spec,
        out_shape=jax.ShapeDtypeStruct((bsz, ts, D_MLA), F32),
        compiler_params=_cparams(("arbitrary",)),
        name="attn_sample",
    )(page_table, qlat, qpe, ckv_new_bf, kpe_new_bf, wuv_pad, cache_ckv, cache_kpe_t)


def _rwkv_prep_body(rw, mu_ref, w0_ref, w2_ref, a0_ref, a2_ref, kk_ref, ka_ref, rk_ref, bd_ref, sprev_ref,
                    q_out, w_out, k_out, v_out, kk_out, b_out, vkr_out, bonus_out, shift_out, buf):
    nb, tt, _ = rw.shape
    ti = pl.program_id(1)
    rows = nb * tt

    @pl.when(ti == 0)
    def _():
        buf[:, SUBLANE - 1:SUBLANE, :] = sprev_ref[...]

    @pl.when(ti > 0)
    def _():
        buf[:, SUBLANE - 1:SUBLANE, :] = buf[:, tt + SUBLANE - 1:tt + SUBLANE, :]

    buf[:, SUBLANE:, :] = rw
    shift_out[...] = rw[:, tt - 1:tt, :]
    rw_prev = buf[:, SUBLANE - 1:SUBLANE - 1 + tt, :]
    rws = (rw + mu_ref[...] * (rw_prev - rw)).reshape(rows, SHIFT_W)
    r = rws[:, 0:D_R]
    k = rws[:, D_R:2 * D_R]
    v = rws[:, 2 * D_R:3 * D_R]
    wa = rws[:, 3 * D_R:]
    w_log = -_softplus(-(w0_ref[...] + _dot(jnp.tanh(wa).astype(BF16), w2_ref[...]))) - 0.5
    decay = jnp.exp(-jnp.exp(w_log))
    a = _sigmoid(a0_ref[...] + _dot(wa.astype(BF16), a2_ref[...]))
    bd = bd_ref[...]
    kk = k * kk_ref[...]
    kk = kk / jnp.maximum(jnp.sqrt(_seg_sum(kk * kk, bd)), 1e-12)
    k = k * (1.0 + (a - 1.0) * ka_ref[...])
    bonus = _seg_sum(r * k * rk_ref[...], bd) * v
    b = kk * a
    q = decay * r - kk * _seg_sum(b * r, bd)
    vkr = v * _seg_sum(k * r, bd)
    shp = (nb, tt, D_R)
    q_out[...] = q.reshape(shp)
    w_out[...] = decay.reshape(shp)
    k_out[...] = k.reshape(shp)
    v_out[...] = v.reshape(shp)
    kk_out[...] = kk.reshape(shp)
    b_out[...] = b.reshape(shp)
    vkr_out[...] = vkr.reshape(shp)
    bonus_out[...] = bonus.reshape(shp)


def _rwkv_scan_kernel(q_ref, w_ref, k_ref, v_ref, kk_ref, b_ref, vkr_ref, s0_ref, bd_ref, eye_ref,
                      y_ref, sT_ref, s_scr):
    nb, tc, _ = q_ref.shape
    ci = pl.program_id(1)
    rows = nb * R_HEAD

    @pl.when(ci == 0)
    def _():
        s_scr[...] = s0_ref[...]

    bd = bd_ref[...]
    eye = eye_ref[...][None]
    eye_bf = eye.astype(BF16)

    gb = max(nb // SCAN_GROUPS, 1)

    def seg(x):
        return _dot(x.reshape(gb * R_HEAD, D_R), bd).reshape(gb, R_HEAD, D_R)

    def step(t, carry):
        for g0 in range(0, nb, gb):
            grp = slice(g0, g0 + gb)
            row = lambda ref: ref[grp, pl.ds(t, 1), :]
            s = s_scr[grp]
            s_bf = s.astype(BF16)
            sa = seg(s_bf * row(kk_ref).astype(BF16))
            y_col = seg(s_bf * row(q_ref).astype(BF16))
            v_col = seg(eye_bf * row(v_ref).astype(BF16))
            s_scr[grp] = s * row(w_ref) - sa * row(b_ref) + v_col * row(k_ref)
            y_ref[grp, pl.ds(t, 1), :] = jnp.sum(y_col * eye, axis=1, keepdims=True) + row(vkr_ref)
        return carry

    lax.fori_loop(0, tc, step, 0, unroll=SCAN_UNROLL)

    @pl.when(ci == pl.num_programs(1) - 1)
    def _():
        sT_ref[...] = s_scr[...]


def _rwkv_scan(q, w, k, v, kk, b, vkr, s0, bd, eye):
    bsz, seq, _ = q.shape
    nb = math.gcd(bsz, 8)
    tc = min(seq, 256)
    row = pl.BlockSpec((nb, tc, D_R), lambda bi, c: (bi, c, 0))
    st = pl.BlockSpec((nb, R_HEAD, D_R), lambda bi, c: (bi, 0, 0))
    return pl.pallas_call(
        _rwkv_scan_kernel,
        grid=(bsz // nb, seq // tc),
        in_specs=[row] * 7 + [st, pl.BlockSpec(bd.shape, lambda bi, c: (0, 0)),
                              pl.BlockSpec(eye.shape, lambda bi, c: (0, 0))],
        out_specs=[row, st],
        out_shape=[jax.ShapeDtypeStruct((bsz, seq, D_R), F32),
                   jax.ShapeDtypeStruct((bsz, R_HEAD, D_R), F32)],
        scratch_shapes=[pltpu.VMEM((nb, R_HEAD, D_R), F32)],
        compiler_params=_cparams(("parallel", "arbitrary")),
        name="rwkv_scan",
    )(q, w, k, v, kk, b, vkr, s0, bd, eye)


def _out_kernel(x_ref, mod_ref, omla_ref, szmla_ref, cbz_ref, ccx_ref, cprev_ref, szrw_ref,
                yr_ref, bonus_ref, gm_ref, wmla_ref, wconv_ref, wrw_ref, wout_ref, cw_ref,
                gng_ref, gnb_ref, bd_ref, fg_ref, xo_ref, cstate_ref, buf, *, final):
    nb, tt, d = x_ref.shape
    ti = pl.program_id(1)
    rows = nb * tt

    @pl.when(ti == 0)
    def _():
        buf[:, SUBLANE - 2:SUBLANE, :] = cprev_ref[...]

    @pl.when(ti > 0)
    def _():
        buf[:, SUBLANE - 2:SUBLANE, :] = buf[:, tt + SUBLANE - 2:tt + SUBLANE, :]

    buf[:, SUBLANE:, :] = ccx_ref[...]
    cstate_ref[...] = buf[:, tt + SUBLANE - 2:tt + SUBLANE, :]
    conv = (buf[:, SUBLANE - 2:SUBLANE - 2 + tt, :] * cw_ref[0:1, :]
            + buf[:, SUBLANE - 1:SUBLANE - 1 + tt, :] * cw_ref[1:2, :]
            + buf[:, SUBLANE:, :] * cw_ref[2:3, :])
    y_conv = _dot((cbz_ref[...] * conv).reshape(rows, D_CONV).astype(BF16), wconv_ref[...])

    y_mla = _dot((omla_ref[...] * szmla_ref[...].astype(F32)).reshape(rows, D_MLA).astype(BF16), wmla_ref[...])

    bd = bd_ref[...]
    yr = yr_ref[...].reshape(rows, D_R)
    mu = _seg_sum(yr, bd) * (1.0 / R_HEAD)
    dy = yr - mu
    var = _seg_sum(dy * dy, bd) * (1.0 / R_HEAD)
    yn = dy * lax.rsqrt(var + GN_EPS) * gng_ref[...] + gnb_ref[...]
    o_rw = yn + bonus_ref[...].reshape(rows, D_R)
    y_rw = _dot((o_rw * szrw_ref[...].astype(F32).reshape(rows, D_R)).astype(BF16), wrw_ref[...])

    g = gm_ref[...].astype(F32).reshape(rows, N_BRANCH * d)
    merged = g[:, 0:d] * y_mla + g[:, d:2 * d] * y_conv + g[:, 2 * d:] * y_rw
    delta = _dot(merged.astype(BF16), wout_ref[...]).reshape(nb, tt, d)
    xo = x_ref[...] + mod_ref[:, :, 2 * d:] * delta
    if final:
        xo = xo * lax.rsqrt(jnp.mean(xo * xo, axis=-1, keepdims=True) + RMS_EPS) * fg_ref[...]
    xo_ref[...] = xo


def _out_proj(x, mod, o_mla, sz_mla, cbz, ccx, conv_prev, sz_rw, y_r, bonus, gm,
              wmla, wconv, wrw, wout, conv_w, gn_g, gn_b, bd, final_g, final):
    bsz, seq, d = x.shape
    nb, tt = _row_tiles(bsz, seq, 256)
    grid = (bsz // nb, seq // tt)
    row = lambda w: pl.BlockSpec((nb, tt, w), lambda b, t: (b, t, 0))
    full = lambda a: pl.BlockSpec(a.shape, lambda b, t: (0,) * a.ndim)
    consts = [wmla, wconv, wrw, wout, conv_w, gn_g.reshape(1, -1), gn_b.reshape(1, -1), bd,
              final_g.reshape(1, -1)]
    return pl.pallas_call(
        functools.partial(_out_kernel, final=final),
        grid=grid,
        in_specs=[row(d), pl.BlockSpec((nb, 1, 3 * d), lambda b, t: (b, 0, 0)),
                  row(D_MLA), row(D_MLA), row(D_CONV), row(D_CONV),
                  pl.BlockSpec((nb, CONV_W - 1, D_CONV), lambda b, t: (b, 0, 0)),
                  row(D_R), row(D_R), row(D_R), row(N_BRANCH * d)] + [full(a) for a in consts],
        out_specs=[row(d), pl.BlockSpec((nb, CONV_W - 1, D_CONV), lambda b, t: (b, 0, 0))],
        out_shape=[jax.ShapeDtypeStruct((bsz, seq, d), F32),
                   jax.ShapeDtypeStruct((bsz, CONV_W - 1, D_CONV), F32)],
        scratch_shapes=[pltpu.VMEM((nb, tt + SUBLANE, D_CONV), F32)],
        compiler_params=_cparams(("parallel", "arbitrary")),
        name="out_proj",
    )(x, mod, o_mla, sz_mla, cbz, ccx, conv_prev, sz_rw, y_r, bonus, gm, *consts)


def _swap_halves(w):
    half = QK_ROPE // 2
    return jnp.concatenate([w[..., half:], w[..., :half]], axis=-1)


def _pad_w_in(w_in):
    a = Q_LORA + KV_LORA
    k_rope = w_in[:, :, a:a + QK_ROPE]
    pad = jnp.zeros(w_in.shape[:2] + (KR_W - 2 * QK_ROPE,), w_in.dtype)
    return jnp.concatenate([w_in[:, :, :a], k_rope, _swap_halves(k_rope), pad, w_in[:, :, a + QK_ROPE:]],
                           axis=2).astype(BF16)


def _layer_params(l, w_ada, w_in_p, w_q_b, w_uk, w_uv, w_mla_out, w_conv_out, rwkv_w2, rwkv_a2,
                  w_rwkv_out, w_out):
    w_in_p = w_in_p[l]
    wq = w_q_b[l].reshape(Q_LORA, N_HEADS, QK_NOPE + QK_ROPE)
    wn = jnp.pad(wq[:, :, :QK_NOPE], ((0, 0), (0, 0), (0, LANE - QK_NOPE))).reshape(Q_LORA, N_HEADS * LANE)
    wr = wq[:, :, QK_NOPE:]
    wrs = _swap_halves(wr)
    uk = jnp.pad(jnp.transpose(w_uk[l], (1, 2, 0)), ((0, 0), (0, LANE - QK_NOPE), (0, 0)))
    wuv = jnp.transpose(w_uv[l], (1, 0, 2))
    zv = jnp.zeros_like(wuv)
    even = (jnp.arange(N_HEADS) % 2 == 0)[:, None, None]
    wuv_pad = jnp.where(even, jnp.concatenate([wuv, zv], axis=2), jnp.concatenate([zv, wuv], axis=2))
    zeros = jnp.zeros((W_LORA, D_R), F32)
    return dict(
        w_ada=w_ada[l].astype(BF16), w_in_p=w_in_p,
        wn=wn.astype(BF16), wr=wr.reshape(Q_LORA, -1).astype(BF16), wrs=wrs.reshape(Q_LORA, -1).astype(BF16),
        uk=uk.astype(BF16), wuv_pad=wuv_pad.astype(BF16),
        wmla=w_mla_out[l].astype(BF16), wconv=w_conv_out[l].astype(BF16),
        wrw=w_rwkv_out[l].astype(BF16), wout=w_out[l].astype(BF16),
        w2p=jnp.concatenate([rwkv_w2[l], zeros], axis=0).astype(BF16),
        a2p=jnp.concatenate([zeros, rwkv_a2[l]], axis=0).astype(BF16),
    )


def _rope_tables(pos):
    half = QK_ROPE // 2
    inv = ROPE_THETA ** (-jnp.arange(half, dtype=F32) / half)
    ang = pos.astype(F32)[:, None] * inv[None, :]
    cos, sin = jnp.cos(ang), jnp.sin(ang)
    cos_k = jnp.concatenate([cos, cos], axis=1)
    sin_k = jnp.concatenate([-sin, sin], axis=1)
    return jnp.tile(cos_k, (1, N_HEADS)), jnp.tile(sin_k, (1, N_HEADS)), cos_k, sin_k


def _state_to_lanes(s):
    b = s.shape[0]
    return jnp.transpose(s, (0, 2, 1, 3)).reshape(b, R_HEAD, D_R)


def _state_from_lanes(s):
    b = s.shape[0]
    return jnp.transpose(s.reshape(b, R_HEAD, R_HEADS, R_HEAD), (0, 2, 1, 3))


def _layer(x, c_mod, tabs, p, vecs, conv_prev, shift_prev, s0, consts, final_g, final, attend, q_dtype):
    bd, eye = consts
    (sz_mla, cbz, ccx, sz_rw, gm, qlat, qpe, ckv, kpe, ckv_bf, kpe_bf,
     q, w, k, v, kk, b, vkr, bonus, shift_state) = _front(
        x, c_mod, vecs["norm_g"], p["w_in_p"], tabs, vecs["q_norm_g"], vecs["kv_norm_g"],
        p["wn"], p["wr"], p["wrs"], p["uk"], q_dtype,
        shift_prev, vecs["mu"], vecs["w0"], p["w2p"], vecs["a0"], p["a2p"],
        vecs["k_k"], vecs["k_a"], vecs["r_k"], bd)
    o_mla = attend(qlat, qpe, ckv_bf, kpe_bf, p["wuv_pad"])
    y_r, s_new = _rwkv_scan(q, w, k, v, kk, b, vkr, s0, bd, eye)
    x_new, conv_state = _out_proj(
        x, c_mod, o_mla, sz_mla, cbz, ccx, conv_prev, sz_rw, y_r, bonus, gm,
        p["wmla"], p["wconv"], p["wrw"], p["wout"], vecs["conv_w"], vecs["gn_g"], vecs["gn_b"], bd,
        final_g, final)
    return x_new, ckv, kpe, conv_state, shift_state[:, 0], s_new


def kernel(x_prompt, x_sample, cache_mla_ckv, cache_mla_kpe, state_conv, state_rwkv_shift, state_rwkv, page_table, c_prompt, c_sample, norm_g, w_ada, b_ada, w_in, q_norm_g, w_q_b, kv_norm_g, w_uk, w_uv, w_mla_out, conv_w, w_conv_out, rwkv_mu, rwkv_w0, rwkv_w2, rwkv_a0, rwkv_a2, rwkv_k_k, rwkv_k_a, rwkv_r_k, rwkv_gn_g, rwkv_gn_b, w_rwkv_out, w_out, final_norm_g):
    depth = norm_g.shape[0]
    bp, tp, _ = x_prompt.shape
    bs, ts, _ = x_sample.shape
    past = page_table.shape[1] * PAGE
    tabs_p = _rope_tables(jnp.arange(tp, dtype=jnp.int32))
    tabs_s = _rope_tables(past + jnp.arange(ts, dtype=jnp.int32))
    seg = jnp.arange(D_R, dtype=jnp.int32) // R_HEAD
    bd = (seg[:, None] == seg[None, :]).astype(BF16)
    eye = (jnp.arange(R_HEAD, dtype=jnp.int32)[:, None] == (jnp.arange(D_R, dtype=jnp.int32) % R_HEAD)[None, :]).astype(F32)
    consts = (bd, eye)
    w_in = _pad_w_in(w_in)
    xp, xs = x_prompt, x_sample
    st_p = ([], [], [], [], [])
    st_s = ([], [], [], [], [])
    for l in range(depth):
        final = l == depth - 1
        p = _layer_params(l, w_ada, w_in, w_q_b, w_uk, w_uv, w_mla_out, w_conv_out, rwkv_w2, rwkv_a2,
                          w_rwkv_out, w_out)
        vecs = dict(norm_g=norm_g[l], q_norm_g=q_norm_g[l], kv_norm_g=kv_norm_g[l], mu=rwkv_mu[l],
                    w0=rwkv_w0[l], a0=rwkv_a0[l], k_k=rwkv_k_k[l], k_a=rwkv_k_a[l],
                    r_k=rwkv_r_k[l].reshape(-1), conv_w=conv_w[l], gn_g=rwkv_gn_g[l], gn_b=rwkv_gn_b[l])
        mod_p = _ada_mod(c_prompt, p["w_ada"], b_ada[l])
        mod_s = _ada_mod(c_sample, p["w_ada"], b_ada[l])
        xp, ckv, kpe, cst, sst, rst = _layer(
            xp, mod_p, tabs_p, p, vecs,
            jnp.zeros((bp, CONV_W - 1, D_CONV), F32), jnp.zeros((bp, SHIFT_W), F32),
            jnp.zeros((bp, R_HEAD, D_R), F32), consts, final_norm_g, final, _attn_prompt, BF16)
        for lst, val in zip(st_p, (ckv, kpe, cst, sst, _state_from_lanes(rst))):
            lst.append(val)
        attend_s = functools.partial(_attn_sample_bound, page_table, cache_mla_ckv, cache_mla_kpe, l)
        xs, ckv, kpe, cst, sst, rst = _layer(
            xs, mod_s, tabs_s, p, vecs, state_conv[l], state_rwkv_shift[l],
            _state_to_lanes(state_rwkv[l]), consts, final_norm_g, final, attend_s, F32)
        for lst, val in zip(st_s, (ckv, kpe, cst, sst, _state_from_lanes(rst))):
            lst.append(val)
    return (xp, xs,
            jnp.stack(st_p[0]), jnp.stack(st_p[1]), jnp.stack(st_p[2]), jnp.stack(st_p[3]), jnp.stack(st_p[4]),
            jnp.stack(st_s[0]), jnp.stack(st_s[1]), jnp.stack(st_s[2]), jnp.stack(st_s[3]), jnp.stack(st_s[4]))


def _attn_sample_bound(page_table, cache_ckv, cache_kpe, layer, qlat, qpe, ckv_bf, kpe_bf, wuv_pad):
    return _attn_sample(page_table, qlat, qpe, ckv_bf, kpe_bf, wuv_pad, cache_ckv, cache_kpe, layer)
```

```python
import functools
import math

import jax
import jax.numpy as jnp
from jax import lax
from jax.experimental import pallas as pl
from jax.experimental.pallas import tpu as pltpu

F32 = jnp.float32
BF16 = jnp.bfloat16

D_MODEL = 1024
PAGE = 128
N_HEADS = 8
QK_NOPE = 64
QK_ROPE = 32
V_HEAD = 64
Q_LORA = 384
KV_LORA = 256
D_MLA = N_HEADS * V_HEAD
ROPE_THETA = 10000.0
ATTN_SCALE = (QK_NOPE + QK_ROPE) ** -0.5
Q_SCALE = ATTN_SCALE * math.log2(math.e)
D_CONV = 256
CONV_W = 3
R_HEADS = 4
R_HEAD = 64
D_R = R_HEADS * R_HEAD
W_LORA = 64
A_LORA = 64
SHIFT_W = 3 * D_R + W_LORA + A_LORA
GN_EPS = 64e-5
RMS_EPS = 1e-6
N_BRANCH = 3

LANE = 128
SUBLANE = 8
KR_W = LANE
SEG_W = (Q_LORA, KV_LORA, KR_W, D_MLA, 4 * D_CONV, SHIFT_W, D_R, N_BRANCH * D_MODEL)
SEG_OFF = tuple(sum(SEG_W[:i]) for i in range(len(SEG_W)))
PROJ_P = sum(SEG_W)
VMEM_LIMIT = 56 * 1024 * 1024
NEG_BIG = -1e30
SCAN_UNROLL = 8
SCAN_GROUPS = 2
PAGE_SLOTS = 3


def _cparams(sem):
    return pltpu.CompilerParams(dimension_semantics=sem, vmem_limit_bytes=VMEM_LIMIT)


def _dot(a, b):
    return jnp.dot(a, b, preferred_element_type=F32)


def _dot_nt(a, b):
    return lax.dot_general(a, b, (((1,), (1,)), ((), ())), preferred_element_type=F32)


def _sigmoid(x):
    return 1.0 / (1.0 + jnp.exp(-x))


def _silu(x):
    return x * _sigmoid(x)


def _softplus(x):
    return jnp.maximum(x, 0.0) + jnp.log1p(jnp.exp(-jnp.abs(x)))


def _seg_sum(x, bd):
    hi = x.astype(BF16)
    lo = (x - hi.astype(F32)).astype(BF16)
    return _dot(hi, bd) + _dot(lo, bd)


def _row_tiles(batch, seq, cap):
    if seq >= LANE:
        tt = min(seq, cap)
        assert seq % tt == 0
        return 1, tt
    assert seq % SUBLANE == 0
    return batch, seq


def _ada_kernel(c_ref, w_ref, b_ref, o_ref):
    s = _silu(c_ref[...])
    o_ref[...] = _dot(s.astype(BF16), w_ref[...]) + b_ref[...]


def _ada_mod(c, w_ada_bf, b_ada):
    bsz = c.shape[0]
    out = pl.pallas_call(
        _ada_kernel,
        out_shape=jax.ShapeDtypeStruct((bsz, 3 * D_MODEL), F32),
        compiler_params=pltpu.CompilerParams(vmem_limit_bytes=VMEM_LIMIT),
        name="ada_mod",
    )(c, w_ada_bf, b_ada.reshape(1, -1))
    return out.reshape(bsz, 1, 3 * D_MODEL)


N_MLA_IN, N_RWKV_IN = 10, 10
N_GATE_OUT = 5


def _front_kernel(x_ref, mod_ref, g_ref, w_ref, *refs):
    nb, tt, d = x_ref.shape
    mla_in = refs[:N_MLA_IN]
    rwkv_in = refs[N_MLA_IN:N_MLA_IN + N_RWKV_IN]
    outs = refs[N_MLA_IN + N_RWKV_IN:]
    zmla_out, cbz_out, ccx_out, zrw_out, gm_out = outs[:N_GATE_OUT]
    mla_out = outs[N_GATE_OUT:N_GATE_OUT + 6]
    rwkv_out = outs[N_GATE_OUT + 6:N_GATE_OUT + 15]
    buf = outs[N_GATE_OUT + 15]
    x = x_ref[...]
    ms = jnp.mean(x * x, axis=-1, keepdims=True)
    xn = x * lax.rsqrt(ms + RMS_EPS) * g_ref[...]
    shift = mod_ref[:, :, 0:D_MODEL]
    scale = mod_ref[:, :, D_MODEL:2 * D_MODEL]
    u = (xn * (1.0 + scale) + shift).reshape(nb * tt, d).astype(BF16)

    def seg(i):
        return _dot(u, w_ref[:, SEG_OFF[i]:SEG_OFF[i] + SEG_W[i]])

    zmla_out[...] = _silu(seg(3)).reshape(nb, tt, SEG_W[3]).astype(zmla_out.dtype)
    c4 = seg(4)
    cbz_out[...] = (c4[:, 0:D_CONV] * _silu(c4[:, 3 * D_CONV:])).reshape(nb, tt, D_CONV)
    ccx_out[...] = (c4[:, D_CONV:2 * D_CONV] * c4[:, 2 * D_CONV:3 * D_CONV]).reshape(nb, tt, D_CONV)
    zrw_out[...] = _silu(seg(6)).reshape(nb, tt, SEG_W[6]).astype(zrw_out.dtype)
    gm_out[...] = _sigmoid(seg(7)).reshape(nb, tt, SEG_W[7]).astype(gm_out.dtype)
    _mla_prep_body(seg(0), seg(1).reshape(nb, tt, KV_LORA), seg(2).reshape(nb, tt, KR_W), nb, tt,
                   *mla_in, *mla_out)
    _rwkv_prep_body(seg(5).reshape(nb, tt, SHIFT_W), *rwkv_in, *rwkv_out, buf)


def _front(x, mod, norm_g, w_in_p, tabs, q_norm_g, kv_norm_g, wn, wr, wrs, uk, q_dtype,
           shift_prev, mu, w0, w2p, a0, a2p, k_k, k_a, r_k, bd):
    bsz, seq, d = x.shape
    nb, tt = _row_tiles(bsz, seq, 512)
    grid = (bsz // nb, seq // tt)
    cos_q, sin_q, cos_k, sin_k = tabs
    row = lambda w: pl.BlockSpec((nb, tt, w), lambda b, t: (b, t, 0))
    tab = lambda w: pl.BlockSpec((tt, w), lambda b, t: (t, 0))
    full = lambda a: pl.BlockSpec(a.shape, lambda b, t: (0,) * a.ndim)
    hq = lambda w: pl.BlockSpec((nb, N_HEADS, tt, w), lambda b, t: (b, 0, t, 0))
    state = pl.BlockSpec((nb, 1, SHIFT_W), lambda b, t: (b, 0, 0))
    mla_consts = [q_norm_g.reshape(1, -1), kv_norm_g.reshape(1, -1), wn, wr, wrs, uk]
    rwkv_consts = [mu.reshape(1, -1), w0.reshape(1, -1), w2p, a0.reshape(1, -1), a2p,
                   k_k.reshape(1, -1), k_a.reshape(1, -1), r_k.reshape(1, -1), bd]
    assert 4 + len(mla_consts) == N_MLA_IN and len(rwkv_consts) + 1 == N_RWKV_IN
    g = norm_g.reshape(1, d)
    outs = pl.pallas_call(
        _front_kernel,
        grid=grid,
        in_specs=[row(d), pl.BlockSpec((nb, 1, 3 * d), lambda b, t: (b, 0, 0)), full(g),
                  pl.BlockSpec(w_in_p.shape, lambda b, t: (0, 0), pipeline_mode=pl.Buffered(1)),
                  tab(N_HEADS * QK_ROPE), tab(N_HEADS * QK_ROPE), tab(QK_ROPE), tab(QK_ROPE)]
                 + [full(a) for a in mla_consts] + [full(a) for a in rwkv_consts[:-1]] + [full(bd), state],
        out_specs=[row(SEG_W[3]), row(D_CONV), row(D_CONV), row(SEG_W[6]), row(SEG_W[7]),
                   hq(KV_LORA), hq(QK_ROPE), row(KV_LORA), row(QK_ROPE), row(KV_LORA), row(QK_ROPE)]
                  + [row(D_R)] * 8 + [state],
        out_shape=[jax.ShapeDtypeStruct((bsz, seq, SEG_W[3]), BF16),
                   jax.ShapeDtypeStruct((bsz, seq, D_CONV), F32),
                   jax.ShapeDtypeStruct((bsz, seq, D_CONV), F32),
                   jax.ShapeDtypeStruct((bsz, seq, SEG_W[6]), BF16),
                   jax.ShapeDtypeStruct((bsz, seq, SEG_W[7]), BF16)]
                  + [jax.ShapeDtypeStruct((bsz, N_HEADS, seq, KV_LORA), q_dtype),
                     jax.ShapeDtypeStruct((bsz, N_HEADS, seq, QK_ROPE), q_dtype),
                     jax.ShapeDtypeStruct((bsz, seq, KV_LORA), F32),
                     jax.ShapeDtypeStruct((bsz, seq, QK_ROPE), F32),
                     jax.ShapeDtypeStruct((bsz, seq, KV_LORA), BF16),
                     jax.ShapeDtypeStruct((bsz, seq, QK_ROPE), BF16)]
                  + [jax.ShapeDtypeStruct((bsz, seq, D_R), F32)] * 8
                  + [jax.ShapeDtypeStruct((bsz, 1, SHIFT_W), F32)],
        scratch_shapes=[pltpu.VMEM((nb, tt + SUBLANE, SHIFT_W), F32)],
        compiler_params=_cparams(("parallel", "arbitrary")),
        name="front",
    )(x, mod, g, w_in_p, cos_q, sin_q, cos_k, sin_k, *mla_consts, *rwkv_consts,
      shift_prev.reshape(bsz, 1, SHIFT_W))
    return outs


def _mla_prep_body(qa, kva, kr, nb, tt, cq_ref, sq_ref, ck_ref, sk_ref,
                   gq_ref, gkv_ref, wn_ref, wr_ref, wrs_ref, uk_ref,
                   qlat_ref, qpe_ref, ckv_ref, kpe_ref, ckvb_ref, kpeb_ref):
    rows = nb * tt
    cq = qa * lax.rsqrt(jnp.mean(qa * qa, axis=-1, keepdims=True) + RMS_EPS) * gq_ref[...]
    cqb = cq.astype(BF16)
    qn = _dot(cqb, wn_ref[...])
    cos_q = jnp.broadcast_to(cq_ref[...][None], (nb, tt, N_HEADS * QK_ROPE)).reshape(rows, -1)
    sin_q = jnp.broadcast_to(sq_ref[...][None], (nb, tt, N_HEADS * QK_ROPE)).reshape(rows, -1)
    qp = (_dot(cqb, wr_ref[...]) * cos_q + _dot(cqb, wrs_ref[...]) * sin_q) * Q_SCALE
    for h in range(N_HEADS):
        qn_h = qn[:, h * LANE:(h + 1) * LANE].astype(BF16)
        ql = _dot(qn_h, uk_ref[h]) * Q_SCALE
        qlat_ref[:, h] = ql.reshape(nb, tt, KV_LORA).astype(qlat_ref.dtype)
        qpe_ref[:, h] = qp[:, h * QK_ROPE:(h + 1) * QK_ROPE].reshape(nb, tt, QK_ROPE).astype(qpe_ref.dtype)
    ckv = kva * lax.rsqrt(jnp.mean(kva * kva, axis=-1, keepdims=True) + RMS_EPS) * gkv_ref[...]
    ckv_ref[...] = ckv
    ckvb_ref[...] = ckv.astype(BF16)
    kpe = kr[:, :, 0:QK_ROPE] * ck_ref[...][None] + kr[:, :, QK_ROPE:2 * QK_ROPE] * sk_ref[...][None]
    kpe_ref[...] = kpe
    kpeb_ref[...] = kpe.astype(BF16)


def _value_up_proj(o_lat, wuv_ref, t):
    tiles = []
    for j in range(N_HEADS // 2):
        h0, h1 = 2 * j, 2 * j + 1
        tiles.append(_dot(o_lat[h0 * t:(h0 + 1) * t].astype(BF16), wuv_ref[h0])
                     + _dot(o_lat[h1 * t:(h1 + 1) * t].astype(BF16), wuv_ref[h1]))
    return jnp.concatenate(tiles, axis=1)


def _softmax_update(s, m_old, l_old):
    m_new = jnp.maximum(m_old, jnp.max(s, axis=-1, keepdims=True))
    alpha = jnp.exp2(m_old - m_new)
    p = jnp.exp2(s - jnp.tile(m_new, (1, s.shape[1] // LANE)))
    l_new = alpha * l_old + jnp.sum(p, axis=-1, keepdims=True)
    return p, m_new, l_new, alpha


def _attn_prompt_kernel(ql_ref, qp_ref, ckv_ref, kpe_ref, wuv_ref, o_ref,
                        m_scr, l_scr, a_scr, acc_scr, s_scr, p_scr, *, tq, tk, rc):
    qi = pl.program_id(1)
    rows = N_HEADS * tq
    ql = ql_ref[0].reshape(rows, KV_LORA)
    qp = qp_ref[0].reshape(rows, QK_ROPE)
    n_kt = (qi * tq + tq + tk - 1) // tk

    def scores(kt, slot):
        start = pl.multiple_of(kt * tk, tk)
        s_scr[slot] = (_dot_nt(ql, ckv_ref[0, pl.ds(start, tk), :])
                       + _dot_nt(qp, kpe_ref[0, pl.ds(start, tk), :]))

    def softmax(kt, slot, first):
        for c in range(rows // rc):
            rs = slice(c * rc, (c + 1) * rc)
            s = s_scr[slot, rs, :]
            if first:
                q_pos = qi * tq + (c * rc) % tq + lax.broadcasted_iota(jnp.int32, (rc, tk), 0)
                k_pos = kt * tk + lax.broadcasted_iota(jnp.int32, (rc, tk), 1)
                s = jnp.where(k_pos <= q_pos, s, NEG_BIG)
                m_new = jnp.broadcast_to(jnp.max(s, axis=-1, keepdims=True), (rc, LANE))
                p = jnp.exp2(s - jnp.tile(m_new, (1, tk // LANE)))
                l_new = jnp.broadcast_to(jnp.sum(p, axis=-1, keepdims=True), (rc, LANE))
            else:
                p, m_new, l_new, alpha = _softmax_update(s, m_scr[rs, :], l_scr[rs, :])
                a_scr[rs, :] = alpha
            m_scr[rs, :] = m_new
            l_scr[rs, :] = l_new
            p_scr[rs, :] = p.astype(BF16)

    def weighted_values(kt, first):
        start = pl.multiple_of(kt * tk, tk)
        pv = _dot(p_scr[...], ckv_ref[0, pl.ds(start, tk), :])
        if first:
            acc_scr[...] = pv
        else:
            acc_scr[...] = jnp.tile(a_scr[...], (1, KV_LORA // LANE)) * acc_scr[...] + pv

    last = n_kt - 1
    scores(last, 0)
    softmax(last, 0, True)
    scores(0, 1)
    weighted_values(last, True)

    def body(kt, carry):
        slot = (kt + 1) % 2
        softmax(kt, slot, False)
        scores(kt + 1, 1 - slot)
        weighted_values(kt, False)
        return carry

    lax.fori_loop(0, last - 1, body, 0)

    @pl.when(last >= 1)
    def _():
        softmax(last - 1, last % 2, False)
        weighted_values(last - 1, False)
    o_lat = acc_scr[...] / jnp.tile(l_scr[...], (1, KV_LORA // LANE))
    o_ref[0] = _value_up_proj(o_lat, wuv_ref, tq)


def _attn_prompt(qlat, qpe, ckv_bf, kpe_bf, wuv_pad):
    bsz, _, seq, _ = qlat.shape
    tq = 128
    tk = min(256, seq)
    assert tk % tq == 0 and seq % tk == 0
    rc = 64
    kern = functools.partial(_attn_prompt_kernel, tq=tq, tk=tk, rc=rc)
    rows = N_HEADS * tq
    return pl.pallas_call(
        kern,
        grid=(bsz, seq // tq),
        in_specs=[pl.BlockSpec((1, N_HEADS, tq, KV_LORA), lambda b, q: (b, 0, q, 0)),
                  pl.BlockSpec((1, N_HEADS, tq, QK_ROPE), lambda b, q: (b, 0, q, 0)),
                  pl.BlockSpec((1, seq, KV_LORA), lambda b, q: (b, 0, 0)),
                  pl.BlockSpec((1, seq, QK_ROPE), lambda b, q: (b, 0, 0)),
                  pl.BlockSpec(wuv_pad.shape, lambda b, q: (0, 0, 0))],
        out_specs=pl.BlockSpec((1, tq, D_MLA), lambda b, q: (b, q, 0)),
        out_shape=jax.ShapeDtypeStruct((bsz, seq, D_MLA), F32),
        scratch_shapes=[pltpu.VMEM((rows, LANE), F32), pltpu.VMEM((rows, LANE), F32),
                        pltpu.VMEM((rows, LANE), F32),
                        pltpu.VMEM((rows, KV_LORA), F32),
                        pltpu.VMEM((2, rows, tk), F32), pltpu.VMEM((rows, tk), BF16)],
        compiler_params=_cparams(("parallel", "arbitrary")),
        name="attn_prompt",
    )(qlat, qpe, ckv_bf, kpe_bf, wuv_pad)


def _attn_sample_kernel(pt_ref, ql_ref, qp_ref, ckvn_ref, kpen_ref, wuv_ref, cache_ckv, cache_kpe_t,
                        o_ref, ckv_buf, kpe_buf, sem, *, layer, pp, cp, n_steps, ts):
    ahead = PAGE_SLOTS - 1
    b = pl.program_id(0)
    nb = pl.num_programs(0)
    rows = N_HEADS * ts
    total = nb * n_steps
    ql = ql_ref[0].reshape(rows, KV_LORA).astype(BF16)
    qp = qp_ref[0].reshape(rows, QK_ROPE).astype(BF16)

    def page_copies(seq, grp, slot):
        out = []
        for i in range(pp):
            page = pt_ref[seq, grp * pp + i]
            out.append(pltpu.make_async_copy(cache_ckv.at[layer, page], ckv_buf.at[slot, i], sem.at[slot]))
            out.append(pltpu.make_async_copy(cache_kpe_t.at[layer, page], kpe_buf.at[slot, i], sem.at[slot]))
        return out

    def start_group(g):
        slot = lax.rem(g, PAGE_SLOTS)
        g = jnp.minimum(g, total - 1)
        seq = lax.shift_right_logical(g, n_steps.bit_length() - 1)
        for c in page_copies(seq, jnp.bitwise_and(g, n_steps - 1), slot):
            c.start()

    def wait_group(g):
        for c in page_copies(0, 0, lax.rem(g, PAGE_SLOTS)):
            c.wait()

    @pl.when(b == 0)
    def _():
        for g in range(ahead):
            start_group(jnp.int32(g))

    def partial_softmax(s, vals):
        m = jnp.max(s, axis=-1, keepdims=True)
        p = jnp.exp2(s - m)
        return m, jnp.sum(p, axis=-1, keepdims=True), _dot(p.astype(BF16), vals)

    def merge(parts):
        m = functools.reduce(jnp.maximum, [pm for pm, _, _ in parts])
        l = sum(pl_ * jnp.exp2(pm - m) for pm, pl_, _ in parts)
        acc = sum(pa * jnp.exp2(pm - m) for pm, _, pa in parts)
        return m, l, acc

    def group_parts(slot):
        n_c = pp // cp
        cks = [ckv_buf[slot, c * cp:(c + 1) * cp].reshape(cp * PAGE, KV_LORA).astype(BF16) for c in range(n_c)]
        kps = [jnp.concatenate([kpe_buf[slot, c * cp + i].astype(BF16) for i in range(cp)], axis=1)
               for c in range(n_c)]
        ss = [_dot_nt(ql, ck) + _dot(qp, kp_t) for ck, kp_t in zip(cks, kps)]
        ms = [jnp.max(s, axis=-1, keepdims=True) for s in ss]
        ps = [jnp.exp2(s - m) for s, m in zip(ss, ms)]
        ls = [jnp.sum(p, axis=-1, keepdims=True) for p in ps]
        accs = [_dot(p.astype(BF16), ck) for p, ck in zip(ps, cks)]
        return list(zip(ms, ls, accs))

    ckn = ckvn_ref[0]
    kpn = kpen_ref[0]
    s_new = _dot_nt(ql, ckn) + _dot_nt(qp, kpn)
    t_q = lax.broadcasted_iota(jnp.int32, (rows, ts), 0) % ts
    t_k = lax.broadcasted_iota(jnp.int32, (rows, ts), 1)
    state = partial_softmax(jnp.where(t_k <= t_q, s_new, NEG_BIG), ckn)

    for st in range(n_steps):
        g = b * n_steps + st
        start_group(g + ahead)
        wait_group(g)
        state = merge([state] + group_parts(lax.rem(g, PAGE_SLOTS)))

    @pl.when(b == nb - 1)
    def _():
        for g in range(ahead):
            wait_group(total + g)

    _, l, acc = state
    o_ref[0] = _value_up_proj(acc / l, wuv_ref, ts)


def _attn_sample(page_table, qlat, qpe, ckv_new_bf, kpe_new_bf, wuv_pad, cache_ckv, cache_kpe, layer):
    bsz, _, ts, _ = qlat.shape
    n_pages = page_table.shape[1]
    pp = min(32, n_pages // 2)
    cp = min(8, pp)
    n_steps = n_pages // pp
    assert n_pages % pp == 0 and pp % cp == 0
    assert n_steps & (n_steps - 1) == 0
    kern = functools.partial(_attn_sample_kernel, layer=layer, pp=pp, cp=cp, n_steps=n_steps, ts=ts)

    cache_kpe_t = jnp.swapaxes(cache_kpe, 2, 3)

    grid_spec = pltpu.PrefetchScalarGridSpec(
        num_scalar_prefetch=1,
        grid=(bsz,),
        in_specs=[pl.BlockSpec((1, N_HEADS, ts, KV_LORA), lambda b, pt: (b, 0, 0, 0)),
                  pl.BlockSpec((1, N_HEADS, ts, QK_ROPE), lambda b, pt: (b, 0, 0, 0)),
                  pl.BlockSpec((1, ts, KV_LORA), lambda b, pt: (b, 0, 0)),
                  pl.BlockSpec((1, ts, QK_ROPE), lambda b, pt: (b, 0, 0)),
                  pl.BlockSpec(wuv_pad.shape, lambda b, pt: (0, 0, 0)),
                  pl.BlockSpec(memory_space=pl.ANY),
                  pl.BlockSpec(memory_space=pl.ANY)],
        out_specs=pl.BlockSpec((1, ts, D_MLA), lambda b, pt: (b, 0, 0)),
        scratch_shapes=[pltpu.VMEM((PAGE_SLOTS, pp, PAGE, KV_LORA), F32),
                        pltpu.VMEM((PAGE_SLOTS, pp, QK_ROPE, PAGE), F32),
                        pltpu.SemaphoreType.DMA((PAGE_SLOTS,))],
    )
    return pl.pallas_call(
        kern,
        grid_spec=grid_spec,
        out_shape=jax.ShapeDtypeStruct((bsz, ts, D_MLA), F32),
        compiler_params=_cparams(("arbitrary",)),
        name="attn_sample",
    )(page_table, qlat, qpe, ckv_new_bf, kpe_new_bf, wuv_pad, cache_ckv, cache_kpe_t)


def _rwkv_prep_body(rw, mu_ref, w0_ref, w2_ref, a0_ref, a2_ref, kk_ref, ka_ref, rk_ref, bd_ref, sprev_ref,
                    q_out, w_out, k_out, v_out, kk_out, b_out, vkr_out, bonus_out, shift_out, buf):
    nb, tt, _ = rw.shape
    ti = pl.program_id(1)
    rows = nb * tt

    @pl.when(ti == 0)
    def _():
        buf[:, SUBLANE - 1:SUBLANE, :] = sprev_ref[...]

    @pl.when(ti > 0)
    def _():
        buf[:, SUBLANE - 1:SUBLANE, :] = buf[:, tt + SUBLANE - 1:tt + SUBLANE, :]

    buf[:, SUBLANE:, :] = rw
    shift_out[...] = rw[:, tt - 1:tt, :]
    rw_prev = buf[:, SUBLANE - 1:SUBLANE - 1 + tt, :]
    rws = (rw + mu_ref[...] * (rw_prev - rw)).reshape(rows, SHIFT_W)
    r = rws[:, 0:D_R]
    k = rws[:, D_R:2 * D_R]
    v = rws[:, 2 * D_R:3 * D_R]
    wa = rws[:, 3 * D_R:]
    w_log = -_softplus(-(w0_ref[...] + _dot(jnp.tanh(wa).astype(BF16), w2_ref[...]))) - 0.5
    decay = jnp.exp(-jnp.exp(w_log))
    a = _sigmoid(a0_ref[...] + _dot(wa.astype(BF16), a2_ref[...]))
    bd = bd_ref[...]
    kk = k * kk_ref[...]
    kk = kk / jnp.maximum(jnp.sqrt(_seg_sum(kk * kk, bd)), 1e-12)
    k = k * (1.0 + (a - 1.0) * ka_ref[...])
    bonus = _seg_sum(r * k * rk_ref[...], bd) * v
    b = kk * a
    q = decay * r - kk * _seg_sum(b * r, bd)
    vkr = v * _seg_sum(k * r, bd)
    shp = (nb, tt, D_R)
    q_out[...] = q.reshape(shp)
    w_out[...] = decay.reshape(shp)
    k_out[...] = k.reshape(shp)
    v_out[...] = v.reshape(shp)
    kk_out[...] = kk.reshape(shp)
    b_out[...] = b.reshape(shp)
    vkr_out[...] = vkr.reshape(shp)
    bonus_out[...] = bonus.reshape(shp)


def _rwkv_scan_kernel(q_ref, w_ref, k_ref, v_ref, kk_ref, b_ref, vkr_ref, s0_ref, bd_ref, eye_ref,
                      y_ref, sT_ref, s_scr):
    nb, tc, _ = q_ref.shape
    ci = pl.program_id(1)
    rows = nb * R_HEAD

    @pl.when(ci == 0)
    def _():
        s_scr[...] = s0_ref[...]

    bd = bd_ref[...]
    eye = eye_ref[...][None]
    eye_bf = eye.astype(BF16)

    gb = max(nb // SCAN_GROUPS, 1)

    def seg(x):
        return _dot(x.reshape(gb * R_HEAD, D_R), bd).reshape(gb, R_HEAD, D_R)

    def step(t, carry):
        for g0 in range(0, nb, gb):
            grp = slice(g0, g0 + gb)
            row = lambda ref: ref[grp, pl.ds(t, 1), :]
            s = s_scr[grp]
            s_bf = s.astype(BF16)
            sa = seg(s_bf * row(kk_ref).astype(BF16))
            y_col = seg(s_bf * row(q_ref).astype(BF16))
            v_col = seg(eye_bf * row(v_ref).astype(BF16))
            s_scr[grp] = s * row(w_ref) - sa * row(b_ref) + v_col * row(k_ref)
            y_ref[grp, pl.ds(t, 1), :] = jnp.sum(y_col * eye, axis=1, keepdims=True) + row(vkr_ref)
        return carry

    lax.fori_loop(0, tc, step, 0, unroll=SCAN_UNROLL)

    @pl.when(ci == pl.num_programs(1) - 1)
    def _():
        sT_ref[...] = s_scr[...]


def _rwkv_scan(q, w, k, v, kk, b, vkr, s0, bd, eye):
    bsz, seq, _ = q.shape
    nb = math.gcd(bsz, 8)
    tc = min(seq, 256)
    row = pl.BlockSpec((nb, tc, D_R), lambda bi, c: (bi, c, 0))
    st = pl.BlockSpec((nb, R_HEAD, D_R), lambda bi, c: (bi, 0, 0))
    return pl.pallas_call(
        _rwkv_scan_kernel,
        grid=(bsz // nb, seq // tc),
        in_specs=[row] * 7 + [st, pl.BlockSpec(bd.shape, lambda bi, c: (0, 0)),
                              pl.BlockSpec(eye.shape, lambda bi, c: (0, 0))],
        out_specs=[row, st],
        out_shape=[jax.ShapeDtypeStruct((bsz, seq, D_R), F32),
                   jax.ShapeDtypeStruct((bsz, R_HEAD, D_R), F32)],
        scratch_shapes=[pltpu.VMEM((nb, R_HEAD, D_R), F32)],
        compiler_params=_cparams(("parallel", "arbitrary")),
        name="rwkv_scan",
    )(q, w, k, v, kk, b, vkr, s0, bd, eye)


def _out_kernel(x_ref, mod_ref, omla_ref, szmla_ref, cbz_ref, ccx_ref, cprev_ref, szrw_ref,
                yr_ref, bonus_ref, gm_ref, wmla_ref, wconv_ref, wrw_ref, wout_ref, cw_ref,
                gng_ref, gnb_ref, bd_ref, fg_ref, xo_ref, cstate_ref, buf, *, final):
    nb, tt, d = x_ref.shape
    ti = pl.program_id(1)
    rows = nb * tt

    @pl.when(ti == 0)
    def _():
        buf[:, SUBLANE - 2:SUBLANE, :] = cprev_ref[...]

    @pl.when(ti > 0)
    def _():
        buf[:, SUBLANE - 2:SUBLANE, :] = buf[:, tt + SUBLANE - 2:tt + SUBLANE, :]

    buf[:, SUBLANE:, :] = ccx_ref[...]
    cstate_ref[...] = buf[:, tt + SUBLANE - 2:tt + SUBLANE, :]
    conv = (buf[:, SUBLANE - 2:SUBLANE - 2 + tt, :] * cw_ref[0:1, :]
            + buf[:, SUBLANE - 1:SUBLANE - 1 + tt, :] * cw_ref[1:2, :]
            + buf[:, SUBLANE:, :] * cw_ref[2:3, :])
    y_conv = _dot((cbz_ref[...] * conv).reshape(rows, D_CONV).astype(BF16), wconv_ref[...])

    y_mla = _dot((omla_ref[...] * szmla_ref[...].astype(F32)).reshape(rows, D_MLA).astype(BF16), wmla_ref[...])

    bd = bd_ref[...]
    yr = yr_ref[...].reshape(rows, D_R)
    mu = _seg_sum(yr, bd) * (1.0 / R_HEAD)
    dy = yr - mu
    var = _seg_sum(dy * dy, bd) * (1.0 / R_HEAD)
    yn = dy * lax.rsqrt(var + GN_EPS) * gng_ref[...] + gnb_ref[...]
    o_rw = yn + bonus_ref[...].reshape(rows, D_R)
    y_rw = _dot((o_rw * szrw_ref[...].astype(F32).reshape(rows, D_R)).astype(BF16), wrw_ref[...])

    g = gm_ref[...].astype(F32).reshape(rows, N_BRANCH * d)
    merged = g[:, 0:d] * y_mla + g[:, d:2 * d] * y_conv + g[:, 2 * d:] * y_rw
    delta = _dot(merged.astype(BF16), wout_ref[...]).reshape(nb, tt, d)
    xo = x_ref[...] + mod_ref[:, :, 2 * d:] * delta
    if final:
        xo = xo * lax.rsqrt(jnp.mean(xo * xo, axis=-1, keepdims=True) + RMS_EPS) * fg_ref[...]
    xo_ref[...] = xo


def _out_proj(x, mod, o_mla, sz_mla, cbz, ccx, conv_prev, sz_rw, y_r, bonus, gm,
              wmla, wconv, wrw, wout, conv_w, gn_g, gn_b, bd, final_g, final):
    bsz, seq, d = x.shape
    nb, tt = _row_tiles(bsz, seq, 256)
    grid = (bsz // nb, seq // tt)
    row = lambda w: pl.BlockSpec((nb, tt, w), lambda b, t: (b, t, 0))
    full = lambda a: pl.BlockSpec(a.shape, lambda b, t: (0,) * a.ndim)
    consts = [wmla, wconv, wrw, wout, conv_w, gn_g.reshape(1, -1), gn_b.reshape(1, -1), bd,
              final_g.reshape(1, -1)]
    return pl.pallas_call(
        functools.partial(_out_kernel, final=final),
        grid=grid,
        in_specs=[row(d), pl.BlockSpec((nb, 1, 3 * d), lambda b, t: (b, 0, 0)),
                  row(D_MLA), row(D_MLA), row(D_CONV), row(D_CONV),
                  pl.BlockSpec((nb, CONV_W - 1, D_CONV), lambda b, t: (b, 0, 0)),
                  row(D_R), row(D_R), row(D_R), row(N_BRANCH * d)] + [full(a) for a in consts],
        out_specs=[row(d), pl.BlockSpec((nb, CONV_W - 1, D_CONV), lambda b, t: (b, 0, 0))],
        out_shape=[jax.ShapeDtypeStruct((bsz, seq, d), F32),
                   jax.ShapeDtypeStruct((bsz, CONV_W - 1, D_CONV), F32)],
        scratch_shapes=[pltpu.VMEM((nb, tt + SUBLANE, D_CONV), F32)],
        compiler_params=_cparams(("parallel", "arbitrary")),
        name="out_proj",
    )(x, mod, o_mla, sz_mla, cbz, ccx, conv_prev, sz_rw, y_r, bonus, gm, *consts)


def _swap_halves(w):
    half = QK_ROPE // 2
    return jnp.concatenate([w[..., half:], w[..., :half]], axis=-1)


def _pad_w_in(w_in):
    a = Q_LORA + KV_LORA
    k_rope = w_in[:, :, a:a + QK_ROPE]
    tail = w_in.shape[2] - a - QK_ROPE

    def place(piece, at):
        return jnp.pad(piece, ((0, 0), (0, 0), (at, PROJ_P - at - piece.shape[2])))

    return (place(w_in[:, :, :a], 0) + place(k_rope, a) + place(_swap_halves(k_rope), a + QK_ROPE)
            + place(w_in[:, :, a + QK_ROPE:], PROJ_P - tail)).astype(BF16)


def _layer_params(l, w_ada, w_in_p, w_q_b, w_uk, w_uv, w_mla_out, w_conv_out, rwkv_w2, rwkv_a2,
                  w_rwkv_out, w_out):
    w_in_p = w_in_p[l]
    wq = w_q_b[l].reshape(Q_LORA, N_HEADS, QK_NOPE + QK_ROPE)
    wn = jnp.pad(wq[:, :, :QK_NOPE], ((0, 0), (0, 0), (0, LANE - QK_NOPE))).reshape(Q_LORA, N_HEADS * LANE)
    wr = wq[:, :, QK_NOPE:]
    wrs = _swap_halves(wr)
    uk = jnp.pad(jnp.transpose(w_uk[l], (1, 2, 0)), ((0, 0), (0, LANE - QK_NOPE), (0, 0)))
    wuv = jnp.transpose(w_uv[l], (1, 0, 2))
    zv = jnp.zeros_like(wuv)
    even = (jnp.arange(N_HEADS) % 2 == 0)[:, None, None]
    wuv_pad = jnp.where(even, jnp.concatenate([wuv, zv], axis=2), jnp.concatenate([zv, wuv], axis=2))
    zeros = jnp.zeros((W_LORA, D_R), F32)
    return dict(
        w_ada=w_ada[l].astype(BF16), w_in_p=w_in_p,
        wn=wn.astype(BF16), wr=wr.reshape(Q_LORA, -1).astype(BF16), wrs=wrs.reshape(Q_LORA, -1).astype(BF16),
        uk=uk.astype(BF16), wuv_pad=wuv_pad.astype(BF16),
        wmla=w_mla_out[l].astype(BF16), wconv=w_conv_out[l].astype(BF16),
        wrw=w_rwkv_out[l].astype(BF16), wout=w_out[l].astype(BF16),
        w2p=jnp.concatenate([rwkv_w2[l], zeros], axis=0).astype(BF16),
        a2p=jnp.concatenate([zeros, rwkv_a2[l]], axis=0).astype(BF16),
    )


def _rope_tables(pos):
    half = QK_ROPE // 2
    inv = ROPE_THETA ** (-jnp.arange(half, dtype=F32) / half)
    ang = pos.astype(F32)[:, None] * inv[None, :]
    cos, sin = jnp.cos(ang), jnp.sin(ang)
    cos_k = jnp.concatenate([cos, cos], axis=1)
    sin_k = jnp.concatenate([-sin, sin], axis=1)
    return jnp.tile(cos_k, (1, N_HEADS)), jnp.tile(sin_k, (1, N_HEADS)), cos_k, sin_k


def _state_to_lanes(s):
    b = s.shape[0]
    return jnp.transpose(s, (0, 2, 1, 3)).reshape(b, R_HEAD, D_R)


def _state_from_lanes(s):
    b = s.shape[0]
    return jnp.transpose(s.reshape(b, R_HEAD, R_HEADS, R_HEAD), (0, 2, 1, 3))


def _layer(x, c_mod, tabs, p, vecs, conv_prev, shift_prev, s0, consts, final_g, final, attend, q_dtype):
    bd, eye = consts
    (sz_mla, cbz, ccx, sz_rw, gm, qlat, qpe, ckv, kpe, ckv_bf, kpe_bf,
     q, w, k, v, kk, b, vkr, bonus, shift_state) = _front(
        x, c_mod, vecs["norm_g"], p["w_in_p"], tabs, vecs["q_norm_g"], vecs["kv_norm_g"],
        p["wn"], p["wr"], p["wrs"], p["uk"], q_dtype,
        shift_prev, vecs["mu"], vecs["w0"], p["w2p"], vecs["a0"], p["a2p"],
        vecs["k_k"], vecs["k_a"], vecs["r_k"], bd)
    o_mla = attend(qlat, qpe, ckv_bf, kpe_bf, p["wuv_pad"])
    y_r, s_new = _rwkv_scan(q, w, k, v, kk, b, vkr, s0, bd, eye)
    x_new, conv_state = _out_proj(
        x, c_mod, o_mla, sz_mla, cbz, ccx, conv_prev, sz_rw, y_r, bonus, gm,
        p["wmla"], p["wconv"], p["wrw"], p["wout"], vecs["conv_w"], vecs["gn_g"], vecs["gn_b"], bd,
        final_g, final)
    return x_new, ckv, kpe, conv_state, shift_state[:, 0], s_new


def kernel(x_prompt, x_sample, cache_mla_ckv, cache_mla_kpe, state_conv, state_rwkv_shift, state_rwkv, page_table, c_prompt, c_sample, norm_g, w_ada, b_ada, w_in, q_norm_g, w_q_b, kv_norm_g, w_uk, w_uv, w_mla_out, conv_w, w_conv_out, rwkv_mu, rwkv_w0, rwkv_w2, rwkv_a0, rwkv_a2, rwkv_k_k, rwkv_k_a, rwkv_r_k, rwkv_gn_g, rwkv_gn_b, w_rwkv_out, w_out, final_norm_g):
    depth = norm_g.shape[0]
    bp, tp, _ = x_prompt.shape
    bs, ts, _ = x_sample.shape
    past = page_table.shape[1] * PAGE
    tabs_p = _rope_tables(jnp.arange(tp, dtype=jnp.int32))
    tabs_s = _rope_tables(past + jnp.arange(ts, dtype=jnp.int32))
    seg = jnp.arange(D_R, dtype=jnp.int32) // R_HEAD
    bd = (seg[:, None] == seg[None, :]).astype(BF16)
    eye = (jnp.arange(R_HEAD, dtype=jnp.int32)[:, None] == (jnp.arange(D_R, dtype=jnp.int32) % R_HEAD)[None, :]).astype(F32)
    consts = (bd, eye)
    w_in = _pad_w_in(w_in)
    xp, xs = x_prompt, x_sample
    st_p = ([], [], [], [], [])
    st_s = ([], [], [], [], [])
    for l in range(depth):
        final = l == depth - 1
        p = _layer_params(l, w_ada, w_in, w_q_b, w_uk, w_uv, w_mla_out, w_conv_out, rwkv_w2, rwkv_a2,
                          w_rwkv_out, w_out)
        vecs = dict(norm_g=norm_g[l], q_norm_g=q_norm_g[l], kv_norm_g=kv_norm_g[l], mu=rwkv_mu[l],
                    w0=rwkv_w0[l], a0=rwkv_a0[l], k_k=rwkv_k_k[l], k_a=rwkv_k_a[l],
                    r_k=rwkv_r_k[l].reshape(-1), conv_w=conv_w[l], gn_g=rwkv_gn_g[l], gn_b=rwkv_gn_b[l])
        mod_p = _ada_mod(c_prompt, p["w_ada"], b_ada[l])
        mod_s = _ada_mod(c_sample, p["w_ada"], b_ada[l])
        xp, ckv, kpe, cst, sst, rst = _layer(
            xp, mod_p, tabs_p, p, vecs,
            jnp.zeros((bp, CONV_W - 1, D_CONV), F32), jnp.zeros((bp, SHIFT_W), F32),
            jnp.zeros((bp, R_HEAD, D_R), F32), consts, final_norm_g, final, _attn_prompt, BF16)
        for lst, val in zip(st_p, (ckv, kpe, cst, sst, _state_from_lanes(rst))):
            lst.append(val)
        attend_s = functools.partial(_attn_sample_bound, page_table, cache_mla_ckv, cache_mla_kpe, l)
        xs, ckv, kpe, cst, sst, rst = _layer(
            xs, mod_s, tabs_s, p, vecs, state_conv[l], state_rwkv_shift[l],
            _state_to_lanes(state_rwkv[l]), consts, final_norm_g, final, attend_s, F32)
        for lst, val in zip(st_s, (ckv, kpe, cst, sst, _state_from_lanes(rst))):
            lst.append(val)
    return (xp, xs,
            jnp.stack(st_p[0]), jnp.stack(st_p[1]), jnp.stack(st_p[2]), jnp.stack(st_p[3]), jnp.stack(st_p[4]),
            jnp.stack(st_s[0]), jnp.stack(st_s[1]), jnp.stack(st_s[2]), jnp.stack(st_s[3]), jnp.stack(st_s[4]))


def _attn_sample_bound(page_table, cache_ckv, cache_kpe, layer, qlat, qpe, ckv_bf, kpe_bf, wuv_pad):
    return _attn_sample(page_table, qlat, qpe, ckv_bf, kpe_bf, wuv_pad, cache_ckv, cache_kpe, layer)
```

```python
import functools
import math

import jax
import jax.numpy as jnp
from jax import lax
from jax.experimental import pallas as pl
from jax.experimental.pallas import tpu as pltpu

F32 = jnp.float32
BF16 = jnp.bfloat16

D_MODEL = 1024
PAGE = 128
N_HEADS = 8
QK_NOPE = 64
QK_ROPE = 32
V_HEAD = 64
Q_LORA = 384
KV_LORA = 256
D_MLA = N_HEADS * V_HEAD
ROPE_THETA = 10000.0
ATTN_SCALE = (QK_NOPE + QK_ROPE) ** -0.5
Q_SCALE = ATTN_SCALE * math.log2(math.e)
D_CONV = 256
CONV_W = 3
R_HEADS = 4
R_HEAD = 64
D_R = R_HEADS * R_HEAD
W_LORA = 64
A_LORA = 64
SHIFT_W = 3 * D_R + W_LORA + A_LORA
GN_EPS = 64e-5
RMS_EPS = 1e-6
N_BRANCH = 3

LANE = 128
SUBLANE = 8
KR_W = LANE
SEG_W = (Q_LORA, KV_LORA, KR_W, D_MLA, 4 * D_CONV, SHIFT_W, D_R, N_BRANCH * D_MODEL)
SEG_OFF = tuple(sum(SEG_W[:i]) for i in range(len(SEG_W)))
PROJ_P = sum(SEG_W)
VMEM_LIMIT = 56 * 1024 * 1024
NEG_BIG = -1e30
SCAN_UNROLL = 16
SCAN_GROUPS = 2
PAGE_SLOTS = 3


def _cparams(sem):
    return pltpu.CompilerParams(dimension_semantics=sem, vmem_limit_bytes=VMEM_LIMIT)


def _dot(a, b):
    return jnp.dot(a, b, preferred_element_type=F32)


def _dot_nt(a, b):
    return lax.dot_general(a, b, (((1,), (1,)), ((), ())), preferred_element_type=F32)


def _sigmoid(x):
    return 1.0 / (1.0 + jnp.exp(-x))


def _silu(x):
    return x * _sigmoid(x)


def _softplus(x):
    return jnp.maximum(x, 0.0) + jnp.log1p(jnp.exp(-jnp.abs(x)))


def _seg_sum(x, bd):
    hi = x.astype(BF16)
    lo = (x - hi.astype(F32)).astype(BF16)
    return _dot(hi, bd) + _dot(lo, bd)


def _row_tiles(batch, seq, cap):
    if seq >= LANE:
        tt = min(seq, cap)
        assert seq % tt == 0
        return 1, tt
    assert seq % SUBLANE == 0
    return batch, seq


def _ada_kernel(c_ref, w_ref, b_ref, o_ref):
    s = _silu(c_ref[...])
    o_ref[...] = _dot(s.astype(BF16), w_ref[...]) + b_ref[...]


def _ada_mod(c, w_ada_bf, b_ada):
    bsz = c.shape[0]
    out = pl.pallas_call(
        _ada_kernel,
        out_shape=jax.ShapeDtypeStruct((bsz, 3 * D_MODEL), F32),
        compiler_params=pltpu.CompilerParams(vmem_limit_bytes=VMEM_LIMIT),
        name="ada_mod",
    )(c, w_ada_bf, b_ada.reshape(1, -1))
    return out.reshape(bsz, 1, 3 * D_MODEL)


N_MLA_IN, N_RWKV_IN = 10, 10
N_GATE_OUT = 5


def _front_kernel(x_ref, mod_ref, g_ref, w_ref, *refs):
    nb, tt, d = x_ref.shape
    mla_in = refs[:N_MLA_IN]
    rwkv_in = refs[N_MLA_IN:N_MLA_IN + N_RWKV_IN]
    outs = refs[N_MLA_IN + N_RWKV_IN:]
    zmla_out, cbz_out, ccx_out, zrw_out, gm_out = outs[:N_GATE_OUT]
    mla_out = outs[N_GATE_OUT:N_GATE_OUT + 6]
    rwkv_out = outs[N_GATE_OUT + 6:N_GATE_OUT + 15]
    buf = outs[N_GATE_OUT + 15]
    x = x_ref[...]
    ms = jnp.mean(x * x, axis=-1, keepdims=True)
    xn = x * lax.rsqrt(ms + RMS_EPS) * g_ref[...]
    shift = mod_ref[:, :, 0:D_MODEL]
    scale = mod_ref[:, :, D_MODEL:2 * D_MODEL]
    u = (xn * (1.0 + scale) + shift).reshape(nb * tt, d).astype(BF16)

    def seg(i):
        return _dot(u, w_ref[:, SEG_OFF[i]:SEG_OFF[i] + SEG_W[i]])

    zmla_out[...] = _silu(seg(3)).reshape(nb, tt, SEG_W[3]).astype(zmla_out.dtype)
    c4 = seg(4)
    cbz_out[...] = (c4[:, 0:D_CONV] * _silu(c4[:, 3 * D_CONV:])).reshape(nb, tt, D_CONV)
    ccx_out[...] = (c4[:, D_CONV:2 * D_CONV] * c4[:, 2 * D_CONV:3 * D_CONV]).reshape(nb, tt, D_CONV)
    zrw_out[...] = _silu(seg(6)).reshape(nb, tt, SEG_W[6]).astype(zrw_out.dtype)
    gm_out[...] = _sigmoid(seg(7)).reshape(nb, tt, SEG_W[7]).astype(gm_out.dtype)
    _mla_prep_body(seg(0), seg(1).reshape(nb, tt, KV_LORA), seg(2).reshape(nb, tt, KR_W), nb, tt,
                   *mla_in, *mla_out)
    _rwkv_prep_body(seg(5).reshape(nb, tt, SHIFT_W), *rwkv_in, *rwkv_out, buf)


def _front(x, mod, norm_g, w_in_p, tabs, q_norm_g, kv_norm_g, wn, wr, wrs, uk, q_dtype,
           shift_prev, mu, w0, w2p, a0, a2p, k_k, k_a, r_k, bd):
    bsz, seq, d = x.shape
    nb, tt = _row_tiles(bsz, seq, 256)
    grid = (bsz // nb, seq // tt)
    cos_q, sin_q, cos_k, sin_k = tabs
    row = lambda w: pl.BlockSpec((nb, tt, w), lambda b, t: (b, t, 0))
    tab = lambda w: pl.BlockSpec((tt, w), lambda b, t: (t, 0))
    full = lambda a: pl.BlockSpec(a.shape, lambda b, t: (0,) * a.ndim)
    hq = lambda w: pl.BlockSpec((nb, N_HEADS, tt, w), lambda b, t: (b, 0, t, 0))
    state = pl.BlockSpec((nb, 1, SHIFT_W), lambda b, t: (b, 0, 0))
    mla_consts = [q_norm_g.reshape(1, -1), kv_norm_g.reshape(1, -1), wn, wr, wrs, uk]
    rwkv_consts = [mu.reshape(1, -1), w0.reshape(1, -1), w2p, a0.reshape(1, -1), a2p,
                   k_k.reshape(1, -1), k_a.reshape(1, -1), r_k.reshape(1, -1), bd]
    assert 4 + len(mla_consts) == N_MLA_IN and len(rwkv_consts) + 1 == N_RWKV_IN
    g = norm_g.reshape(1, d)
    outs = pl.pallas_call(
        _front_kernel,
        grid=grid,
        in_specs=[row(d), pl.BlockSpec((nb, 1, 3 * d), lambda b, t: (b, 0, 0)), full(g), full(w_in_p),
                  tab(N_HEADS * QK_ROPE), tab(N_HEADS * QK_ROPE), tab(QK_ROPE), tab(QK_ROPE)]
                 + [full(a) for a in mla_consts] + [full(a) for a in rwkv_consts[:-1]] + [full(bd), state],
        out_specs=[row(SEG_W[3]), row(D_CONV), row(D_CONV), row(SEG_W[6]), row(SEG_W[7]),
                   hq(KV_LORA), hq(QK_ROPE), row(KV_LORA), row(QK_ROPE), row(KV_LORA), row(QK_ROPE)]
                  + [row(D_R)] * 8 + [state],
        out_shape=[jax.ShapeDtypeStruct((bsz, seq, SEG_W[3]), BF16),
                   jax.ShapeDtypeStruct((bsz, seq, D_CONV), F32),
                   jax.ShapeDtypeStruct((bsz, seq, D_CONV), F32),
                   jax.ShapeDtypeStruct((bsz, seq, SEG_W[6]), BF16),
                   jax.ShapeDtypeStruct((bsz, seq, SEG_W[7]), BF16)]
                  + [jax.ShapeDtypeStruct((bsz, N_HEADS, seq, KV_LORA), q_dtype),
                     jax.ShapeDtypeStruct((bsz, N_HEADS, seq, QK_ROPE), q_dtype),
                     jax.ShapeDtypeStruct((bsz, seq, KV_LORA), F32),
                     jax.ShapeDtypeStruct((bsz, seq, QK_ROPE), F32),
                     jax.ShapeDtypeStruct((bsz, seq, KV_LORA), BF16),
                     jax.ShapeDtypeStruct((bsz, seq, QK_ROPE), BF16)]
                  + [jax.ShapeDtypeStruct((bsz, seq, D_R), F32)] * 8
                  + [jax.ShapeDtypeStruct((bsz, 1, SHIFT_W), F32)],
        scratch_shapes=[pltpu.VMEM((nb, tt + SUBLANE, SHIFT_W), F32)],
        compiler_params=_cparams(("parallel", "arbitrary")),
        name="front",
    )(x, mod, g, w_in_p, cos_q, sin_q, cos_k, sin_k, *mla_consts, *rwkv_consts,
      shift_prev.reshape(bsz, 1, SHIFT_W))
    return outs


def _mla_prep_body(qa, kva, kr, nb, tt, cq_ref, sq_ref, ck_ref, sk_ref,
                   gq_ref, gkv_ref, wn_ref, wr_ref, wrs_ref, uk_ref,
                   qlat_ref, qpe_ref, ckv_ref, kpe_ref, ckvb_ref, kpeb_ref):
    rows = nb * tt
    cq = qa * lax.rsqrt(jnp.mean(qa * qa, axis=-1, keepdims=True) + RMS_EPS) * gq_ref[...]
    cqb = cq.astype(BF16)
    qn = _dot(cqb, wn_ref[...])
    cos_q = jnp.broadcast_to(cq_ref[...][None], (nb, tt, N_HEADS * QK_ROPE)).reshape(rows, -1)
    sin_q = jnp.broadcast_to(sq_ref[...][None], (nb, tt, N_HEADS * QK_ROPE)).reshape(rows, -1)
    qp = (_dot(cqb, wr_ref[...]) * cos_q + _dot(cqb, wrs_ref[...]) * sin_q) * Q_SCALE
    for h in range(N_HEADS):
        qn_h = qn[:, h * LANE:(h + 1) * LANE].astype(BF16)
        ql = _dot(qn_h, uk_ref[h]) * Q_SCALE
        qlat_ref[:, h] = ql.reshape(nb, tt, KV_LORA).astype(qlat_ref.dtype)
        qpe_ref[:, h] = qp[:, h * QK_ROPE:(h + 1) * QK_ROPE].reshape(nb, tt, QK_ROPE).astype(qpe_ref.dtype)
    ckv = kva * lax.rsqrt(jnp.mean(kva * kva, axis=-1, keepdims=True) + RMS_EPS) * gkv_ref[...]
    ckv_ref[...] = ckv
    ckvb_ref[...] = ckv.astype(BF16)
    kpe = kr[:, :, 0:QK_ROPE] * ck_ref[...][None] + kr[:, :, QK_ROPE:2 * QK_ROPE] * sk_ref[...][None]
    kpe_ref[...] = kpe
    kpeb_ref[...] = kpe.astype(BF16)


def _value_up_proj(o_lat, wuv_ref, t):
    tiles = []
    for j in range(N_HEADS // 2):
        h0, h1 = 2 * j, 2 * j + 1
        tiles.append(_dot(o_lat[h0 * t:(h0 + 1) * t].astype(BF16), wuv_ref[h0])
                     + _dot(o_lat[h1 * t:(h1 + 1) * t].astype(BF16), wuv_ref[h1]))
    return jnp.concatenate(tiles, axis=1)


def _softmax_update(s, m_old, l_old):
    m_new = jnp.maximum(m_old, jnp.max(s, axis=-1, keepdims=True))
    alpha = jnp.exp2(m_old - m_new)
    p = jnp.exp2(s - jnp.tile(m_new, (1, s.shape[1] // LANE)))
    l_new = alpha * l_old + jnp.sum(p, axis=-1, keepdims=True)
    return p, m_new, l_new, alpha


def _attn_prompt_kernel(ql_ref, qp_ref, ckv_ref, kpe_ref, wuv_ref, o_ref,
                        m_all, l_all, acc_all, s_all, p_all, *, tq, tk, rc, qt):
    for j in range(qt):
        _attn_prompt_tile(pl.program_id(1) * qt + j,
                          ql_ref.at[0, :, j * tq:(j + 1) * tq, :], qp_ref.at[0, :, j * tq:(j + 1) * tq, :],
                          ckv_ref, kpe_ref, wuv_ref, o_ref.at[0, j * tq:(j + 1) * tq, :],
                          m_all.at[j], l_all.at[j], acc_all.at[j], s_all.at[j], p_all.at[j],
                          tq=tq, tk=tk, rc=rc)


def _attn_prompt_tile(qi, ql_ref, qp_ref, ckv_ref, kpe_ref, wuv_ref, o_ref,
                      m_scr, l_scr, acc_scr, s_scr, p_scr, *, tq, tk, rc):
    rows = N_HEADS * tq
    ql = ql_ref[...].reshape(rows, KV_LORA)
    qp = qp_ref[...].reshape(rows, QK_ROPE)
    m_scr[...] = jnp.full(m_scr.shape, NEG_BIG, F32)
    l_scr[...] = jnp.zeros(l_scr.shape, F32)
    acc_scr[...] = jnp.zeros(acc_scr.shape, F32)
    n_kt = (qi * tq + tq + tk - 1) // tk

    def scores(kt, slot):
        start = pl.multiple_of(kt * tk, tk)
        s_scr[slot] = (_dot_nt(ql, ckv_ref[0, pl.ds(start, tk), :])
                       + _dot_nt(qp, kpe_ref[0, pl.ds(start, tk), :]))

    def softmax(kt, slot, masked):
        for c in range(rows // rc):
            rs = slice(c * rc, (c + 1) * rc)
            s = s_scr[slot, rs, :]
            if masked:
                q_pos = qi * tq + (c * rc) % tq + lax.broadcasted_iota(jnp.int32, (rc, tk), 0)
                k_pos = kt * tk + lax.broadcasted_iota(jnp.int32, (rc, tk), 1)
                s = jnp.where(k_pos <= q_pos, s, NEG_BIG)
            p, m_new, l_new, alpha = _softmax_update(s, m_scr[rs, :], l_scr[rs, :])
            m_scr[rs, :] = m_new
            l_scr[rs, :] = l_new
            p_scr[rs, :] = p.astype(BF16)
            acc_scr[rs, :] = jnp.tile(alpha, (1, KV_LORA // LANE)) * acc_scr[rs, :]

    def weighted_values(kt):
        start = pl.multiple_of(kt * tk, tk)
        acc_scr[...] += _dot(p_scr[...], ckv_ref[0, pl.ds(start, tk), :])

    last = n_kt - 1
    scores(last, 0)
    softmax(last, 0, True)
    scores(0, 1)
    weighted_values(last)

    def body(kt, carry):
        slot = (kt + 1) % 2
        softmax(kt, slot, False)
        scores(kt + 1, 1 - slot)
        weighted_values(kt)
        return carry

    lax.fori_loop(0, last - 1, body, 0)

    @pl.when(last >= 1)
    def _():
        softmax(last - 1, last % 2, False)
        weighted_values(last - 1)
    o_lat = acc_scr[...] / jnp.tile(l_scr[...], (1, KV_LORA // LANE))
    o_ref[...] = _value_up_proj(o_lat, wuv_ref, tq)


def _attn_prompt(qlat, qpe, ckv_bf, kpe_bf, wuv_pad):
    bsz, _, seq, _ = qlat.shape
    tq = 128
    tk = min(256, seq)
    qt = 2 if seq % (2 * tq) == 0 else 1
    assert tk % tq == 0 and seq % tk == 0
    rc = 64
    kern = functools.partial(_attn_prompt_kernel, tq=tq, tk=tk, rc=rc, qt=qt)
    rows = N_HEADS * tq
    return pl.pallas_call(
        kern,
        grid=(bsz, seq // (qt * tq)),
        in_specs=[pl.BlockSpec((1, N_HEADS, qt * tq, KV_LORA), lambda b, q: (b, 0, q, 0)),
                  pl.BlockSpec((1, N_HEADS, qt * tq, QK_ROPE), lambda b, q: (b, 0, q, 0)),
                  pl.BlockSpec((1, seq, KV_LORA), lambda b, q: (b, 0, 0)),
                  pl.BlockSpec((1, seq, QK_ROPE), lambda b, q: (b, 0, 0)),
                  pl.BlockSpec(wuv_pad.shape, lambda b, q: (0, 0, 0))],
        out_specs=pl.BlockSpec((1, qt * tq, D_MLA), lambda b, q: (b, q, 0)),
        out_shape=jax.ShapeDtypeStruct((bsz, seq, D_MLA), F32),
        scratch_shapes=[pltpu.VMEM((qt, rows, LANE), F32), pltpu.VMEM((qt, rows, LANE), F32),
                        pltpu.VMEM((qt, rows, KV_LORA), F32),
                        pltpu.VMEM((qt, 2, rows, tk), F32), pltpu.VMEM((qt, rows, tk), BF16)],
        compiler_params=_cparams(("parallel", "arbitrary")),
        name="attn_prompt",
    )(qlat, qpe, ckv_bf, kpe_bf, wuv_pad)


def _attn_sample_kernel(pt_ref, ql_ref, qp_ref, ckvn_ref, kpen_ref, wuv_ref, cache_ckv, cache_kpe_t,
                        o_ref, ckv_buf, kpe_buf, sem, *, layer, pp, cp, n_steps, ts):
    ahead = PAGE_SLOTS - 1
    b = pl.program_id(0)
    nb = pl.num_programs(0)
    rows = N_HEADS * ts
    total = nb * n_steps
    ql = ql_ref[0].reshape(rows, KV_LORA).astype(BF16)
    qp = qp_ref[0].reshape(rows, QK_ROPE).astype(BF16)

    def page_copies(seq, grp, slot):
        out = []
        for i in range(pp):
            page = pt_ref[seq, grp * pp + i]
            out.append(pltpu.make_async_copy(cache_ckv.at[layer, page], ckv_buf.at[slot, i], sem.at[slot]))
            out.append(pltpu.make_async_copy(cache_kpe_t.at[layer, page], kpe_buf.at[slot, i], sem.at[slot]))
        return out

    def start_group(g):
        slot = lax.rem(g, PAGE_SLOTS)
        g = jnp.minimum(g, total - 1)
        seq = lax.shift_right_logical(g, n_steps.bit_length() - 1)
        for c in page_copies(seq, jnp.bitwise_and(g, n_steps - 1), slot):
            c.start()

    def wait_group(g):
        for c in page_copies(0, 0, lax.rem(g, PAGE_SLOTS)):
            c.wait()

    @pl.when(b == 0)
    def _():
        for g in range(ahead):
            start_group(jnp.int32(g))

    def partial_softmax(s, vals):
        m = jnp.max(s, axis=-1, keepdims=True)
        p = jnp.exp2(s - m)
        return m, jnp.sum(p, axis=-1, keepdims=True), _dot(p.astype(BF16), vals)

    def merge(parts):
        m = functools.reduce(jnp.maximum, [pm for pm, _, _ in parts])
        l = sum(pl_ * jnp.exp2(pm - m) for pm, pl_, _ in parts)
        acc = sum(pa * jnp.exp2(pm - m) for pm, _, pa in parts)
        return m, l, acc

    def group_parts(slot):
        n_c = pp // cp
        cks = [ckv_buf[slot, c * cp:(c + 1) * cp].reshape(cp * PAGE, KV_LORA).astype(BF16) for c in range(n_c)]
        kps = [jnp.concatenate([kpe_buf[slot, c * cp + i].astype(BF16) for i in range(cp)], axis=1)
               for c in range(n_c)]
        ss = [_dot_nt(ql, ck) + _dot(qp, kp_t) for ck, kp_t in zip(cks, kps)]
        ms = [jnp.max(s, axis=-1, keepdims=True) for s in ss]
        ps = [jnp.exp2(s - m) for s, m in zip(ss, ms)]
        ls = [jnp.sum(p, axis=-1, keepdims=True) for p in ps]
        accs = [_dot(p.astype(BF16), ck) for p, ck in zip(ps, cks)]
        return list(zip(ms, ls, accs))

    ckn = ckvn_ref[0]
    kpn = kpen_ref[0]
    s_new = _dot_nt(ql, ckn) + _dot_nt(qp, kpn)
    t_q = lax.broadcasted_iota(jnp.int32, (rows, ts), 0) % ts
    t_k = lax.broadcasted_iota(jnp.int32, (rows, ts), 1)
    state = partial_softmax(jnp.where(t_k <= t_q, s_new, NEG_BIG), ckn)

    for st in range(n_steps):
        g = b * n_steps + st
        start_group(g + ahead)
        wait_group(g)
        state = merge([state] + group_parts(lax.rem(g, PAGE_SLOTS)))

    @pl.when(b == nb - 1)
    def _():
        for g in range(ahead):
            wait_group(total + g)

    _, l, acc = state
    o_ref[0] = _value_up_proj(acc / l, wuv_ref, ts)


def _attn_sample(page_table, qlat, qpe, ckv_new_bf, kpe_new_bf, wuv_pad, cache_ckv, cache_kpe, layer):
    bsz, _, ts, _ = qlat.shape
    n_pages = page_table.shape[1]
    pp = min(32, n_pages // 2)
    cp = min(8, pp)
    n_steps = n_pages // pp
    assert n_pages % pp == 0 and pp % cp == 0
    assert n_steps & (n_steps - 1) == 0
    kern = functools.partial(_attn_sample_kernel, layer=layer, pp=pp, cp=cp, n_steps=n_steps, ts=ts)

    cache_kpe_t = jnp.swapaxes(cache_kpe, 2, 3)

    grid_spec = pltpu.PrefetchScalarGridSpec(
        num_scalar_prefetch=1,
        grid=(bsz,),
        in_specs=[pl.BlockSpec((1, N_HEADS, ts, KV_LORA), lambda b, pt: (b, 0, 0, 0)),
                  pl.BlockSpec((1, N_HEADS, ts, QK_ROPE), lambda b, pt: (b, 0, 0, 0)),
                  pl.BlockSpec((1, ts, KV_LORA), lambda b, pt: (b, 0, 0)),
                  pl.BlockSpec((1, ts, QK_ROPE), lambda b, pt: (b, 0, 0)),
                  pl.BlockSpec(wuv_pad.shape, lambda b, pt: (0, 0, 0)),
                  pl.BlockSpec(memory_space=pl.ANY),
                  pl.BlockSpec(memory_space=pl.ANY)],
        out_specs=pl.BlockSpec((1, ts, D_MLA), lambda b, pt: (b, 0, 0)),
        scratch_shapes=[pltpu.VMEM((PAGE_SLOTS, pp, PAGE, KV_LORA), F32),
                        pltpu.VMEM((PAGE_SLOTS, pp, QK_ROPE, PAGE), F32),
                        pltpu.SemaphoreType.DMA((PAGE_SLOTS,))],
    )
    return pl.pallas_call(
        kern,
        grid_spec=grid_spec,
        out_shape=jax.ShapeDtypeStruct((bsz, ts, D_MLA), F32),
        compiler_params=_cparams(("arbitrary",)),
        name="attn_sample",
    )(page_table, qlat, qpe, ckv_new_bf, kpe_new_bf, wuv_pad, cache_ckv, cache_kpe_t)


def _rwkv_prep_body(rw, mu_ref, w0_ref, w2_ref, a0_ref, a2_ref, kk_ref, ka_ref, rk_ref, bd_ref, sprev_ref,
                    q_out, w_out, k_out, v_out, kk_out, b_out, vkr_out, bonus_out, shift_out, buf):
    nb, tt, _ = rw.shape
    ti = pl.program_id(1)
    rows = nb * tt

    @pl.when(ti == 0)
    def _():
        buf[:, SUBLANE - 1:SUBLANE, :] = sprev_ref[...]

    @pl.when(ti > 0)
    def _():
        buf[:, SUBLANE - 1:SUBLANE, :] = buf[:, tt + SUBLANE - 1:tt + SUBLANE, :]

    buf[:, SUBLANE:, :] = rw
    shift_out[...] = rw[:, tt - 1:tt, :]
    rw_prev = buf[:, SUBLANE - 1:SUBLANE - 1 + tt, :]
    rws = (rw + mu_ref[...] * (rw_prev - rw)).reshape(rows, SHIFT_W)
    r = rws[:, 0:D_R]
    k = rws[:, D_R:2 * D_R]
    v = rws[:, 2 * D_R:3 * D_R]
    wa = rws[:, 3 * D_R:]
    w_log = -_softplus(-(w0_ref[...] + _dot(jnp.tanh(wa).astype(BF16), w2_ref[...]))) - 0.5
    decay = jnp.exp(-jnp.exp(w_log))
    a = _sigmoid(a0_ref[...] + _dot(wa.astype(BF16), a2_ref[...]))
    bd = bd_ref[...]
    kk = k * kk_ref[...]
    kk = kk / jnp.maximum(jnp.sqrt(_seg_sum(kk * kk, bd)), 1e-12)
    k = k * (1.0 + (a - 1.0) * ka_ref[...])
    bonus = _seg_sum(r * k * rk_ref[...], bd) * v
    b = kk * a
    q = decay * r - kk * _seg_sum(b * r, bd)
    vkr = v * _seg_sum(k * r, bd)
    shp = (nb, tt, D_R)
    q_out[...] = q.reshape(shp)
    w_out[...] = decay.reshape(shp)
    k_out[...] = k.reshape(shp)
    v_out[...] = v.reshape(shp)
    kk_out[...] = kk.reshape(shp)
    b_out[...] = b.reshape(shp)
    vkr_out[...] = vkr.reshape(shp)
    bonus_out[...] = bonus.reshape(shp)


def _rwkv_scan_kernel(q_ref, w_ref, k_ref, v_ref, kk_ref, b_ref, vkr_ref, s0_ref, bd_ref, eye_ref,
                      y_ref, sT_ref, s_scr):
    nb, tc, _ = q_ref.shape
    ci = pl.program_id(1)
    rows = nb * R_HEAD

    @pl.when(ci == 0)
    def _():
        s_scr[...] = s0_ref[...]

    bd = bd_ref[...]
    eye = eye_ref[...][None]
    eye_bf = eye.astype(BF16)

    gb = max(nb // SCAN_GROUPS, 1)

    def seg(x):
        return _dot(x.reshape(gb * R_HEAD, D_R), bd).reshape(gb, R_HEAD, D_R)

    def step(t, carry):
        for g0 in range(0, nb, gb):
            grp = slice(g0, g0 + gb)
            row = lambda ref: ref[grp, pl.ds(t, 1), :]
            s = s_scr[grp]
            s_bf = s.astype(BF16)
            sa = seg(s_bf * row(kk_ref).astype(BF16))
            y_col = seg(s_bf * row(q_ref).astype(BF16))
            v_col = seg(eye_bf * row(v_ref).astype(BF16))
            s_scr[grp] = s * row(w_ref) - sa * row(b_ref) + v_col * row(k_ref)
            y_ref[grp, pl.ds(t, 1), :] = jnp.sum(y_col * eye, axis=1, keepdims=True) + row(vkr_ref)
        return carry

    lax.fori_loop(0, tc, step, 0, unroll=SCAN_UNROLL)

    @pl.when(ci == pl.num_programs(1) - 1)
    def _():
        sT_ref[...] = s_scr[...]


def _rwkv_scan(q, w, k, v, kk, b, vkr, s0, bd, eye):
    bsz, seq, _ = q.shape
    nb = math.gcd(bsz, 8)
    tc = min(seq, 256)
    row = pl.BlockSpec((nb, tc, D_R), lambda bi, c: (bi, c, 0))
    st = pl.BlockSpec((nb, R_HEAD, D_R), lambda bi, c: (bi, 0, 0))
    return pl.pallas_call(
        _rwkv_scan_kernel,
        grid=(bsz // nb, seq // tc),
        in_specs=[row] * 7 + [st, pl.BlockSpec(bd.shape, lambda bi, c: (0, 0)),
                              pl.BlockSpec(eye.shape, lambda bi, c: (0, 0))],
        out_specs=[row, st],
        out_shape=[jax.ShapeDtypeStruct((bsz, seq, D_R), F32),
                   jax.ShapeDtypeStruct((bsz, R_HEAD, D_R), F32)],
        scratch_shapes=[pltpu.VMEM((nb, R_HEAD, D_R), F32)],
        compiler_params=_cparams(("parallel", "arbitrary")),
        name="rwkv_scan",
    )(q, w, k, v, kk, b, vkr, s0, bd, eye)


def _out_kernel(x_ref, mod_ref, omla_ref, szmla_ref, cbz_ref, ccx_ref, cprev_ref, szrw_ref,
                yr_ref, bonus_ref, gm_ref, wmla_ref, wconv_ref, wrw_ref, wout_ref, cw_ref,
                gng_ref, gnb_ref, bd_ref, fg_ref, xo_ref, cstate_ref, buf, *, final):
    nb, tt, d = x_ref.shape
    ti = pl.program_id(1)
    rows = nb * tt

    @pl.when(ti == 0)
    def _():
        buf[:, SUBLANE - 2:SUBLANE, :] = cprev_ref[...]

    @pl.when(ti > 0)
    def _():
        buf[:, SUBLANE - 2:SUBLANE, :] = buf[:, tt + SUBLANE - 2:tt + SUBLANE, :]

    buf[:, SUBLANE:, :] = ccx_ref[...]
    cstate_ref[...] = buf[:, tt + SUBLANE - 2:tt + SUBLANE, :]
    conv = (buf[:, SUBLANE - 2:SUBLANE - 2 + tt, :] * cw_ref[0:1, :]
            + buf[:, SUBLANE - 1:SUBLANE - 1 + tt, :] * cw_ref[1:2, :]
            + buf[:, SUBLANE:, :] * cw_ref[2:3, :])
    y_conv = _dot((cbz_ref[...] * conv).reshape(rows, D_CONV).astype(BF16), wconv_ref[...])

    y_mla = _dot((omla_ref[...] * szmla_ref[...].astype(F32)).reshape(rows, D_MLA).astype(BF16), wmla_ref[...])

    bd = bd_ref[...]
    yr = yr_ref[...].reshape(rows, D_R)
    mu = _seg_sum(yr, bd) * (1.0 / R_HEAD)
    dy = yr - mu
    var = _seg_sum(dy * dy, bd) * (1.0 / R_HEAD)
    yn = dy * lax.rsqrt(var + GN_EPS) * gng_ref[...] + gnb_ref[...]
    o_rw = yn + bonus_ref[...].reshape(rows, D_R)
    y_rw = _dot((o_rw * szrw_ref[...].astype(F32).reshape(rows, D_R)).astype(BF16), wrw_ref[...])

    g = gm_ref[...].astype(F32).reshape(rows, N_BRANCH * d)
    merged = g[:, 0:d] * y_mla + g[:, d:2 * d] * y_conv + g[:, 2 * d:] * y_rw
    delta = _dot(merged.astype(BF16), wout_ref[...]).reshape(nb, tt, d)
    xo = x_ref[...] + mod_ref[:, :, 2 * d:] * delta
    if final:
        xo = xo * lax.rsqrt(jnp.mean(xo * xo, axis=-1, keepdims=True) + RMS_EPS) * fg_ref[...]
    xo_ref[...] = xo


def _out_proj(x, mod, o_mla, sz_mla, cbz, ccx, conv_prev, sz_rw, y_r, bonus, gm,
              wmla, wconv, wrw, wout, conv_w, gn_g, gn_b, bd, final_g, final):
    bsz, seq, d = x.shape
    nb, tt = _row_tiles(bsz, seq, 256)
    grid = (bsz // nb, seq // tt)
    row = lambda w: pl.BlockSpec((nb, tt, w), lambda b, t: (b, t, 0))
    full = lambda a: pl.BlockSpec(a.shape, lambda b, t: (0,) * a.ndim)
    consts = [wmla, wconv, wrw, wout, conv_w, gn_g.reshape(1, -1), gn_b.reshape(1, -1), bd,
              final_g.reshape(1, -1)]
    return pl.pallas_call(
        functools.partial(_out_kernel, final=final),
        grid=grid,
        in_specs=[row(d), pl.BlockSpec((nb, 1, 3 * d), lambda b, t: (b, 0, 0)),
                  row(D_MLA), row(D_MLA), row(D_CONV), row(D_CONV),
                  pl.BlockSpec((nb, CONV_W - 1, D_CONV), lambda b, t: (b, 0, 0)),
                  row(D_R), row(D_R), row(D_R), row(N_BRANCH * d)] + [full(a) for a in consts],
        out_specs=[row(d), pl.BlockSpec((nb, CONV_W - 1, D_CONV), lambda b, t: (b, 0, 0))],
        out_shape=[jax.ShapeDtypeStruct((bsz, seq, d), F32),
                   jax.ShapeDtypeStruct((bsz, CONV_W - 1, D_CONV), F32)],
        scratch_shapes=[pltpu.VMEM((nb, tt + SUBLANE, D_CONV), F32)],
        compiler_params=_cparams(("parallel", "arbitrary")),
        name="out_proj",
    )(x, mod, o_mla, sz_mla, cbz, ccx, conv_prev, sz_rw, y_r, bonus, gm, *consts)


def _swap_halves(w):
    half = QK_ROPE // 2
    return jnp.concatenate([w[..., half:], w[..., :half]], axis=-1)


def _pad_w_in(w_in):
    a = Q_LORA + KV_LORA
    k_rope = w_in[:, :, a:a + QK_ROPE]
    tail = w_in.shape[2] - a - QK_ROPE

    def place(piece, at):
        return jnp.pad(piece, ((0, 0), (0, 0), (at, PROJ_P - at - piece.shape[2])))

    return (place(w_in[:, :, :a], 0) + place(k_rope, a) + place(_swap_halves(k_rope), a + QK_ROPE)
            + place(w_in[:, :, a + QK_ROPE:], PROJ_P - tail)).astype(BF16)


def _layer_params(l, w_ada, w_in_p, w_q_b, w_uk, w_uv, w_mla_out, w_conv_out, rwkv_w2, rwkv_a2,
                  w_rwkv_out, w_out):
    w_in_p = w_in_p[l]
    wq = w_q_b[l].reshape(Q_LORA, N_HEADS, QK_NOPE + QK_ROPE)
    wn = jnp.pad(wq[:, :, :QK_NOPE], ((0, 0), (0, 0), (0, LANE - QK_NOPE))).reshape(Q_LORA, N_HEADS * LANE)
    wr = wq[:, :, QK_NOPE:]
    wrs = _swap_halves(wr)
    uk = jnp.pad(jnp.transpose(w_uk[l], (1, 2, 0)), ((0, 0), (0, LANE - QK_NOPE), (0, 0)))
    wuv = jnp.transpose(w_uv[l], (1, 0, 2))
    zv = jnp.zeros_like(wuv)
    even = (jnp.arange(N_HEADS) % 2 == 0)[:, None, None]
    wuv_pad = jnp.where(even, jnp.concatenate([wuv, zv], axis=2), jnp.concatenate([zv, wuv], axis=2))
    zeros = jnp.zeros((W_LORA, D_R), F32)
    return dict(
        w_ada=w_ada[l].astype(BF16), w_in_p=w_in_p,
        wn=wn.astype(BF16), wr=wr.reshape(Q_LORA, -1).astype(BF16), wrs=wrs.reshape(Q_LORA, -1).astype(BF16),
        uk=uk.astype(BF16), wuv_pad=wuv_pad.astype(BF16),
        wmla=w_mla_out[l].astype(BF16), wconv=w_conv_out[l].astype(BF16),
        wrw=w_rwkv_out[l].astype(BF16), wout=w_out[l].astype(BF16),
        w2p=jnp.concatenate([rwkv_w2[l], zeros], axis=0).astype(BF16),
        a2p=jnp.concatenate([zeros, rwkv_a2[l]], axis=0).astype(BF16),
    )


def _rope_tables(pos):
    half = QK_ROPE // 2
    inv = ROPE_THETA ** (-jnp.arange(half, dtype=F32) / half)
    ang = pos.astype(F32)[:, None] * inv[None, :]
    cos, sin = jnp.cos(ang), jnp.sin(ang)
    cos_k = jnp.concatenate([cos, cos], axis=1)
    sin_k = jnp.concatenate([-sin, sin], axis=1)
    return jnp.tile(cos_k, (1, N_HEADS)), jnp.tile(sin_k, (1, N_HEADS)), cos_k, sin_k


def _state_to_lanes(s):
    b = s.shape[0]
    return jnp.transpose(s, (0, 2, 1, 3)).reshape(b, R_HEAD, D_R)


def _state_from_lanes(s):
    b = s.shape[0]
    return jnp.transpose(s.reshape(b, R_HEAD, R_HEADS, R_HEAD), (0, 2, 1, 3))


def _layer(x, c_mod, tabs, p, vecs, conv_prev, shift_prev, s0, consts, final_g, final, attend, q_dtype):
    bd, eye = consts
    (sz_mla, cbz, ccx, sz_rw, gm, qlat, qpe, ckv, kpe, ckv_bf, kpe_bf,
     q, w, k, v, kk, b, vkr, bonus, shift_state) = _front(
        x, c_mod, vecs["norm_g"], p["w_in_p"], tabs, vecs["q_norm_g"], vecs["kv_norm_g"],
        p["wn"], p["wr"], p["wrs"], p["uk"], q_dtype,
        shift_prev, vecs["mu"], vecs["w0"], p["w2p"], vecs["a0"], p["a2p"],
        vecs["k_k"], vecs["k_a"], vecs["r_k"], bd)
    o_mla = attend(qlat, qpe, ckv_bf, kpe_bf, p["wuv_pad"])
    y_r, s_new = _rwkv_scan(q, w, k, v, kk, b, vkr, s0, bd, eye)
    x_new, conv_state = _out_proj(
        x, c_mod, o_mla, sz_mla, cbz, ccx, conv_prev, sz_rw, y_r, bonus, gm,
        p["wmla"], p["wconv"], p["wrw"], p["wout"], vecs["conv_w"], vecs["gn_g"], vecs["gn_b"], bd,
        final_g, final)
    return x_new, ckv, kpe, conv_state, shift_state[:, 0], s_new


def kernel(x_prompt, x_sample, cache_mla_ckv, cache_mla_kpe, state_conv, state_rwkv_shift, state_rwkv, page_table, c_prompt, c_sample, norm_g, w_ada, b_ada, w_in, q_norm_g, w_q_b, kv_norm_g, w_uk, w_uv, w_mla_out, conv_w, w_conv_out, rwkv_mu, rwkv_w0, rwkv_w2, rwkv_a0, rwkv_a2, rwkv_k_k, rwkv_k_a, rwkv_r_k, rwkv_gn_g, rwkv_gn_b, w_rwkv_out, w_out, final_norm_g):
    depth = norm_g.shape[0]
    bp, tp, _ = x_prompt.shape
    bs, ts, _ = x_sample.shape
    past = page_table.shape[1] * PAGE
    tabs_p = _rope_tables(jnp.arange(tp, dtype=jnp.int32))
    tabs_s = _rope_tables(past + jnp.arange(ts, dtype=jnp.int32))
    seg = jnp.arange(D_R, dtype=jnp.int32) // R_HEAD
    bd = (seg[:, None] == seg[None, :]).astype(BF16)
    eye = (jnp.arange(R_HEAD, dtype=jnp.int32)[:, None] == (jnp.arange(D_R, dtype=jnp.int32) % R_HEAD)[None, :]).astype(F32)
    consts = (bd, eye)
    w_in = _pad_w_in(w_in)
    xp, xs = x_prompt, x_sample
    st_p = ([], [], [], [], [])
    st_s = ([], [], [], [], [])
    for l in range(depth):
        final = l == depth - 1
        p = _layer_params(l, w_ada, w_in, w_q_b, w_uk, w_uv, w_mla_out, w_conv_out, rwkv_w2, rwkv_a2,
                          w_rwkv_out, w_out)
        vecs = dict(norm_g=norm_g[l], q_norm_g=q_norm_g[l], kv_norm_g=kv_norm_g[l], mu=rwkv_mu[l],
                    w0=rwkv_w0[l], a0=rwkv_a0[l], k_k=rwkv_k_k[l], k_a=rwkv_k_a[l],
                    r_k=rwkv_r_k[l].reshape(-1), conv_w=conv_w[l], gn_g=rwkv_gn_g[l], gn_b=rwkv_gn_b[l])
        mod_p = _ada_mod(c_prompt, p["w_ada"], b_ada[l])
        mod_s = _ada_mod(c_sample, p["w_ada"], b_ada[l])
        xp, ckv, kpe, cst, sst, rst = _layer(
            xp, mod_p, tabs_p, p, vecs,
            jnp.zeros((bp, CONV_W - 1, D_CONV), F32), jnp.zeros((bp, SHIFT_W), F32),
            jnp.zeros((bp, R_HEAD, D_R), F32), consts, final_norm_g, final, _attn_prompt, BF16)
        for lst, val in zip(st_p, (ckv, kpe, cst, sst, _state_from_lanes(rst))):
            lst.append(val)
        attend_s = functools.partial(_attn_sample_bound, page_table, cache_mla_ckv, cache_mla_kpe, l)
        xs, ckv, kpe, cst, sst, rst = _layer(
            xs, mod_s, tabs_s, p, vecs, state_conv[l], state_rwkv_shift[l],
            _state_to_lanes(state_rwkv[l]), consts, final_norm_g, final, attend_s, F32)
        for lst, val in zip(st_s, (ckv, kpe, cst, sst, _state_from_lanes(rst))):
            lst.append(val)
    return (xp, xs,
            jnp.stack(st_p[0]), jnp.stack(st_p[1]), jnp.stack(st_p[2]), jnp.stack(st_p[3]), jnp.stack(st_p[4]),
            jnp.stack(st_s[0]), jnp.stack(st_s[1]), jnp.stack(st_s[2]), jnp.stack(st_s[3]), jnp.stack(st_s[4]))


def _attn_sample_bound(page_table, cache_ckv, cache_kpe, layer, qlat, qpe, ckv_bf, kpe_bf, wuv_pad):
    return _attn_sample(page_table, qlat, qpe, ckv_bf, kpe_bf, wuv_pad, cache_ckv, cache_kpe, layer)
```

```python
import functools
import math

import jax
import jax.numpy as jnp
from jax import lax
from jax.experimental import pallas as pl
from jax.experimental.pallas import tpu as pltpu

F32 = jnp.float32
BF16 = jnp.bfloat16

D_MODEL = 1024
PAGE = 128
N_HEADS = 8
QK_NOPE = 64
QK_ROPE = 32
V_HEAD = 64
Q_LORA = 384
KV_LORA = 256
D_MLA = N_HEADS * V_HEAD
ROPE_THETA = 10000.0
ATTN_SCALE = (QK_NOPE + QK_ROPE) ** -0.5
Q_SCALE = ATTN_SCALE * math.log2(math.e)
D_CONV = 256
CONV_W = 3
R_HEADS = 4
R_HEAD = 64
D_R = R_HEADS * R_HEAD
W_LORA = 64
A_LORA = 64
SHIFT_W = 3 * D_R + W_LORA + A_LORA
GN_EPS = 64e-5
RMS_EPS = 1e-6
N_BRANCH = 3

LANE = 128
SUBLANE = 8
KR_W = LANE
SEG_W = (Q_LORA, KV_LORA, KR_W, D_MLA, 4 * D_CONV, SHIFT_W, D_R, N_BRANCH * D_MODEL)
SEG_OFF = tuple(sum(SEG_W[:i]) for i in range(len(SEG_W)))
PROJ_P = sum(SEG_W)
VMEM_LIMIT = 56 * 1024 * 1024
NEG_BIG = -1e30
SCAN_UNROLL = 16
SCAN_GROUPS = 2
PAGE_SLOTS = 4


def _cparams(sem):
    return pltpu.CompilerParams(dimension_semantics=sem, vmem_limit_bytes=VMEM_LIMIT)


def _dot(a, b):
    return jnp.dot(a, b, preferred_element_type=F32)


def _dot_nt(a, b):
    return lax.dot_general(a, b, (((1,), (1,)), ((), ())), preferred_element_type=F32)


def _sigmoid(x):
    return 1.0 / (1.0 + jnp.exp(-x))


def _silu(x):
    return x * _sigmoid(x)


def _softplus(x):
    return jnp.maximum(x, 0.0) + jnp.log1p(jnp.exp(-jnp.abs(x)))


def _seg_sum(x, bd):
    hi = x.astype(BF16)
    lo = (x - hi.astype(F32)).astype(BF16)
    return _dot(hi, bd) + _dot(lo, bd)


def _row_tiles(batch, seq, cap):
    if seq >= LANE:
        tt = min(seq, cap)
        assert seq % tt == 0
        return 1, tt
    assert seq % SUBLANE == 0
    return batch, seq


def _ada_kernel(c_ref, w_ref, b_ref, o_ref):
    s = _silu(c_ref[...])
    o_ref[...] = _dot(s.astype(BF16), w_ref[...]) + b_ref[...]


def _ada_mod(c, w_ada_bf, b_ada):
    bsz = c.shape[0]
    out = pl.pallas_call(
        _ada_kernel,
        out_shape=jax.ShapeDtypeStruct((bsz, 3 * D_MODEL), F32),
        compiler_params=pltpu.CompilerParams(vmem_limit_bytes=VMEM_LIMIT),
        name="ada_mod",
    )(c, w_ada_bf, b_ada.reshape(1, -1))
    return out.reshape(bsz, 1, 3 * D_MODEL)


N_MLA_IN, N_RWKV_IN = 10, 10
N_GATE_OUT = 5


def _front_kernel(x_ref, mod_ref, g_ref, w_ref, *refs):
    nb, tt, d = x_ref.shape
    mla_in = refs[:N_MLA_IN]
    rwkv_in = refs[N_MLA_IN:N_MLA_IN + N_RWKV_IN]
    outs = refs[N_MLA_IN + N_RWKV_IN:]
    zmla_out, cbz_out, ccx_out, zrw_out, gm_out = outs[:N_GATE_OUT]
    mla_out = outs[N_GATE_OUT:N_GATE_OUT + 6]
    rwkv_out = outs[N_GATE_OUT + 6:N_GATE_OUT + 15]
    buf = outs[N_GATE_OUT + 15]
    x = x_ref[...]
    ms = jnp.mean(x * x, axis=-1, keepdims=True)
    xn = x * lax.rsqrt(ms + RMS_EPS) * g_ref[...]
    shift = mod_ref[:, :, 0:D_MODEL]
    scale = mod_ref[:, :, D_MODEL:2 * D_MODEL]
    u = (xn * (1.0 + scale) + shift).reshape(nb * tt, d).astype(BF16)

    def seg(i):
        return _dot(u, w_ref[:, SEG_OFF[i]:SEG_OFF[i] + SEG_W[i]])

    zmla_out[...] = _silu(seg(3)).reshape(nb, tt, SEG_W[3]).astype(zmla_out.dtype)
    c4 = seg(4)
    cbz_out[...] = (c4[:, 0:D_CONV] * _silu(c4[:, 3 * D_CONV:])).reshape(nb, tt, D_CONV)
    ccx_out[...] = (c4[:, D_CONV:2 * D_CONV] * c4[:, 2 * D_CONV:3 * D_CONV]).reshape(nb, tt, D_CONV)
    zrw_out[...] = _silu(seg(6)).reshape(nb, tt, SEG_W[6]).astype(zrw_out.dtype)
    gm_out[...] = _sigmoid(seg(7)).reshape(nb, tt, SEG_W[7]).astype(gm_out.dtype)
    _mla_prep_body(seg(0), seg(1).reshape(nb, tt, KV_LORA), seg(2).reshape(nb, tt, KR_W), nb, tt,
                   *mla_in, *mla_out)
    _rwkv_prep_body(seg(5).reshape(nb, tt, SHIFT_W), *rwkv_in, *rwkv_out, buf)


def _front(x, mod, norm_g, w_in_p, tabs, q_norm_g, kv_norm_g, wn, wr, wrs, uk, q_dtype,
           shift_prev, mu, w0, w2p, a0, a2p, k_k, k_a, r_k, bd):
    bsz, seq, d = x.shape
    nb, tt = _row_tiles(bsz, seq, 256)
    grid = (bsz // nb, seq // tt)
    cos_q, sin_q, cos_k, sin_k = tabs
    row = lambda w: pl.BlockSpec((nb, tt, w), lambda b, t: (b, t, 0))
    tab = lambda w: pl.BlockSpec((tt, w), lambda b, t: (t, 0))
    full = lambda a: pl.BlockSpec(a.shape, lambda b, t: (0,) * a.ndim)
    hq = lambda w: pl.BlockSpec((nb, N_HEADS, tt, w), lambda b, t: (b, 0, t, 0))
    state = pl.BlockSpec((nb, 1, SHIFT_W), lambda b, t: (b, 0, 0))
    mla_consts = [q_norm_g.reshape(1, -1), kv_norm_g.reshape(1, -1), wn, wr, wrs, uk]
    rwkv_consts = [mu.reshape(1, -1), w0.reshape(1, -1), w2p, a0.reshape(1, -1), a2p,
                   k_k.reshape(1, -1), k_a.reshape(1, -1), r_k.reshape(1, -1), bd]
    assert 4 + len(mla_consts) == N_MLA_IN and len(rwkv_consts) + 1 == N_RWKV_IN
    g = norm_g.reshape(1, d)
    outs = pl.pallas_call(
        _front_kernel,
        grid=grid,
        in_specs=[row(d), pl.BlockSpec((nb, 1, 3 * d), lambda b, t: (b, 0, 0)), full(g), full(w_in_p),
                  tab(N_HEADS * QK_ROPE), tab(N_HEADS * QK_ROPE), tab(QK_ROPE), tab(QK_ROPE)]
                 + [full(a) for a in mla_consts] + [full(a) for a in rwkv_consts[:-1]] + [full(bd), state],
        out_specs=[row(SEG_W[3]), row(D_CONV), row(D_CONV), row(SEG_W[6]), row(SEG_W[7]),
                   hq(KV_LORA), hq(QK_ROPE), row(KV_LORA), row(QK_ROPE), row(KV_LORA), row(QK_ROPE)]
                  + [row(D_R)] * 8 + [state],
        out_shape=[jax.ShapeDtypeStruct((bsz, seq, SEG_W[3]), BF16),
                   jax.ShapeDtypeStruct((bsz, seq, D_CONV), F32),
                   jax.ShapeDtypeStruct((bsz, seq, D_CONV), F32),
                   jax.ShapeDtypeStruct((bsz, seq, SEG_W[6]), BF16),
                   jax.ShapeDtypeStruct((bsz, seq, SEG_W[7]), BF16)]
                  + [jax.ShapeDtypeStruct((bsz, N_HEADS, seq, KV_LORA), q_dtype),
                     jax.ShapeDtypeStruct((bsz, N_HEADS, seq, QK_ROPE), q_dtype),
                     jax.ShapeDtypeStruct((bsz, seq, KV_LORA), F32),
                     jax.ShapeDtypeStruct((bsz, seq, QK_ROPE), F32),
                     jax.ShapeDtypeStruct((bsz, seq, KV_LORA), BF16),
                     jax.ShapeDtypeStruct((bsz, seq, QK_ROPE), BF16)]
                  + [jax.ShapeDtypeStruct((bsz, seq, D_R), F32)] * 8
                  + [jax.ShapeDtypeStruct((bsz, 1, SHIFT_W), F32)],
        scratch_shapes=[pltpu.VMEM((nb, tt + SUBLANE, SHIFT_W), F32)],
        compiler_params=_cparams(("parallel", "arbitrary")),
        name="front",
    )(x, mod, g, w_in_p, cos_q, sin_q, cos_k, sin_k, *mla_consts, *rwkv_consts,
      shift_prev.reshape(bsz, 1, SHIFT_W))
    return outs


def _mla_prep_body(qa, kva, kr, nb, tt, cq_ref, sq_ref, ck_ref, sk_ref,
                   gq_ref, gkv_ref, wn_ref, wr_ref, wrs_ref, uk_ref,
                   qlat_ref, qpe_ref, ckv_ref, kpe_ref, ckvb_ref, kpeb_ref):
    rows = nb * tt
    cq = qa * lax.rsqrt(jnp.mean(qa * qa, axis=-1, keepdims=True) + RMS_EPS) * gq_ref[...]
    cqb = cq.astype(BF16)
    qn = _dot(cqb, wn_ref[...])
    cos_q = jnp.broadcast_to(cq_ref[...][None], (nb, tt, N_HEADS * QK_ROPE)).reshape(rows, -1)
    sin_q = jnp.broadcast_to(sq_ref[...][None], (nb, tt, N_HEADS * QK_ROPE)).reshape(rows, -1)
    qp = (_dot(cqb, wr_ref[...]) * cos_q + _dot(cqb, wrs_ref[...]) * sin_q) * Q_SCALE
    for h in range(N_HEADS):
        qn_h = qn[:, h * LANE:(h + 1) * LANE].astype(BF16)
        ql = _dot(qn_h, uk_ref[h]) * Q_SCALE
        qlat_ref[:, h] = ql.reshape(nb, tt, KV_LORA).astype(qlat_ref.dtype)
        qpe_ref[:, h] = qp[:, h * QK_ROPE:(h + 1) * QK_ROPE].reshape(nb, tt, QK_ROPE).astype(qpe_ref.dtype)
    ckv = kva * lax.rsqrt(jnp.mean(kva * kva, axis=-1, keepdims=True) + RMS_EPS) * gkv_ref[...]
    ckv_ref[...] = ckv
    ckvb_ref[...] = ckv.astype(BF16)
    kpe = kr[:, :, 0:QK_ROPE] * ck_ref[...][None] + kr[:, :, QK_ROPE:2 * QK_ROPE] * sk_ref[...][None]
    kpe_ref[...] = kpe
    kpeb_ref[...] = kpe.astype(BF16)


def _value_up_proj(o_lat, wuv_ref, t):
    tiles = []
    for j in range(N_HEADS // 2):
        h0, h1 = 2 * j, 2 * j + 1
        tiles.append(_dot(o_lat[h0 * t:(h0 + 1) * t].astype(BF16), wuv_ref[h0])
                     + _dot(o_lat[h1 * t:(h1 + 1) * t].astype(BF16), wuv_ref[h1]))
    return jnp.concatenate(tiles, axis=1)


def _softmax_update(s, m_old, l_old):
    m_new = jnp.maximum(m_old, jnp.max(s, axis=-1, keepdims=True))
    alpha = jnp.exp2(m_old - m_new)
    p = jnp.exp2(s - jnp.tile(m_new, (1, s.shape[1] // LANE)))
    l_new = alpha * l_old + jnp.sum(p, axis=-1, keepdims=True)
    return p, m_new, l_new, alpha


def _attn_prompt_kernel(ql_ref, qp_ref, ckv_ref, kpe_ref, wuv_ref, o_ref,
                        m_all, l_all, acc_all, s_all, p_all, *, tq, tk, rc, qt):
    for j in range(qt):
        _attn_prompt_tile(pl.program_id(1) * qt + j,
                          ql_ref.at[0, :, j * tq:(j + 1) * tq, :], qp_ref.at[0, :, j * tq:(j + 1) * tq, :],
                          ckv_ref, kpe_ref, wuv_ref, o_ref.at[0, j * tq:(j + 1) * tq, :],
                          m_all.at[j], l_all.at[j], acc_all.at[j], s_all.at[j], p_all.at[j],
                          tq=tq, tk=tk, rc=rc)


def _attn_prompt_tile(qi, ql_ref, qp_ref, ckv_ref, kpe_ref, wuv_ref, o_ref,
                      m_scr, l_scr, acc_scr, s_scr, p_scr, *, tq, tk, rc):
    rows = N_HEADS * tq
    ql = ql_ref[...].reshape(rows, KV_LORA)
    qp = qp_ref[...].reshape(rows, QK_ROPE)
    m_scr[...] = jnp.full(m_scr.shape, NEG_BIG, F32)
    l_scr[...] = jnp.zeros(l_scr.shape, F32)
    acc_scr[...] = jnp.zeros(acc_scr.shape, F32)
    n_kt = (qi * tq + tq + tk - 1) // tk

    def scores(kt, slot):
        start = pl.multiple_of(kt * tk, tk)
        s_scr[slot] = (_dot_nt(ql, ckv_ref[0, pl.ds(start, tk), :])
                       + _dot_nt(qp, kpe_ref[0, pl.ds(start, tk), :]))

    def softmax(kt, slot, masked):
        for c in range(rows // rc):
            rs = slice(c * rc, (c + 1) * rc)
            s = s_scr[slot, rs, :]
            if masked:
                q_pos = qi * tq + (c * rc) % tq + lax.broadcasted_iota(jnp.int32, (rc, tk), 0)
                k_pos = kt * tk + lax.broadcasted_iota(jnp.int32, (rc, tk), 1)
                s = jnp.where(k_pos <= q_pos, s, NEG_BIG)
            p, m_new, l_new, alpha = _softmax_update(s, m_scr[rs, :], l_scr[rs, :])
            m_scr[rs, :] = m_new
            l_scr[rs, :] = l_new
            p_scr[rs, :] = p.astype(BF16)
            acc_scr[rs, :] = jnp.tile(alpha, (1, KV_LORA // LANE)) * acc_scr[rs, :]

    def weighted_values(kt):
        start = pl.multiple_of(kt * tk, tk)
        acc_scr[...] += _dot(p_scr[...], ckv_ref[0, pl.ds(start, tk), :])

    last = n_kt - 1
    scores(last, 0)
    softmax(last, 0, True)
    scores(0, 1)
    weighted_values(last)

    def body(kt, carry):
        slot = (kt + 1) % 2
        softmax(kt, slot, False)
        scores(kt + 1, 1 - slot)
        weighted_values(kt)
        return carry

    lax.fori_loop(0, last - 1, body, 0)

    @pl.when(last >= 1)
    def _():
        softmax(last - 1, last % 2, False)
        weighted_values(last - 1)
    o_lat = acc_scr[...] / jnp.tile(l_scr[...], (1, KV_LORA // LANE))
    o_ref[...] = _value_up_proj(o_lat, wuv_ref, tq).astype(o_ref.dtype)


def _attn_prompt(qlat, qpe, ckv_bf, kpe_bf, wuv_pad):
    bsz, _, seq, _ = qlat.shape
    tq = 128
    tk = min(256, seq)
    qt = 4 if seq % (4 * tq) == 0 else 1
    assert tk % tq == 0 and seq % tk == 0
    rc = 64
    kern = functools.partial(_attn_prompt_kernel, tq=tq, tk=tk, rc=rc, qt=qt)
    rows = N_HEADS * tq
    return pl.pallas_call(
        kern,
        grid=(bsz, seq // (qt * tq)),
        in_specs=[pl.BlockSpec((1, N_HEADS, qt * tq, KV_LORA), lambda b, q: (b, 0, q, 0)),
                  pl.BlockSpec((1, N_HEADS, qt * tq, QK_ROPE), lambda b, q: (b, 0, q, 0)),
                  pl.BlockSpec((1, seq, KV_LORA), lambda b, q: (b, 0, 0)),
                  pl.BlockSpec((1, seq, QK_ROPE), lambda b, q: (b, 0, 0)),
                  pl.BlockSpec(wuv_pad.shape, lambda b, q: (0, 0, 0))],
        out_specs=pl.BlockSpec((1, qt * tq, D_MLA), lambda b, q: (b, q, 0)),
        out_shape=jax.ShapeDtypeStruct((bsz, seq, D_MLA), BF16),
        scratch_shapes=[pltpu.VMEM((qt, rows, LANE), F32), pltpu.VMEM((qt, rows, LANE), F32),
                        pltpu.VMEM((qt, rows, KV_LORA), F32),
                        pltpu.VMEM((qt, 2, rows, tk), F32), pltpu.VMEM((qt, rows, tk), BF16)],
        compiler_params=_cparams(("parallel", "arbitrary")),
        name="attn_prompt",
    )(qlat, qpe, ckv_bf, kpe_bf, wuv_pad)


def _attn_sample_kernel(pt_ref, ql_ref, qp_ref, ckvn_ref, kpen_ref, wuv_ref, cache_ckv, cache_kpe_t,
                        o_ref, ckv_buf, kpe_buf, sem, *, layer, pp, cp, n_steps, ts):
    ahead = PAGE_SLOTS - 1
    b = pl.program_id(0)
    nb = pl.num_programs(0)
    rows = N_HEADS * ts
    total = nb * n_steps
    ql = ql_ref[0].reshape(rows, KV_LORA).astype(BF16)
    qp = qp_ref[0].reshape(rows, QK_ROPE).astype(BF16)

    def page_copies(seq, grp, slot):
        out = []
        for i in range(pp):
            page = pt_ref[seq, grp * pp + i]
            out.append(pltpu.make_async_copy(cache_ckv.at[layer, page], ckv_buf.at[slot, i], sem.at[slot]))
            out.append(pltpu.make_async_copy(cache_kpe_t.at[layer, page], kpe_buf.at[slot, i], sem.at[slot]))
        return out

    def start_group(g):
        slot = lax.rem(g, PAGE_SLOTS)
        g = jnp.minimum(g, total - 1)
        seq = lax.shift_right_logical(g, n_steps.bit_length() - 1)
        for c in page_copies(seq, jnp.bitwise_and(g, n_steps - 1), slot):
            c.start()

    def wait_group(g):
        for c in page_copies(0, 0, lax.rem(g, PAGE_SLOTS)):
            c.wait()

    @pl.when(b == 0)
    def _():
        for g in range(ahead):
            start_group(jnp.int32(g))

    def partial_softmax(s, vals):
        m = jnp.max(s, axis=-1, keepdims=True)
        p = jnp.exp2(s - m)
        return m, jnp.sum(p, axis=-1, keepdims=True), _dot(p.astype(BF16), vals)

    def merge(parts):
        m = functools.reduce(jnp.maximum, [pm for pm, _, _ in parts])
        l = sum(pl_ * jnp.exp2(pm - m) for pm, pl_, _ in parts)
        acc = sum(pa * jnp.exp2(pm - m) for pm, _, pa in parts)
        return m, l, acc

    def group_parts(slot):
        n_c = pp // cp
        cks = [ckv_buf[slot, c * cp:(c + 1) * cp].reshape(cp * PAGE, KV_LORA).astype(BF16) for c in range(n_c)]
        kps = [jnp.concatenate([kpe_buf[slot, c * cp + i].astype(BF16) for i in range(cp)], axis=1)
               for c in range(n_c)]
        ss = [_dot_nt(ql, ck) + _dot(qp, kp_t) for ck, kp_t in zip(cks, kps)]
        ms = [jnp.max(s, axis=-1, keepdims=True) for s in ss]
        ps = [jnp.exp2(s - m) for s, m in zip(ss, ms)]
        ls = [jnp.sum(p, axis=-1, keepdims=True) for p in ps]
        accs = [_dot(p.astype(BF16), ck) for p, ck in zip(ps, cks)]
        return list(zip(ms, ls, accs))

    ckn = ckvn_ref[0]
    kpn = kpen_ref[0]
    s_new = _dot_nt(ql, ckn) + _dot_nt(qp, kpn)
    t_q = lax.broadcasted_iota(jnp.int32, (rows, ts), 0) % ts
    t_k = lax.broadcasted_iota(jnp.int32, (rows, ts), 1)
    state = partial_softmax(jnp.where(t_k <= t_q, s_new, NEG_BIG), ckn)

    for st in range(n_steps):
        g = b * n_steps + st
        start_group(g + ahead)
        wait_group(g)
        state = merge([state] + group_parts(lax.rem(g, PAGE_SLOTS)))

    @pl.when(b == nb - 1)
    def _():
        for g in range(ahead):
            wait_group(total + g)

    _, l, acc = state
    o_ref[0] = _value_up_proj(acc / l, wuv_ref, ts).astype(o_ref.dtype)


def _attn_sample(page_table, qlat, qpe, ckv_new_bf, kpe_new_bf, wuv_pad, cache_ckv, cache_kpe, layer):
    bsz, _, ts, _ = qlat.shape
    n_pages = page_table.shape[1]
    pp = min(32, n_pages // 2)
    cp = min(8, pp)
    n_steps = n_pages // pp
    assert n_pages % pp == 0 and pp % cp == 0
    assert n_steps & (n_steps - 1) == 0
    kern = functools.partial(_attn_sample_kernel, layer=layer, pp=pp, cp=cp, n_steps=n_steps, ts=ts)

    cache_kpe_t = jnp.swapaxes(cache_kpe, 2, 3)

    grid_spec = pltpu.PrefetchScalarGridSpec(
        num_scalar_prefetch=1,
        grid=(bsz,),
        in_specs=[pl.BlockSpec((1, N_HEADS, ts, KV_LORA), lambda b, pt: (b, 0, 0, 0)),
                  pl.BlockSpec((1, N_HEADS, ts, QK_ROPE), lambda b, pt: (b, 0, 0, 0)),
                  pl.BlockSpec((1, ts, KV_LORA), lambda b, pt: (b, 0, 0)),
                  pl.BlockSpec((1, ts, QK_ROPE), lambda b, pt: (b, 0, 0)),
                  pl.BlockSpec(wuv_pad.shape, lambda b, pt: (0, 0, 0)),
                  pl.BlockSpec(memory_space=pl.ANY),
                  pl.BlockSpec(memory_space=pl.ANY)],
        out_specs=pl.BlockSpec((1, ts, D_MLA), lambda b, pt: (b, 0, 0)),
        scratch_shapes=[pltpu.VMEM((PAGE_SLOTS, pp, PAGE, KV_LORA), F32),
                        pltpu.VMEM((PAGE_SLOTS, pp, QK_ROPE, PAGE), F32),
                        pltpu.SemaphoreType.DMA((PAGE_SLOTS,))],
    )
    return pl.pallas_call(
        kern,
        grid_spec=grid_spec,
        out_shape=jax.ShapeDtypeStruct((bsz, ts, D_MLA), BF16),
        compiler_params=_cparams(("arbitrary",)),
        name="attn_sample",
    )(page_table, qlat, qpe, ckv_new_bf, kpe_new_bf, wuv_pad, cache_ckv, cache_kpe_t)


def _rwkv_prep_body(rw, mu_ref, w0_ref, w2_ref, a0_ref, a2_ref, kk_ref, ka_ref, rk_ref, bd_ref, sprev_ref,
                    q_out, w_out, k_out, v_out, kk_out, b_out, vkr_out, bonus_out, shift_out, buf):
    nb, tt, _ = rw.shape
    ti = pl.program_id(1)
    rows = nb * tt

    @pl.when(ti == 0)
    def _():
        buf[:, SUBLANE - 1:SUBLANE, :] = sprev_ref[...]

    @pl.when(ti > 0)
    def _():
        buf[:, SUBLANE - 1:SUBLANE, :] = buf[:, tt + SUBLANE - 1:tt + SUBLANE, :]

    buf[:, SUBLANE:, :] = rw
    shift_out[...] = rw[:, tt - 1:tt, :]
    rw_prev = buf[:, SUBLANE - 1:SUBLANE - 1 + tt, :]
    rws = (rw + mu_ref[...] * (rw_prev - rw)).reshape(rows, SHIFT_W)
    r = rws[:, 0:D_R]
    k = rws[:, D_R:2 * D_R]
    v = rws[:, 2 * D_R:3 * D_R]
    wa = rws[:, 3 * D_R:]
    w_log = -_softplus(-(w0_ref[...] + _dot(jnp.tanh(wa).astype(BF16), w2_ref[...]))) - 0.5
    decay = jnp.exp(-jnp.exp(w_log))
    a = _sigmoid(a0_ref[...] + _dot(wa.astype(BF16), a2_ref[...]))
    bd = bd_ref[...]
    kk = k * kk_ref[...]
    kk = kk / jnp.maximum(jnp.sqrt(_seg_sum(kk * kk, bd)), 1e-12)
    k = k * (1.0 + (a - 1.0) * ka_ref[...])
    bonus = _seg_sum(r * k * rk_ref[...], bd) * v
    b = kk * a
    q = decay * r - kk * _seg_sum(b * r, bd)
    vkr = v * _seg_sum(k * r, bd)
    shp = (nb, tt, D_R)
    q_out[...] = q.reshape(shp)
    w_out[...] = decay.reshape(shp)
    k_out[...] = k.reshape(shp)
    v_out[...] = v.reshape(shp)
    kk_out[...] = kk.reshape(shp)
    b_out[...] = b.reshape(shp)
    vkr_out[...] = vkr.reshape(shp)
    bonus_out[...] = bonus.reshape(shp)


def _rwkv_scan_kernel(q_ref, w_ref, k_ref, v_ref, kk_ref, b_ref, vkr_ref, s0_ref, bd_ref, eye_ref,
                      y_ref, sT_ref, s_scr):
    nb, tc, _ = q_ref.shape
    ci = pl.program_id(1)
    rows = nb * R_HEAD

    @pl.when(ci == 0)
    def _():
        s_scr[...] = s0_ref[...]

    bd = bd_ref[...]
    eye = eye_ref[...][None]
    eye_bf = eye.astype(BF16)

    gb = max(nb // SCAN_GROUPS, 1)

    def seg(x):
        return _dot(x.reshape(gb * R_HEAD, D_R), bd).reshape(gb, R_HEAD, D_R)

    def step(t, carry):
        for g0 in range(0, nb, gb):
            grp = slice(g0, g0 + gb)
            row = lambda ref: ref[grp, pl.ds(t, 1), :]
            s = s_scr[grp]
            s_bf = s.astype(BF16)
            sa = seg(s_bf * row(kk_ref).astype(BF16))
            y_col = seg(s_bf * row(q_ref).astype(BF16))
            v_col = seg(eye_bf * row(v_ref).astype(BF16))
            s_scr[grp] = s * row(w_ref) - sa * row(b_ref) + v_col * row(k_ref)
            y_ref[grp, pl.ds(t, 1), :] = jnp.sum(y_col * eye, axis=1, keepdims=True) + row(vkr_ref)
        return carry

    lax.fori_loop(0, tc, step, 0, unroll=SCAN_UNROLL)

    @pl.when(ci == pl.num_programs(1) - 1)
    def _():
        sT_ref[...] = s_scr[...]


def _rwkv_scan(q, w, k, v, kk, b, vkr, s0, bd, eye):
    bsz, seq, _ = q.shape
    nb = math.gcd(bsz, 8)
    tc = min(seq, 256)
    row = pl.BlockSpec((nb, tc, D_R), lambda bi, c: (bi, c, 0))
    st = pl.BlockSpec((nb, R_HEAD, D_R), lambda bi, c: (bi, 0, 0))
    return pl.pallas_call(
        _rwkv_scan_kernel,
        grid=(bsz // nb, seq // tc),
        in_specs=[row] * 7 + [st, pl.BlockSpec(bd.shape, lambda bi, c: (0, 0)),
                              pl.BlockSpec(eye.shape, lambda bi, c: (0, 0))],
        out_specs=[row, st],
        out_shape=[jax.ShapeDtypeStruct((bsz, seq, D_R), F32),
                   jax.ShapeDtypeStruct((bsz, R_HEAD, D_R), F32)],
        scratch_shapes=[pltpu.VMEM((nb, R_HEAD, D_R), F32)],
        compiler_params=_cparams(("parallel", "arbitrary")),
        name="rwkv_scan",
    )(q, w, k, v, kk, b, vkr, s0, bd, eye)


def _out_kernel(x_ref, mod_ref, omla_ref, szmla_ref, cbz_ref, ccx_ref, cprev_ref, szrw_ref,
                yr_ref, bonus_ref, gm_ref, wmla_ref, wconv_ref, wrw_ref, wout_ref, cw_ref,
                gng_ref, gnb_ref, bd_ref, fg_ref, xo_ref, cstate_ref, buf, *, final):
    nb, tt, d = x_ref.shape
    ti = pl.program_id(1)
    rows = nb * tt

    @pl.when(ti == 0)
    def _():
        buf[:, SUBLANE - 2:SUBLANE, :] = cprev_ref[...]

    @pl.when(ti > 0)
    def _():
        buf[:, SUBLANE - 2:SUBLANE, :] = buf[:, tt + SUBLANE - 2:tt + SUBLANE, :]

    buf[:, SUBLANE:, :] = ccx_ref[...]
    cstate_ref[...] = buf[:, tt + SUBLANE - 2:tt + SUBLANE, :]
    conv = (buf[:, SUBLANE - 2:SUBLANE - 2 + tt, :] * cw_ref[0:1, :]
            + buf[:, SUBLANE - 1:SUBLANE - 1 + tt, :] * cw_ref[1:2, :]
            + buf[:, SUBLANE:, :] * cw_ref[2:3, :])
    y_conv = _dot((cbz_ref[...] * conv).reshape(rows, D_CONV).astype(BF16), wconv_ref[...])

    y_mla = _dot((omla_ref[...].astype(F32) * szmla_ref[...].astype(F32)).reshape(rows, D_MLA).astype(BF16),
                 wmla_ref[...])

    bd = bd_ref[...]
    yr = yr_ref[...].reshape(rows, D_R)
    mu = _seg_sum(yr, bd) * (1.0 / R_HEAD)
    dy = yr - mu
    var = _seg_sum(dy * dy, bd) * (1.0 / R_HEAD)
    yn = dy * lax.rsqrt(var + GN_EPS) * gng_ref[...] + gnb_ref[...]
    o_rw = yn + bonus_ref[...].reshape(rows, D_R)
    y_rw = _dot((o_rw * szrw_ref[...].astype(F32).reshape(rows, D_R)).astype(BF16), wrw_ref[...])

    g = gm_ref[...].astype(F32).reshape(rows, N_BRANCH * d)
    merged = g[:, 0:d] * y_mla + g[:, d:2 * d] * y_conv + g[:, 2 * d:] * y_rw
    delta = _dot(merged.astype(BF16), wout_ref[...]).reshape(nb, tt, d)
    xo = x_ref[...] + mod_ref[:, :, 2 * d:] * delta
    if final:
        xo = xo * lax.rsqrt(jnp.mean(xo * xo, axis=-1, keepdims=True) + RMS_EPS) * fg_ref[...]
    xo_ref[...] = xo


def _out_proj(x, mod, o_mla, sz_mla, cbz, ccx, conv_prev, sz_rw, y_r, bonus, gm,
              wmla, wconv, wrw, wout, conv_w, gn_g, gn_b, bd, final_g, final):
    bsz, seq, d = x.shape
    nb, tt = _row_tiles(bsz, seq, 256)
    grid = (bsz // nb, seq // tt)
    row = lambda w: pl.BlockSpec((nb, tt, w), lambda b, t: (b, t, 0))
    full = lambda a: pl.BlockSpec(a.shape, lambda b, t: (0,) * a.ndim)
    consts = [wmla, wconv, wrw, wout, conv_w, gn_g.reshape(1, -1), gn_b.reshape(1, -1), bd,
              final_g.reshape(1, -1)]
    return pl.pallas_call(
        functools.partial(_out_kernel, final=final),
        grid=grid,
        in_specs=[row(d), pl.BlockSpec((nb, 1, 3 * d), lambda b, t: (b, 0, 0)),
                  row(D_MLA), row(D_MLA), row(D_CONV), row(D_CONV),
                  pl.BlockSpec((nb, CONV_W - 1, D_CONV), lambda b, t: (b, 0, 0)),
                  row(D_R), row(D_R), row(D_R), row(N_BRANCH * d)] + [full(a) for a in consts],
        out_specs=[row(d), pl.BlockSpec((nb, CONV_W - 1, D_CONV), lambda b, t: (b, 0, 0))],
        out_shape=[jax.ShapeDtypeStruct((bsz, seq, d), F32),
                   jax.ShapeDtypeStruct((bsz, CONV_W - 1, D_CONV), F32)],
        scratch_shapes=[pltpu.VMEM((nb, tt + SUBLANE, D_CONV), F32)],
        compiler_params=_cparams(("parallel", "arbitrary")),
        name="out_proj",
    )(x, mod, o_mla, sz_mla, cbz, ccx, conv_prev, sz_rw, y_r, bonus, gm, *consts)


def _swap_halves(w):
    half = QK_ROPE // 2
    return jnp.concatenate([w[..., half:], w[..., :half]], axis=-1)


def _pad_w_in(w_in):
    a = Q_LORA + KV_LORA
    k_rope = w_in[:, :, a:a + QK_ROPE]
    tail = w_in.shape[2] - a - QK_ROPE

    def place(piece, at):
        return jnp.pad(piece, ((0, 0), (0, 0), (at, PROJ_P - at - piece.shape[2])))

    return (place(w_in[:, :, :a], 0) + place(k_rope, a) + place(_swap_halves(k_rope), a + QK_ROPE)
            + place(w_in[:, :, a + QK_ROPE:], PROJ_P - tail)).astype(BF16)


def _layer_params(l, w_ada, w_in_p, w_q_b, w_uk, w_uv, w_mla_out, w_conv_out, rwkv_w2, rwkv_a2,
                  w_rwkv_out, w_out):
    w_in_p = w_in_p[l]
    wq = w_q_b[l].reshape(Q_LORA, N_HEADS, QK_NOPE + QK_ROPE)
    wn = jnp.pad(wq[:, :, :QK_NOPE], ((0, 0), (0, 0), (0, LANE - QK_NOPE))).reshape(Q_LORA, N_HEADS * LANE)
    wr = wq[:, :, QK_NOPE:]
    wrs = _swap_halves(wr)
    uk = jnp.pad(jnp.transpose(w_uk[l], (1, 2, 0)), ((0, 0), (0, LANE - QK_NOPE), (0, 0)))
    wuv = jnp.transpose(w_uv[l], (1, 0, 2))
    zv = jnp.zeros_like(wuv)
    even = (jnp.arange(N_HEADS) % 2 == 0)[:, None, None]
    wuv_pad = jnp.where(even, jnp.concatenate([wuv, zv], axis=2), jnp.concatenate([zv, wuv], axis=2))
    zeros = jnp.zeros((W_LORA, D_R), F32)
    return dict(
        w_ada=w_ada[l].astype(BF16), w_in_p=w_in_p,
        wn=wn.astype(BF16), wr=wr.reshape(Q_LORA, -1).astype(BF16), wrs=wrs.reshape(Q_LORA, -1).astype(BF16),
        uk=uk.astype(BF16), wuv_pad=wuv_pad.astype(BF16),
        wmla=w_mla_out[l].astype(BF16), wconv=w_conv_out[l].astype(BF16),
        wrw=w_rwkv_out[l].astype(BF16), wout=w_out[l].astype(BF16),
        w2p=jnp.concatenate([rwkv_w2[l], zeros], axis=0).astype(BF16),
        a2p=jnp.concatenate([zeros, rwkv_a2[l]], axis=0).astype(BF16),
    )


def _rope_tables(pos):
    half = QK_ROPE // 2
    inv = ROPE_THETA ** (-jnp.arange(half, dtype=F32) / half)
    ang = pos.astype(F32)[:, None] * inv[None, :]
    cos, sin = jnp.cos(ang), jnp.sin(ang)
    cos_k = jnp.concatenate([cos, cos], axis=1)
    sin_k = jnp.concatenate([-sin, sin], axis=1)
    return jnp.tile(cos_k, (1, N_HEADS)), jnp.tile(sin_k, (1, N_HEADS)), cos_k, sin_k


def _state_to_lanes(s):
    b = s.shape[0]
    return jnp.transpose(s, (0, 2, 1, 3)).reshape(b, R_HEAD, D_R)


def _state_from_lanes(s):
    b = s.shape[0]
    return jnp.transpose(s.reshape(b, R_HEAD, R_HEADS, R_HEAD), (0, 2, 1, 3))


def _layer(x, c_mod, tabs, p, vecs, conv_prev, shift_prev, s0, consts, final_g, final, attend, q_dtype):
    bd, eye = consts
    (sz_mla, cbz, ccx, sz_rw, gm, qlat, qpe, ckv, kpe, ckv_bf, kpe_bf,
     q, w, k, v, kk, b, vkr, bonus, shift_state) = _front(
        x, c_mod, vecs["norm_g"], p["w_in_p"], tabs, vecs["q_norm_g"], vecs["kv_norm_g"],
        p["wn"], p["wr"], p["wrs"], p["uk"], q_dtype,
        shift_prev, vecs["mu"], vecs["w0"], p["w2p"], vecs["a0"], p["a2p"],
        vecs["k_k"], vecs["k_a"], vecs["r_k"], bd)
    o_mla = attend(qlat, qpe, ckv_bf, kpe_bf, p["wuv_pad"])
    y_r, s_new = _rwkv_scan(q, w, k, v, kk, b, vkr, s0, bd, eye)
    x_new, conv_state = _out_proj(
        x, c_mod, o_mla, sz_mla, cbz, ccx, conv_prev, sz_rw, y_r, bonus, gm,
        p["wmla"], p["wconv"], p["wrw"], p["wout"], vecs["conv_w"], vecs["gn_g"], vecs["gn_b"], bd,
        final_g, final)
    return x_new, ckv, kpe, conv_state, shift_state[:, 0], s_new


def kernel(x_prompt, x_sample, cache_mla_ckv, cache_mla_kpe, state_conv, state_rwkv_shift, state_rwkv, page_table, c_prompt, c_sample, norm_g, w_ada, b_ada, w_in, q_norm_g, w_q_b, kv_norm_g, w_uk, w_uv, w_mla_out, conv_w, w_conv_out, rwkv_mu, rwkv_w0, rwkv_w2, rwkv_a0, rwkv_a2, rwkv_k_k, rwkv_k_a, rwkv_r_k, rwkv_gn_g, rwkv_gn_b, w_rwkv_out, w_out, final_norm_g):
    depth = norm_g.shape[0]
    bp, tp, _ = x_prompt.shape
    bs, ts, _ = x_sample.shape
    past = page_table.shape[1] * PAGE
    tabs_p = _rope_tables(jnp.arange(tp, dtype=jnp.int32))
    tabs_s = _rope_tables(past + jnp.arange(ts, dtype=jnp.int32))
    seg = jnp.arange(D_R, dtype=jnp.int32) // R_HEAD
    bd = (seg[:, None] == seg[None, :]).astype(BF16)
    eye = (jnp.arange(R_HEAD, dtype=jnp.int32)[:, None] == (jnp.arange(D_R, dtype=jnp.int32) % R_HEAD)[None, :]).astype(F32)
    consts = (bd, eye)
    w_in = _pad_w_in(w_in)
    xp, xs = x_prompt, x_sample
    st_p = ([], [], [], [], [])
    st_s = ([], [], [], [], [])
    for l in range(depth):
        final = l == depth - 1
        p = _layer_params(l, w_ada, w_in, w_q_b, w_uk, w_uv, w_mla_out, w_conv_out, rwkv_w2, rwkv_a2,
                          w_rwkv_out, w_out)
        vecs = dict(norm_g=norm_g[l], q_norm_g=q_norm_g[l], kv_norm_g=kv_norm_g[l], mu=rwkv_mu[l],
                    w0=rwkv_w0[l], a0=rwkv_a0[l], k_k=rwkv_k_k[l], k_a=rwkv_k_a[l],
                    r_k=rwkv_r_k[l].reshape(-1), conv_w=conv_w[l], gn_g=rwkv_gn_g[l], gn_b=rwkv_gn_b[l])
        mod_p = _ada_mod(c_prompt, p["w_ada"], b_ada[l])
        mod_s = _ada_mod(c_sample, p["w_ada"], b_ada[l])
        xp, ckv, kpe, cst, sst, rst = _layer(
            xp, mod_p, tabs_p, p, vecs,
            jnp.zeros((bp, CONV_W - 1, D_CONV), F32), jnp.zeros((bp, SHIFT_W), F32),
            jnp.zeros((bp, R_HEAD, D_R), F32), consts, final_norm_g, final, _attn_prompt, BF16)
        for lst, val in zip(st_p, (ckv, kpe, cst, sst, _state_from_lanes(rst))):
            lst.append(val)
        attend_s = functools.partial(_attn_sample_bound, page_table, cache_mla_ckv, cache_mla_kpe, l)
        xs, ckv, kpe, cst, sst, rst = _layer(
            xs, mod_s, tabs_s, p, vecs, state_conv[l], state_rwkv_shift[l],
            _state_to_lanes(state_rwkv[l]), consts, final_norm_g, final, attend_s, F32)
        for lst, val in zip(st_s, (ckv, kpe, cst, sst, _state_from_lanes(rst))):
            lst.append(val)
    return (xp, xs,
            jnp.stack(st_p[0]), jnp.stack(st_p[1]), jnp.stack(st_p[2]), jnp.stack(st_p[3]), jnp.stack(st_p[4]),
            jnp.stack(st_s[0]), jnp.stack(st_s[1]), jnp.stack(st_s[2]), jnp.stack(st_s[3]), jnp.stack(st_s[4]))


def _attn_sample_bound(page_table, cache_ckv, cache_kpe, layer, qlat, qpe, ckv_bf, kpe_bf, wuv_pad):
    return _attn_sample(page_table, qlat, qpe, ckv_bf, kpe_bf, wuv_pad, cache_ckv, cache_kpe, layer)
```

```python
import functools
import math

import jax
import jax.numpy as jnp
from jax import lax
from jax.experimental import pallas as pl
from jax.experimental.pallas import tpu as pltpu

F32 = jnp.float32
BF16 = jnp.bfloat16

D_MODEL = 1024
PAGE = 128
N_HEADS = 8
QK_NOPE = 64
QK_ROPE = 32
V_HEAD = 64
Q_LORA = 384
KV_LORA = 256
D_MLA = N_HEADS * V_HEAD
ROPE_THETA = 10000.0
ATTN_SCALE = (QK_NOPE + QK_ROPE) ** -0.5
Q_SCALE = ATTN_SCALE * math.log2(math.e)
D_CONV = 256
CONV_W = 3
R_HEADS = 4
R_HEAD = 64
D_R = R_HEADS * R_HEAD
W_LORA = 64
A_LORA = 64
SHIFT_W = 3 * D_R + W_LORA + A_LORA
GN_EPS = 64e-5
RMS_EPS = 1e-6
N_BRANCH = 3

LANE = 128
SUBLANE = 8
KR_W = LANE
SEG_W = (Q_LORA, KV_LORA, KR_W, D_MLA, 4 * D_CONV, SHIFT_W, D_R, N_BRANCH * D_MODEL)
SEG_OFF = tuple(sum(SEG_W[:i]) for i in range(len(SEG_W)))
PROJ_P = sum(SEG_W)
VMEM_LIMIT = 56 * 1024 * 1024
NEG_BIG = -1e30
SCAN_UNROLL = 16
SCAN_GROUPS = 2
PAGE_SLOTS = 4


def _cparams(sem):
    return pltpu.CompilerParams(dimension_semantics=sem, vmem_limit_bytes=VMEM_LIMIT)


def _dot(a, b):
    return jnp.dot(a, b, preferred_element_type=F32)


def _dot_nt(a, b):
    return lax.dot_general(a, b, (((1,), (1,)), ((), ())), preferred_element_type=F32)


def _sigmoid(x):
    return 1.0 / (1.0 + jnp.exp(-x))


def _silu(x):
    return x * _sigmoid(x)


def _softplus(x):
    return jnp.maximum(x, 0.0) + jnp.log1p(jnp.exp(-jnp.abs(x)))


def _seg_sum(x, bd, split=True):
    hi = x.astype(BF16)
    if not split:
        return _dot(hi, bd)
    lo = (x - hi.astype(F32)).astype(BF16)
    return _dot(hi, bd) + _dot(lo, bd)


def _row_tiles(batch, seq, cap):
    if seq >= LANE:
        tt = min(seq, cap)
        assert seq % tt == 0
        return 1, tt
    assert seq % SUBLANE == 0
    return batch, seq


def _ada_kernel(c_ref, w_ref, b_ref, o_ref):
    s = _silu(c_ref[...])
    o_ref[...] = _dot(s.astype(BF16), w_ref[...]) + b_ref[...]


def _ada_mod(c, w_ada_bf, b_ada):
    bsz = c.shape[0]
    out = pl.pallas_call(
        _ada_kernel,
        out_shape=jax.ShapeDtypeStruct((bsz, 3 * D_MODEL), F32),
        compiler_params=pltpu.CompilerParams(vmem_limit_bytes=VMEM_LIMIT),
        name="ada_mod",
    )(c, w_ada_bf, b_ada.reshape(1, -1))
    return out.reshape(bsz, 1, 3 * D_MODEL)


N_MLA_IN, N_RWKV_IN = 10, 10
N_GATE_OUT = 5


def _front_kernel(x_ref, mod_ref, g_ref, w_ref, *refs):
    nb, tt, d = x_ref.shape
    mla_in = refs[:N_MLA_IN]
    rwkv_in = refs[N_MLA_IN:N_MLA_IN + N_RWKV_IN]
    outs = refs[N_MLA_IN + N_RWKV_IN:]
    zmla_out, cbz_out, ccx_out, zrw_out, gm_out = outs[:N_GATE_OUT]
    mla_out = outs[N_GATE_OUT:N_GATE_OUT + 6]
    rwkv_out = outs[N_GATE_OUT + 6:N_GATE_OUT + 15]
    buf = outs[N_GATE_OUT + 15]
    x = x_ref[...]
    ms = jnp.mean(x * x, axis=-1, keepdims=True)
    xn = x * lax.rsqrt(ms + RMS_EPS) * g_ref[...]
    shift = mod_ref[:, :, 0:D_MODEL]
    scale = mod_ref[:, :, D_MODEL:2 * D_MODEL]
    u = (xn * (1.0 + scale) + shift).reshape(nb * tt, d).astype(BF16)

    def seg(i):
        return _dot(u, w_ref[:, SEG_OFF[i]:SEG_OFF[i] + SEG_W[i]])

    gm_out[...] = _sigmoid(seg(7)).reshape(nb, tt, SEG_W[7]).astype(gm_out.dtype)
    _rwkv_prep_body(seg(5).reshape(nb, tt, SHIFT_W), *rwkv_in, *rwkv_out, buf)
    zmla_out[...] = _silu(seg(3)).reshape(nb, tt, SEG_W[3]).astype(zmla_out.dtype)
    c4 = seg(4)
    cbz_out[...] = (c4[:, 0:D_CONV] * _silu(c4[:, 3 * D_CONV:])).reshape(nb, tt, D_CONV)
    ccx_out[...] = (c4[:, D_CONV:2 * D_CONV] * c4[:, 2 * D_CONV:3 * D_CONV]).reshape(nb, tt, D_CONV)
    zrw_out[...] = _silu(seg(6)).reshape(nb, tt, SEG_W[6]).astype(zrw_out.dtype)
    _mla_prep_body(seg(0), seg(1).reshape(nb, tt, KV_LORA), seg(2).reshape(nb, tt, KR_W), nb, tt,
                   *mla_in, *mla_out)


def _front(x, mod, norm_g, w_in_p, tabs, q_norm_g, kv_norm_g, wn, wr, wrs, uk, q_dtype,
           shift_prev, mu, w0, w2p, a0, a2p, k_k, k_a, r_k, bd):
    bsz, seq, d = x.shape
    nb, tt = _row_tiles(bsz, seq, 256)
    grid = (bsz // nb, seq // tt)
    cos_q, sin_q, cos_k, sin_k = tabs
    row = lambda w: pl.BlockSpec((nb, tt, w), lambda b, t: (b, t, 0))
    tab = lambda w: pl.BlockSpec((tt, w), lambda b, t: (t, 0))
    full = lambda a: pl.BlockSpec(a.shape, lambda b, t: (0,) * a.ndim)
    hq = lambda w: pl.BlockSpec((nb, N_HEADS, tt, w), lambda b, t: (b, 0, t, 0))
    state = pl.BlockSpec((nb, 1, SHIFT_W), lambda b, t: (b, 0, 0))
    mla_consts = [q_norm_g.reshape(1, -1), kv_norm_g.reshape(1, -1), wn, wr, wrs, uk]
    rwkv_consts = [mu.reshape(1, -1), w0.reshape(1, -1), w2p, a0.reshape(1, -1), a2p,
                   k_k.reshape(1, -1), k_a.reshape(1, -1), r_k.reshape(1, -1), bd]
    assert 4 + len(mla_consts) == N_MLA_IN and len(rwkv_consts) + 1 == N_RWKV_IN
    g = norm_g.reshape(1, d)
    outs = pl.pallas_call(
        _front_kernel,
        grid=grid,
        in_specs=[row(d), pl.BlockSpec((nb, 1, 3 * d), lambda b, t: (b, 0, 0)), full(g), full(w_in_p),
                  tab(N_HEADS * QK_ROPE), tab(N_HEADS * QK_ROPE), tab(QK_ROPE), tab(QK_ROPE)]
                 + [full(a) for a in mla_consts] + [full(a) for a in rwkv_consts[:-1]] + [full(bd), state],
        out_specs=[row(SEG_W[3]), row(D_CONV), row(D_CONV), row(SEG_W[6]), row(SEG_W[7]),
                   hq(KV_LORA), hq(QK_ROPE), row(KV_LORA), row(QK_ROPE), row(KV_LORA), row(QK_ROPE)]
                  + [row(D_R)] * 8 + [state],
        out_shape=[jax.ShapeDtypeStruct((bsz, seq, SEG_W[3]), BF16),
                   jax.ShapeDtypeStruct((bsz, seq, D_CONV), F32),
                   jax.ShapeDtypeStruct((bsz, seq, D_CONV), F32),
                   jax.ShapeDtypeStruct((bsz, seq, SEG_W[6]), BF16),
                   jax.ShapeDtypeStruct((bsz, seq, SEG_W[7]), BF16)]
                  + [jax.ShapeDtypeStruct((bsz, N_HEADS, seq, KV_LORA), q_dtype),
                     jax.ShapeDtypeStruct((bsz, N_HEADS, seq, QK_ROPE), q_dtype),
                     jax.ShapeDtypeStruct((bsz, seq, KV_LORA), F32),
                     jax.ShapeDtypeStruct((bsz, seq, QK_ROPE), F32),
                     jax.ShapeDtypeStruct((bsz, seq, KV_LORA), BF16),
                     jax.ShapeDtypeStruct((bsz, seq, QK_ROPE), BF16)]
                  + [jax.ShapeDtypeStruct((bsz, seq, D_R), F32)] * 8
                  + [jax.ShapeDtypeStruct((bsz, 1, SHIFT_W), F32)],
        scratch_shapes=[pltpu.VMEM((nb, tt + SUBLANE, SHIFT_W), F32)],
        compiler_params=_cparams(("parallel", "arbitrary")),
        name="front",
    )(x, mod, g, w_in_p, cos_q, sin_q, cos_k, sin_k, *mla_consts, *rwkv_consts,
      shift_prev.reshape(bsz, 1, SHIFT_W))
    return outs


def _mla_prep_body(qa, kva, kr, nb, tt, cq_ref, sq_ref, ck_ref, sk_ref,
                   gq_ref, gkv_ref, wn_ref, wr_ref, wrs_ref, uk_ref,
                   qlat_ref, qpe_ref, ckv_ref, kpe_ref, ckvb_ref, kpeb_ref):
    rows = nb * tt
    cq = qa * lax.rsqrt(jnp.mean(qa * qa, axis=-1, keepdims=True) + RMS_EPS) * gq_ref[...]
    cqb = cq.astype(BF16)
    qn = _dot(cqb, wn_ref[...])
    cos_q = jnp.broadcast_to(cq_ref[...][None], (nb, tt, N_HEADS * QK_ROPE)).reshape(rows, -1)
    sin_q = jnp.broadcast_to(sq_ref[...][None], (nb, tt, N_HEADS * QK_ROPE)).reshape(rows, -1)
    qp = (_dot(cqb, wr_ref[...]) * cos_q + _dot(cqb, wrs_ref[...]) * sin_q) * Q_SCALE
    for h in range(N_HEADS):
        qn_h = qn[:, h * LANE:(h + 1) * LANE].astype(BF16)
        ql = _dot(qn_h, uk_ref[h]) * Q_SCALE
        qlat_ref[:, h] = ql.reshape(nb, tt, KV_LORA).astype(qlat_ref.dtype)
        qpe_ref[:, h] = qp[:, h * QK_ROPE:(h + 1) * QK_ROPE].reshape(nb, tt, QK_ROPE).astype(qpe_ref.dtype)
    ckv = kva * lax.rsqrt(jnp.mean(kva * kva, axis=-1, keepdims=True) + RMS_EPS) * gkv_ref[...]
    ckv_ref[...] = ckv
    ckvb_ref[...] = ckv.astype(BF16)
    kpe = kr[:, :, 0:QK_ROPE] * ck_ref[...][None] + kr[:, :, QK_ROPE:2 * QK_ROPE] * sk_ref[...][None]
    kpe_ref[...] = kpe
    kpeb_ref[...] = kpe.astype(BF16)


def _value_up_proj(o_lat, wuv_ref, t):
    tiles = []
    for j in range(N_HEADS // 2):
        h0, h1 = 2 * j, 2 * j + 1
        tiles.append(_dot(o_lat[h0 * t:(h0 + 1) * t].astype(BF16), wuv_ref[h0])
                     + _dot(o_lat[h1 * t:(h1 + 1) * t].astype(BF16), wuv_ref[h1]))
    return jnp.concatenate(tiles, axis=1)


def _softmax_update(s, m_old, l_old):
    m_new = jnp.maximum(m_old, jnp.max(s, axis=-1, keepdims=True))
    alpha = jnp.exp2(m_old - m_new)
    p = jnp.exp2(s - jnp.tile(m_new, (1, s.shape[1] // LANE)))
    l_new = alpha * l_old + jnp.sum(p, axis=-1, keepdims=True)
    return p, m_new, l_new, alpha


def _attn_prompt_kernel(ql_ref, qp_ref, ckv_ref, kpe_ref, wuv_ref, o_ref,
                        m_all, l_all, acc_all, s_all, p_all, *, tq, tk, rc, qt):
    for j in range(qt):
        _attn_prompt_tile(pl.program_id(1) * qt + j,
                          ql_ref.at[0, :, j * tq:(j + 1) * tq, :], qp_ref.at[0, :, j * tq:(j + 1) * tq, :],
                          ckv_ref, kpe_ref, wuv_ref, o_ref.at[0, j * tq:(j + 1) * tq, :],
                          m_all.at[j], l_all.at[j], acc_all.at[j], s_all.at[j], p_all.at[j],
                          tq=tq, tk=tk, rc=rc)


def _attn_prompt_tile(qi, ql_ref, qp_ref, ckv_ref, kpe_ref, wuv_ref, o_ref,
                      m_scr, l_scr, acc_scr, s_scr, p_scr, *, tq, tk, rc):
    rows = N_HEADS * tq
    ql = ql_ref[...].reshape(rows, KV_LORA)
    qp = qp_ref[...].reshape(rows, QK_ROPE)
    m_scr[...] = jnp.full(m_scr.shape, NEG_BIG, F32)
    l_scr[...] = jnp.zeros(l_scr.shape, F32)
    acc_scr[...] = jnp.zeros(acc_scr.shape, F32)
    n_kt = (qi * tq + tq + tk - 1) // tk

    def scores(kt, slot):
        start = pl.multiple_of(kt * tk, tk)
        s_scr[slot] = (_dot_nt(ql, ckv_ref[0, pl.ds(start, tk), :])
                       + _dot_nt(qp, kpe_ref[0, pl.ds(start, tk), :]))

    def softmax(kt, slot, masked):
        for c in range(rows // rc):
            rs = slice(c * rc, (c + 1) * rc)
            s = s_scr[slot, rs, :]
            if masked:
                q_pos = qi * tq + (c * rc) % tq + lax.broadcasted_iota(jnp.int32, (rc, tk), 0)
                k_pos = kt * tk + lax.broadcasted_iota(jnp.int32, (rc, tk), 1)
                s = jnp.where(k_pos <= q_pos, s, NEG_BIG)
            p, m_new, l_new, alpha = _softmax_update(s, m_scr[rs, :], l_scr[rs, :])
            m_scr[rs, :] = m_new
            l_scr[rs, :] = l_new
            p_scr[rs, :] = p.astype(BF16)
            acc_scr[rs, :] = jnp.tile(alpha, (1, KV_LORA // LANE)) * acc_scr[rs, :]

    def weighted_values(kt):
        start = pl.multiple_of(kt * tk, tk)
        acc_scr[...] += _dot(p_scr[...], ckv_ref[0, pl.ds(start, tk), :])

    last = n_kt - 1
    scores(last, 0)
    softmax(last, 0, True)
    scores(0, 1)
    weighted_values(last)

    def body(kt, carry):
        slot = (kt + 1) % 2
        softmax(kt, slot, False)
        scores(kt + 1, 1 - slot)
        weighted_values(kt)
        return carry

    lax.fori_loop(0, last - 1, body, 0)

    @pl.when(last >= 1)
    def _():
        softmax(last - 1, last % 2, False)
        weighted_values(last - 1)
    o_lat = acc_scr[...] / jnp.tile(l_scr[...], (1, KV_LORA // LANE))
    o_ref[...] = _value_up_proj(o_lat, wuv_ref, tq).astype(o_ref.dtype)


def _attn_prompt(qlat, qpe, ckv_bf, kpe_bf, wuv_pad):
    bsz, _, seq, _ = qlat.shape
    tq = 128
    tk = min(256, seq)
    qt = 4 if seq % (4 * tq) == 0 else 1
    assert tk % tq == 0 and seq % tk == 0
    rc = 64
    kern = functools.partial(_attn_prompt_kernel, tq=tq, tk=tk, rc=rc, qt=qt)
    rows = N_HEADS * tq
    return pl.pallas_call(
        kern,
        grid=(bsz, seq // (qt * tq)),
        in_specs=[pl.BlockSpec((1, N_HEADS, qt * tq, KV_LORA), lambda b, q: (b, 0, q, 0)),
                  pl.BlockSpec((1, N_HEADS, qt * tq, QK_ROPE), lambda b, q: (b, 0, q, 0)),
                  pl.BlockSpec((1, seq, KV_LORA), lambda b, q: (b, 0, 0)),
                  pl.BlockSpec((1, seq, QK_ROPE), lambda b, q: (b, 0, 0)),
                  pl.BlockSpec(wuv_pad.shape, lambda b, q: (0, 0, 0))],
        out_specs=pl.BlockSpec((1, qt * tq, D_MLA), lambda b, q: (b, q, 0)),
        out_shape=jax.ShapeDtypeStruct((bsz, seq, D_MLA), BF16),
        scratch_shapes=[pltpu.VMEM((qt, rows, LANE), F32), pltpu.VMEM((qt, rows, LANE), F32),
                        pltpu.VMEM((qt, rows, KV_LORA), F32),
                        pltpu.VMEM((qt, 2, rows, tk), F32), pltpu.VMEM((qt, rows, tk), BF16)],
        compiler_params=_cparams(("parallel", "arbitrary")),
        name="attn_prompt",
    )(qlat, qpe, ckv_bf, kpe_bf, wuv_pad)


def _attn_sample_kernel(pt_ref, ql_ref, qp_ref, ckvn_ref, kpen_ref, wuv_ref, cache_ckv, cache_kpe_t,
                        o_ref, ckv_buf, kpe_buf, sem, *, layer, pp, cp, n_steps, ts):
    ahead = PAGE_SLOTS - 1
    b = pl.program_id(0)
    nb = pl.num_programs(0)
    rows = N_HEADS * ts
    total = nb * n_steps
    ql = ql_ref[0].reshape(rows, KV_LORA).astype(BF16)
    qp = qp_ref[0].reshape(rows, QK_ROPE).astype(BF16)

    def page_copies(seq, grp, slot):
        out = []
        for i in range(pp):
            page = pt_ref[seq, grp * pp + i]
            out.append(pltpu.make_async_copy(cache_ckv.at[layer, page], ckv_buf.at[slot, i], sem.at[slot]))
            out.append(pltpu.make_async_copy(cache_kpe_t.at[layer, page], kpe_buf.at[slot, i], sem.at[slot]))
        return out

    def start_group(g):
        slot = lax.rem(g, PAGE_SLOTS)
        g = jnp.minimum(g, total - 1)
        seq = lax.shift_right_logical(g, n_steps.bit_length() - 1)
        for c in page_copies(seq, jnp.bitwise_and(g, n_steps - 1), slot):
            c.start()

    def wait_group(g):
        for c in page_copies(0, 0, lax.rem(g, PAGE_SLOTS)):
            c.wait()

    @pl.when(b == 0)
    def _():
        for g in range(ahead):
            start_group(jnp.int32(g))

    def partial_softmax(s, vals):
        m = jnp.max(s, axis=-1, keepdims=True)
        p = jnp.exp2(s - m)
        return m, jnp.sum(p, axis=-1, keepdims=True), _dot(p.astype(BF16), vals)

    def merge(parts):
        m = functools.reduce(jnp.maximum, [pm for pm, _, _ in parts])
        l = sum(pl_ * jnp.exp2(pm - m) for pm, pl_, _ in parts)
        acc = sum(pa * jnp.exp2(pm - m) for pm, _, pa in parts)
        return m, l, acc

    def group_parts(slot):
        n_c = pp // cp
        cks = [ckv_buf[slot, c * cp:(c + 1) * cp].reshape(cp * PAGE, KV_LORA).astype(BF16) for c in range(n_c)]
        kps = [jnp.concatenate([kpe_buf[slot, c * cp + i].astype(BF16) for i in range(cp)], axis=1)
               for c in range(n_c)]
        ss = [_dot_nt(ql, ck) + _dot(qp, kp_t) for ck, kp_t in zip(cks, kps)]
        ms = [jnp.max(s, axis=-1, keepdims=True) for s in ss]
        ps = [jnp.exp2(s - m) for s, m in zip(ss, ms)]
        ls = [jnp.sum(p, axis=-1, keepdims=True) for p in ps]
        accs = [_dot(p.astype(BF16), ck) for p, ck in zip(ps, cks)]
        return list(zip(ms, ls, accs))

    ckn = ckvn_ref[0]
    kpn = kpen_ref[0]
    s_new = _dot_nt(ql, ckn) + _dot_nt(qp, kpn)
    t_q = lax.broadcasted_iota(jnp.int32, (rows, ts), 0) % ts
    t_k = lax.broadcasted_iota(jnp.int32, (rows, ts), 1)
    state = partial_softmax(jnp.where(t_k <= t_q, s_new, NEG_BIG), ckn)

    for st in range(n_steps):
        g = b * n_steps + st
        start_group(g + ahead)
        wait_group(g)
        state = merge([state] + group_parts(lax.rem(g, PAGE_SLOTS)))

    @pl.when(b == nb - 1)
    def _():
        for g in range(ahead):
            wait_group(total + g)

    _, l, acc = state
    o_ref[0] = _value_up_proj(acc / l, wuv_ref, ts).astype(o_ref.dtype)


def _attn_sample(page_table, qlat, qpe, ckv_new_bf, kpe_new_bf, wuv_pad, cache_ckv, cache_kpe, layer):
    bsz, _, ts, _ = qlat.shape
    n_pages = page_table.shape[1]
    pp = min(32, n_pages // 2)
    cp = min(8, pp)
    n_steps = n_pages // pp
    assert n_pages % pp == 0 and pp % cp == 0
    assert n_steps & (n_steps - 1) == 0
    kern = functools.partial(_attn_sample_kernel, layer=layer, pp=pp, cp=cp, n_steps=n_steps, ts=ts)

    cache_kpe_t = jnp.swapaxes(cache_kpe, 2, 3)

    grid_spec = pltpu.PrefetchScalarGridSpec(
        num_scalar_prefetch=1,
        grid=(bsz,),
        in_specs=[pl.BlockSpec((1, N_HEADS, ts, KV_LORA), lambda b, pt: (b, 0, 0, 0)),
                  pl.BlockSpec((1, N_HEADS, ts, QK_ROPE), lambda b, pt: (b, 0, 0, 0)),
                  pl.BlockSpec((1, ts, KV_LORA), lambda b, pt: (b, 0, 0)),
                  pl.BlockSpec((1, ts, QK_ROPE), lambda b, pt: (b, 0, 0)),
                  pl.BlockSpec(wuv_pad.shape, lambda b, pt: (0, 0, 0)),
                  pl.BlockSpec(memory_space=pl.ANY),
                  pl.BlockSpec(memory_space=pl.ANY)],
        out_specs=pl.BlockSpec((1, ts, D_MLA), lambda b, pt: (b, 0, 0)),
        scratch_shapes=[pltpu.VMEM((PAGE_SLOTS, pp, PAGE, KV_LORA), F32),
                        pltpu.VMEM((PAGE_SLOTS, pp, QK_ROPE, PAGE), F32),
                        pltpu.SemaphoreType.DMA((PAGE_SLOTS,))],
    )
    return pl.pallas_call(
        kern,
        grid_spec=grid_spec,
        out_shape=jax.ShapeDtypeStruct((bsz, ts, D_MLA), BF16),
        compiler_params=_cparams(("arbitrary",)),
        name="attn_sample",
    )(page_table, qlat, qpe, ckv_new_bf, kpe_new_bf, wuv_pad, cache_ckv, cache_kpe_t)


def _rwkv_prep_body(rw, mu_ref, w0_ref, w2_ref, a0_ref, a2_ref, kk_ref, ka_ref, rk_ref, bd_ref, sprev_ref,
                    q_out, w_out, k_out, v_out, kk_out, b_out, vkr_out, bonus_out, shift_out, buf):
    nb, tt, _ = rw.shape
    ti = pl.program_id(1)
    rows = nb * tt

    @pl.when(ti == 0)
    def _():
        buf[:, SUBLANE - 1:SUBLANE, :] = sprev_ref[...]

    @pl.when(ti > 0)
    def _():
        buf[:, SUBLANE - 1:SUBLANE, :] = buf[:, tt + SUBLANE - 1:tt + SUBLANE, :]

    buf[:, SUBLANE:, :] = rw
    shift_out[...] = rw[:, tt - 1:tt, :]
    rw_prev = buf[:, SUBLANE - 1:SUBLANE - 1 + tt, :]
    rws = (rw + mu_ref[...] * (rw_prev - rw)).reshape(rows, SHIFT_W)
    r = rws[:, 0:D_R]
    k = rws[:, D_R:2 * D_R]
    v = rws[:, 2 * D_R:3 * D_R]
    wa = rws[:, 3 * D_R:]
    w_log = -_softplus(-(w0_ref[...] + _dot(jnp.tanh(wa).astype(BF16), w2_ref[...]))) - 0.5
    decay = jnp.exp(-jnp.exp(w_log))
    a = _sigmoid(a0_ref[...] + _dot(wa.astype(BF16), a2_ref[...]))
    bd = bd_ref[...]
    kk = k * kk_ref[...]
    kk = kk / jnp.maximum(jnp.sqrt(_seg_sum(kk * kk, bd)), 1e-12)
    k = k * (1.0 + (a - 1.0) * ka_ref[...])
    bonus = _seg_sum(r * k * rk_ref[...], bd, split=False) * v
    b = kk * a
    q = decay * r - kk * _seg_sum(b * r, bd, split=False)
    vkr = v * _seg_sum(k * r, bd, split=False)
    shp = (nb, tt, D_R)
    q_out[...] = q.reshape(shp)
    w_out[...] = decay.reshape(shp)
    k_out[...] = k.reshape(shp)
    v_out[...] = v.reshape(shp)
    kk_out[...] = kk.reshape(shp)
    b_out[...] = b.reshape(shp)
    vkr_out[...] = vkr.reshape(shp)
    bonus_out[...] = bonus.reshape(shp)


def _rwkv_scan_kernel(q_ref, w_ref, k_ref, v_ref, kk_ref, b_ref, vkr_ref, s0_ref, bd_ref, eye_ref,
                      y_ref, sT_ref, s_scr):
    nb, tc, _ = q_ref.shape
    ci = pl.program_id(1)
    rows = nb * R_HEAD

    @pl.when(ci == 0)
    def _():
        s_scr[...] = s0_ref[...]

    bd = bd_ref[...]
    eye = eye_ref[...][None]
    eye_bf = eye.astype(BF16)

    gb = max(nb // SCAN_GROUPS, 1)

    def seg(x):
        return _dot(x.reshape(gb * R_HEAD, D_R), bd).reshape(gb, R_HEAD, D_R)

    def step(t, carry):
        for g0 in range(0, nb, gb):
            grp = slice(g0, g0 + gb)
            row = lambda ref: ref[grp, pl.ds(t, 1), :]
            s = s_scr[grp]
            s_bf = s.astype(BF16)
            sa = seg(s_bf * row(kk_ref).astype(BF16))
            y_col = seg(s_bf * row(q_ref).astype(BF16))
            v_col = seg(eye_bf * row(v_ref).astype(BF16))
            s_scr[grp] = s * row(w_ref) - sa * row(b_ref) + v_col * row(k_ref)
            y_ref[grp, pl.ds(t, 1), :] = jnp.sum(y_col * eye, axis=1, keepdims=True) + row(vkr_ref)
        return carry

    lax.fori_loop(0, tc, step, 0, unroll=SCAN_UNROLL)

    @pl.when(ci == pl.num_programs(1) - 1)
    def _():
        sT_ref[...] = s_scr[...]


def _rwkv_scan(q, w, k, v, kk, b, vkr, s0, bd, eye):
    bsz, seq, _ = q.shape
    nb = math.gcd(bsz, 8)
    tc = min(seq, 256)
    row = pl.BlockSpec((nb, tc, D_R), lambda bi, c: (bi, c, 0))
    st = pl.BlockSpec((nb, R_HEAD, D_R), lambda bi, c: (bi, 0, 0))
    return pl.pallas_call(
        _rwkv_scan_kernel,
        grid=(bsz // nb, seq // tc),
        in_specs=[row] * 7 + [st, pl.BlockSpec(bd.shape, lambda bi, c: (0, 0)),
                              pl.BlockSpec(eye.shape, lambda bi, c: (0, 0))],
        out_specs=[row, st],
        out_shape=[jax.ShapeDtypeStruct((bsz, seq, D_R), F32),
                   jax.ShapeDtypeStruct((bsz, R_HEAD, D_R), F32)],
        scratch_shapes=[pltpu.VMEM((nb, R_HEAD, D_R), F32)],
        compiler_params=_cparams(("parallel", "arbitrary")),
        name="rwkv_scan",
    )(q, w, k, v, kk, b, vkr, s0, bd, eye)


def _out_kernel(x_ref, mod_ref, omla_ref, szmla_ref, cbz_ref, ccx_ref, cprev_ref, szrw_ref,
                yr_ref, bonus_ref, gm_ref, wmla_ref, wconv_ref, wrw_ref, wout_ref, cw_ref,
                gng_ref, gnb_ref, bd_ref, fg_ref, xo_ref, cstate_ref, buf, *, final):
    nb, tt, d = x_ref.shape
    ti = pl.program_id(1)
    rows = nb * tt

    @pl.when(ti == 0)
    def _():
        buf[:, SUBLANE - 2:SUBLANE, :] = cprev_ref[...]

    @pl.when(ti > 0)
    def _():
        buf[:, SUBLANE - 2:SUBLANE, :] = buf[:, tt + SUBLANE - 2:tt + SUBLANE, :]

    buf[:, SUBLANE:, :] = ccx_ref[...]
    cstate_ref[...] = buf[:, tt + SUBLANE - 2:tt + SUBLANE, :]
    conv = (buf[:, SUBLANE - 2:SUBLANE - 2 + tt, :] * cw_ref[0:1, :]
            + buf[:, SUBLANE - 1:SUBLANE - 1 + tt, :] * cw_ref[1:2, :]
            + buf[:, SUBLANE:, :] * cw_ref[2:3, :])
    y_conv = _dot((cbz_ref[...] * conv).reshape(rows, D_CONV).astype(BF16), wconv_ref[...])

    y_mla = _dot((omla_ref[...].astype(F32) * szmla_ref[...].astype(F32)).reshape(rows, D_MLA).astype(BF16),
                 wmla_ref[...])

    bd = bd_ref[...]
    yr = yr_ref[...].reshape(rows, D_R)
    mu = _seg_sum(yr, bd) * (1.0 / R_HEAD)
    dy = yr - mu
    var = _seg_sum(dy * dy, bd) * (1.0 / R_HEAD)
    yn = dy * lax.rsqrt(var + GN_EPS) * gng_ref[...] + gnb_ref[...]
    o_rw = yn + bonus_ref[...].reshape(rows, D_R)
    y_rw = _dot((o_rw * szrw_ref[...].astype(F32).reshape(rows, D_R)).astype(BF16), wrw_ref[...])

    g = gm_ref[...].astype(F32).reshape(rows, N_BRANCH * d)
    merged = g[:, 0:d] * y_mla + g[:, d:2 * d] * y_conv + g[:, 2 * d:] * y_rw
    delta = _dot(merged.astype(BF16), wout_ref[...]).reshape(nb, tt, d)
    xo = x_ref[...] + mod_ref[:, :, 2 * d:] * delta
    if final:
        xo = xo * lax.rsqrt(jnp.mean(xo * xo, axis=-1, keepdims=True) + RMS_EPS) * fg_ref[...]
    xo_ref[...] = xo


def _out_proj(x, mod, o_mla, sz_mla, cbz, ccx, conv_prev, sz_rw, y_r, bonus, gm,
              wmla, wconv, wrw, wout, conv_w, gn_g, gn_b, bd, final_g, final):
    bsz, seq, d = x.shape
    nb, tt = _row_tiles(bsz, seq, 256)
    grid = (bsz // nb, seq // tt)
    row = lambda w: pl.BlockSpec((nb, tt, w), lambda b, t: (b, t, 0))
    full = lambda a: pl.BlockSpec(a.shape, lambda b, t: (0,) * a.ndim)
    consts = [wmla, wconv, wrw, wout, conv_w, gn_g.reshape(1, -1), gn_b.reshape(1, -1), bd,
              final_g.reshape(1, -1)]
    return pl.pallas_call(
        functools.partial(_out_kernel, final=final),
        grid=grid,
        in_specs=[row(d), pl.BlockSpec((nb, 1, 3 * d), lambda b, t: (b, 0, 0)),
                  row(D_MLA), row(D_MLA), row(D_CONV), row(D_CONV),
                  pl.BlockSpec((nb, CONV_W - 1, D_CONV), lambda b, t: (b, 0, 0)),
                  row(D_R), row(D_R), row(D_R), row(N_BRANCH * d)] + [full(a) for a in consts],
        out_specs=[row(d), pl.BlockSpec((nb, CONV_W - 1, D_CONV), lambda b, t: (b, 0, 0))],
        out_shape=[jax.ShapeDtypeStruct((bsz, seq, d), F32),
                   jax.ShapeDtypeStruct((bsz, CONV_W - 1, D_CONV), F32)],
        scratch_shapes=[pltpu.VMEM((nb, tt + SUBLANE, D_CONV), F32)],
        compiler_params=_cparams(("parallel", "arbitrary")),
        name="out_proj",
    )(x, mod, o_mla, sz_mla, cbz, ccx, conv_prev, sz_rw, y_r, bonus, gm, *consts)


def _swap_halves(w):
    half = QK_ROPE // 2
    return jnp.concatenate([w[..., half:], w[..., :half]], axis=-1)


def _pad_w_in(w_in):
    a = Q_LORA + KV_LORA
    k_rope = w_in[:, :, a:a + QK_ROPE]
    tail = w_in.shape[2] - a - QK_ROPE

    def place(piece, at):
        return jnp.pad(piece, ((0, 0), (0, 0), (at, PROJ_P - at - piece.shape[2])))

    return (place(w_in[:, :, :a], 0) + place(k_rope, a) + place(_swap_halves(k_rope), a + QK_ROPE)
            + place(w_in[:, :, a + QK_ROPE:], PROJ_P - tail)).astype(BF16)


def _layer_params(l, w_ada, w_in_p, w_q_b, w_uk, w_uv, w_mla_out, w_conv_out, rwkv_w2, rwkv_a2,
                  w_rwkv_out, w_out):
    w_in_p = w_in_p[l]
    wq = w_q_b[l].reshape(Q_LORA, N_HEADS, QK_NOPE + QK_ROPE)
    wn = jnp.pad(wq[:, :, :QK_NOPE], ((0, 0), (0, 0), (0, LANE - QK_NOPE))).reshape(Q_LORA, N_HEADS * LANE)
    wr = wq[:, :, QK_NOPE:]
    wrs = _swap_halves(wr)
    uk = jnp.pad(jnp.transpose(w_uk[l], (1, 2, 0)), ((0, 0), (0, LANE - QK_NOPE), (0, 0)))
    wuv = jnp.transpose(w_uv[l], (1, 0, 2))
    zv = jnp.zeros_like(wuv)
    even = (jnp.arange(N_HEADS) % 2 == 0)[:, None, None]
    wuv_pad = jnp.where(even, jnp.concatenate([wuv, zv], axis=2), jnp.concatenate([zv, wuv], axis=2))
    zeros = jnp.zeros((W_LORA, D_R), F32)
    return dict(
        w_ada=w_ada[l].astype(BF16), w_in_p=w_in_p,
        wn=wn.astype(BF16), wr=wr.reshape(Q_LORA, -1).astype(BF16), wrs=wrs.reshape(Q_LORA, -1).astype(BF16),
        uk=uk.astype(BF16), wuv_pad=wuv_pad.astype(BF16),
        wmla=w_mla_out[l].astype(BF16), wconv=w_conv_out[l].astype(BF16),
        wrw=w_rwkv_out[l].astype(BF16), wout=w_out[l].astype(BF16),
        w2p=jnp.concatenate([rwkv_w2[l], zeros], axis=0).astype(BF16),
        a2p=jnp.concatenate([zeros, rwkv_a2[l]], axis=0).astype(BF16),
    )


def _rope_tables(pos):
    half = QK_ROPE // 2
    inv = ROPE_THETA ** (-jnp.arange(half, dtype=F32) / half)
    ang = pos.astype(F32)[:, None] * inv[None, :]
    cos, sin = jnp.cos(ang), jnp.sin(ang)
    cos_k = jnp.concatenate([cos, cos], axis=1)
    sin_k = jnp.concatenate([-sin, sin], axis=1)
    return jnp.tile(cos_k, (1, N_HEADS)), jnp.tile(sin_k, (1, N_HEADS)), cos_k, sin_k


def _state_to_lanes(s):
    b = s.shape[0]
    return jnp.transpose(s, (0, 2, 1, 3)).reshape(b, R_HEAD, D_R)


def _state_from_lanes(s):
    b = s.shape[0]
    return jnp.transpose(s.reshape(b, R_HEAD, R_HEADS, R_HEAD), (0, 2, 1, 3))


def _layer(x, c_mod, tabs, p, vecs, conv_prev, shift_prev, s0, consts, final_g, final, attend, q_dtype):
    bd, eye = consts
    (sz_mla, cbz, ccx, sz_rw, gm, qlat, qpe, ckv, kpe, ckv_bf, kpe_bf,
     q, w, k, v, kk, b, vkr, bonus, shift_state) = _front(
        x, c_mod, vecs["norm_g"], p["w_in_p"], tabs, vecs["q_norm_g"], vecs["kv_norm_g"],
        p["wn"], p["wr"], p["wrs"], p["uk"], q_dtype,
        shift_prev, vecs["mu"], vecs["w0"], p["w2p"], vecs["a0"], p["a2p"],
        vecs["k_k"], vecs["k_a"], vecs["r_k"], bd)
    o_mla = attend(qlat, qpe, ckv_bf, kpe_bf, p["wuv_pad"])
    y_r, s_new = _rwkv_scan(q, w, k, v, kk, b, vkr, s0, bd, eye)
    x_new, conv_state = _out_proj(
        x, c_mod, o_mla, sz_mla, cbz, ccx, conv_prev, sz_rw, y_r, bonus, gm,
        p["wmla"], p["wconv"], p["wrw"], p["wout"], vecs["conv_w"], vecs["gn_g"], vecs["gn_b"], bd,
        final_g, final)
    return x_new, ckv, kpe, conv_state, shift_state[:, 0], s_new


def kernel(x_prompt, x_sample, cache_mla_ckv, cache_mla_kpe, state_conv, state_rwkv_shift, state_rwkv, page_table, c_prompt, c_sample, norm_g, w_ada, b_ada, w_in, q_norm_g, w_q_b, kv_norm_g, w_uk, w_uv, w_mla_out, conv_w, w_conv_out, rwkv_mu, rwkv_w0, rwkv_w2, rwkv_a0, rwkv_a2, rwkv_k_k, rwkv_k_a, rwkv_r_k, rwkv_gn_g, rwkv_gn_b, w_rwkv_out, w_out, final_norm_g):
    depth = norm_g.shape[0]
    bp, tp, _ = x_prompt.shape
    bs, ts, _ = x_sample.shape
    past = page_table.shape[1] * PAGE
    tabs_p = _rope_tables(jnp.arange(tp, dtype=jnp.int32))
    tabs_s = _rope_tables(past + jnp.arange(ts, dtype=jnp.int32))
    seg = jnp.arange(D_R, dtype=jnp.int32) // R_HEAD
    bd = (seg[:, None] == seg[None, :]).astype(BF16)
    eye = (jnp.arange(R_HEAD, dtype=jnp.int32)[:, None] == (jnp.arange(D_R, dtype=jnp.int32) % R_HEAD)[None, :]).astype(F32)
    consts = (bd, eye)
    w_in = _pad_w_in(w_in)
    xp, xs = x_prompt, x_sample
    st_p = ([], [], [], [], [])
    st_s = ([], [], [], [], [])
    for l in range(depth):
        final = l == depth - 1
        p = _layer_params(l, w_ada, w_in, w_q_b, w_uk, w_uv, w_mla_out, w_conv_out, rwkv_w2, rwkv_a2,
                          w_rwkv_out, w_out)
        vecs = dict(norm_g=norm_g[l], q_norm_g=q_norm_g[l], kv_norm_g=kv_norm_g[l], mu=rwkv_mu[l],
                    w0=rwkv_w0[l], a0=rwkv_a0[l], k_k=rwkv_k_k[l], k_a=rwkv_k_a[l],
                    r_k=rwkv_r_k[l].reshape(-1), conv_w=conv_w[l], gn_g=rwkv_gn_g[l], gn_b=rwkv_gn_b[l])
        mod_p = _ada_mod(c_prompt, p["w_ada"], b_ada[l])
        mod_s = _ada_mod(c_sample, p["w_ada"], b_ada[l])
        xp, ckv, kpe, cst, sst, rst = _layer(
            xp, mod_p, tabs_p, p, vecs,
            jnp.zeros((bp, CONV_W - 1, D_CONV), F32), jnp.zeros((bp, SHIFT_W), F32),
            jnp.zeros((bp, R_HEAD, D_R), F32), consts, final_norm_g, final, _attn_prompt, BF16)
        for lst, val in zip(st_p, (ckv, kpe, cst, sst, _state_from_lanes(rst))):
            lst.append(val)
        attend_s = functools.partial(_attn_sample_bound, page_table, cache_mla_ckv, cache_mla_kpe, l)
        xs, ckv, kpe, cst, sst, rst = _layer(
            xs, mod_s, tabs_s, p, vecs, state_conv[l], state_rwkv_shift[l],
            _state_to_lanes(state_rwkv[l]), consts, final_norm_g, final, attend_s, F32)
        for lst, val in zip(st_s, (ckv, kpe, cst, sst, _state_from_lanes(rst))):
            lst.append(val)
    return (xp, xs,
            jnp.stack(st_p[0]), jnp.stack(st_p[1]), jnp.stack(st_p[2]), jnp.stack(st_p[3]), jnp.stack(st_p[4]),
            jnp.stack(st_s[0]), jnp.stack(st_s[1]), jnp.stack(st_s[2]), jnp.stack(st_s[3]), jnp.stack(st_s[4]))


def _attn_sample_bound(page_table, cache_ckv, cache_kpe, layer, qlat, qpe, ckv_bf, kpe_bf, wuv_pad):
    return _attn_sample(page_table, qlat, qpe, ckv_bf, kpe_bf, wuv_pad, cache_ckv, cache_kpe, layer)
```

```python
import functools
import math

import jax
import jax.numpy as jnp
from jax import lax
from jax.experimental import pallas as pl
from jax.experimental.pallas import tpu as pltpu

F32 = jnp.float32
BF16 = jnp.bfloat16

D_MODEL = 1024
PAGE = 128
N_HEADS = 8
QK_NOPE = 64
QK_ROPE = 32
V_HEAD = 64
Q_LORA = 384
KV_LORA = 256
D_MLA = N_HEADS * V_HEAD
ROPE_THETA = 10000.0
ATTN_SCALE = (QK_NOPE + QK_ROPE) ** -0.5
Q_SCALE = ATTN_SCALE * math.log2(math.e)
D_CONV = 256
CONV_W = 3
R_HEADS = 4
R_HEAD = 64
D_R = R_HEADS * R_HEAD
W_LORA = 64
A_LORA = 64
SHIFT_W = 3 * D_R + W_LORA + A_LORA
GN_EPS = 64e-5
RMS_EPS = 1e-6
N_BRANCH = 3

LANE = 128
SUBLANE = 8
KR_W = LANE
SEG_W = (Q_LORA, KV_LORA, KR_W, D_MLA, 4 * D_CONV, SHIFT_W, D_R, N_BRANCH * D_MODEL)
SEG_OFF = tuple(sum(SEG_W[:i]) for i in range(len(SEG_W)))
PROJ_P = sum(SEG_W)
VMEM_LIMIT = 56 * 1024 * 1024
NEG_BIG = -1e30

ROW_TILE = 256
ATTN_TQ = 128
ATTN_TK = 256
ATTN_QT = 4
ATTN_RC = 64
PAGE_GROUP = 32
PAGE_PART = 8
PAGE_SLOTS = 4
SCAN_BATCH = 8
SCAN_CHUNK = 256
SCAN_UNROLL = 32
SCAN_GROUPS = 2


def _cparams(sem):
    return pltpu.CompilerParams(dimension_semantics=sem, vmem_limit_bytes=VMEM_LIMIT)


def _dot(a, b):
    return jnp.dot(a, b, preferred_element_type=F32)


def _dot_nt(a, b):
    return lax.dot_general(a, b, (((1,), (1,)), ((), ())), preferred_element_type=F32)


def _sigmoid(x):
    return 1.0 / (1.0 + jnp.exp(-x))


def _silu(x):
    return x * _sigmoid(x)


def _softplus(x):
    return jnp.maximum(x, 0.0) + jnp.log1p(jnp.exp(-jnp.abs(x)))


def _seg_sum(x, bd, split=True):
    hi = x.astype(BF16)
    if not split:
        return _dot(hi, bd)
    lo = (x - hi.astype(F32)).astype(BF16)
    return _dot(hi, bd) + _dot(lo, bd)


def _row_tiles(batch, seq, cap):
    if seq >= LANE:
        tt = min(seq, cap)
        assert seq % tt == 0
        return 1, tt
    assert seq % SUBLANE == 0
    return batch, seq


def _ada_kernel(c_ref, w_ref, b_ref, o_ref):
    s = _silu(c_ref[...])
    o_ref[...] = _dot(s.astype(BF16), w_ref[...]) + b_ref[...]


def _ada_mod(c, w_ada_bf, b_ada):
    bsz = c.shape[0]
    out = pl.pallas_call(
        _ada_kernel,
        out_shape=jax.ShapeDtypeStruct((bsz, 3 * D_MODEL), F32),
        compiler_params=pltpu.CompilerParams(vmem_limit_bytes=VMEM_LIMIT),
        name="ada_mod",
    )(c, w_ada_bf, b_ada.reshape(1, -1))
    return out.reshape(bsz, 1, 3 * D_MODEL)


N_MLA_IN, N_RWKV_IN = 10, 10
N_GATE_OUT = 5


def _front_kernel(x_ref, mod_ref, g_ref, w_ref, *refs):
    nb, tt, d = x_ref.shape
    mla_in = refs[:N_MLA_IN]
    rwkv_in = refs[N_MLA_IN:N_MLA_IN + N_RWKV_IN]
    outs = refs[N_MLA_IN + N_RWKV_IN:]
    zmla_out, cbz_out, ccx_out, zrw_out, gm_out = outs[:N_GATE_OUT]
    mla_out = outs[N_GATE_OUT:N_GATE_OUT + 6]
    rwkv_out = outs[N_GATE_OUT + 6:N_GATE_OUT + 15]
    buf = outs[N_GATE_OUT + 15]
    x = x_ref[...]
    ms = jnp.mean(x * x, axis=-1, keepdims=True)
    xn = x * lax.rsqrt(ms + RMS_EPS) * g_ref[...]
    shift = mod_ref[:, :, 0:D_MODEL]
    scale = mod_ref[:, :, D_MODEL:2 * D_MODEL]
    u = (xn * (1.0 + scale) + shift).reshape(nb * tt, d).astype(BF16)

    def seg(i):
        return _dot(u, w_ref[:, SEG_OFF[i]:SEG_OFF[i] + SEG_W[i]])

    gm_out[...] = _sigmoid(seg(7)).reshape(nb, tt, SEG_W[7]).astype(gm_out.dtype)
    _rwkv_prep_body(seg(5).reshape(nb, tt, SHIFT_W), *rwkv_in, *rwkv_out, buf)
    zmla_out[...] = _silu(seg(3)).reshape(nb, tt, SEG_W[3]).astype(zmla_out.dtype)
    c4 = seg(4)
    cbz_out[...] = (c4[:, 0:D_CONV] * _silu(c4[:, 3 * D_CONV:])).reshape(nb, tt, D_CONV)
    ccx_out[...] = (c4[:, D_CONV:2 * D_CONV] * c4[:, 2 * D_CONV:3 * D_CONV]).reshape(nb, tt, D_CONV)
    zrw_out[...] = _silu(seg(6)).reshape(nb, tt, SEG_W[6]).astype(zrw_out.dtype)
    _mla_prep_body(seg(0), seg(1).reshape(nb, tt, KV_LORA), seg(2).reshape(nb, tt, KR_W), nb, tt,
                   *mla_in, *mla_out)


def _front(x, mod, norm_g, w_in_p, tabs, q_norm_g, kv_norm_g, wn, wr, wrs, uk, q_dtype,
           shift_prev, mu, w0, w2p, a0, a2p, k_k, k_a, r_k, bd):
    bsz, seq, d = x.shape
    nb, tt = _row_tiles(bsz, seq, ROW_TILE)
    grid = (bsz // nb, seq // tt)
    cos_q, sin_q, cos_k, sin_k = tabs
    row = lambda w: pl.BlockSpec((nb, tt, w), lambda b, t: (b, t, 0))
    tab = lambda w: pl.BlockSpec((tt, w), lambda b, t: (t, 0))
    full = lambda a: pl.BlockSpec(a.shape, lambda b, t: (0,) * a.ndim)
    hq = lambda w: pl.BlockSpec((nb, N_HEADS, tt, w), lambda b, t: (b, 0, t, 0))
    state = pl.BlockSpec((nb, 1, SHIFT_W), lambda b, t: (b, 0, 0))
    mla_consts = [q_norm_g.reshape(1, -1), kv_norm_g.reshape(1, -1), wn, wr, wrs, uk]
    rwkv_consts = [mu.reshape(1, -1), w0.reshape(1, -1), w2p, a0.reshape(1, -1), a2p,
                   k_k.reshape(1, -1), k_a.reshape(1, -1), r_k.reshape(1, -1), bd]
    assert 4 + len(mla_consts) == N_MLA_IN and len(rwkv_consts) + 1 == N_RWKV_IN
    g = norm_g.reshape(1, d)
    outs = pl.pallas_call(
        _front_kernel,
        grid=grid,
        in_specs=[row(d), pl.BlockSpec((nb, 1, 3 * d), lambda b, t: (b, 0, 0)), full(g), full(w_in_p),
                  tab(N_HEADS * QK_ROPE), tab(N_HEADS * QK_ROPE), tab(QK_ROPE), tab(QK_ROPE)]
                 + [full(a) for a in mla_consts] + [full(a) for a in rwkv_consts[:-1]] + [full(bd), state],
        out_specs=[row(SEG_W[3]), row(D_CONV), row(D_CONV), row(SEG_W[6]), row(SEG_W[7]),
                   hq(KV_LORA), hq(QK_ROPE), row(KV_LORA), row(QK_ROPE), row(KV_LORA), row(QK_ROPE)]
                  + [row(D_R)] * 8 + [state],
        out_shape=[jax.ShapeDtypeStruct((bsz, seq, SEG_W[3]), BF16),
                   jax.ShapeDtypeStruct((bsz, seq, D_CONV), F32),
                   jax.ShapeDtypeStruct((bsz, seq, D_CONV), F32),
                   jax.ShapeDtypeStruct((bsz, seq, SEG_W[6]), BF16),
                   jax.ShapeDtypeStruct((bsz, seq, SEG_W[7]), BF16)]
                  + [jax.ShapeDtypeStruct((bsz, N_HEADS, seq, KV_LORA), q_dtype),
                     jax.ShapeDtypeStruct((bsz, N_HEADS, seq, QK_ROPE), q_dtype),
                     jax.ShapeDtypeStruct((bsz, seq, KV_LORA), F32),
                     jax.ShapeDtypeStruct((bsz, seq, QK_ROPE), F32),
                     jax.ShapeDtypeStruct((bsz, seq, KV_LORA), BF16),
                     jax.ShapeDtypeStruct((bsz, seq, QK_ROPE), BF16)]
                  + [jax.ShapeDtypeStruct((bsz, seq, D_R), F32)] * 8
                  + [jax.ShapeDtypeStruct((bsz, 1, SHIFT_W), F32)],
        scratch_shapes=[pltpu.VMEM((nb, tt + SUBLANE, SHIFT_W), F32)],
        compiler_params=_cparams(("parallel", "arbitrary")),
        name="front",
    )(x, mod, g, w_in_p, cos_q, sin_q, cos_k, sin_k, *mla_consts, *rwkv_consts,
      shift_prev.reshape(bsz, 1, SHIFT_W))
    return outs


def _mla_prep_body(qa, kva, kr, nb, tt, cq_ref, sq_ref, ck_ref, sk_ref,
                   gq_ref, gkv_ref, wn_ref, wr_ref, wrs_ref, uk_ref,
                   qlat_ref, qpe_ref, ckv_ref, kpe_ref, ckvb_ref, kpeb_ref):
    rows = nb * tt
    cq = qa * lax.rsqrt(jnp.mean(qa * qa, axis=-1, keepdims=True) + RMS_EPS) * gq_ref[...]
    cqb = cq.astype(BF16)
    qn = _dot(cqb, wn_ref[...])
    cos_q = jnp.broadcast_to(cq_ref[...][None], (nb, tt, N_HEADS * QK_ROPE)).reshape(rows, -1)
    sin_q = jnp.broadcast_to(sq_ref[...][None], (nb, tt, N_HEADS * QK_ROPE)).reshape(rows, -1)
    qp = (_dot(cqb, wr_ref[...]) * cos_q + _dot(cqb, wrs_ref[...]) * sin_q) * Q_SCALE
    for h in range(N_HEADS):
        qn_h = qn[:, h * LANE:(h + 1) * LANE].astype(BF16)
        ql = _dot(qn_h, uk_ref[h]) * Q_SCALE
        qlat_ref[:, h] = ql.reshape(nb, tt, KV_LORA).astype(qlat_ref.dtype)
        qpe_ref[:, h] = qp[:, h * QK_ROPE:(h + 1) * QK_ROPE].reshape(nb, tt, QK_ROPE).astype(qpe_ref.dtype)
    ckv = kva * lax.rsqrt(jnp.mean(kva * kva, axis=-1, keepdims=True) + RMS_EPS) * gkv_ref[...]
    ckv_ref[...] = ckv
    ckvb_ref[...] = ckv.astype(BF16)
    kpe = kr[:, :, 0:QK_ROPE] * ck_ref[...][None] + kr[:, :, QK_ROPE:2 * QK_ROPE] * sk_ref[...][None]
    kpe_ref[...] = kpe
    kpeb_ref[...] = kpe.astype(BF16)


def _value_up_proj(o_lat, wuv_ref, t):
    tiles = []
    for j in range(N_HEADS // 2):
        h0, h1 = 2 * j, 2 * j + 1
        tiles.append(_dot(o_lat[h0 * t:(h0 + 1) * t].astype(BF16), wuv_ref[h0])
                     + _dot(o_lat[h1 * t:(h1 + 1) * t].astype(BF16), wuv_ref[h1]))
    return jnp.concatenate(tiles, axis=1)


def _softmax_update(s, m_old, l_old):
    m_new = jnp.maximum(m_old, jnp.max(s, axis=-1, keepdims=True))
    alpha = jnp.exp2(m_old - m_new)
    p = jnp.exp2(s - jnp.tile(m_new, (1, s.shape[1] // LANE)))
    l_new = alpha * l_old + jnp.sum(p, axis=-1, keepdims=True)
    return p, m_new, l_new, alpha


def _attn_prompt_kernel(ql_ref, qp_ref, ckv_ref, kpe_ref, wuv_ref, o_ref,
                        m_all, l_all, acc_all, s_all, p_all, *, tq, tk, rc, qt):
    for j in range(qt):
        _attn_prompt_tile(pl.program_id(1) * qt + j,
                          ql_ref.at[0, :, j * tq:(j + 1) * tq, :], qp_ref.at[0, :, j * tq:(j + 1) * tq, :],
                          ckv_ref, kpe_ref, wuv_ref, o_ref.at[0, j * tq:(j + 1) * tq, :],
                          m_all.at[j], l_all.at[j], acc_all.at[j], s_all.at[j], p_all.at[j],
                          tq=tq, tk=tk, rc=rc)


def _attn_prompt_tile(qi, ql_ref, qp_ref, ckv_ref, kpe_ref, wuv_ref, o_ref,
                      m_scr, l_scr, acc_scr, s_scr, p_scr, *, tq, tk, rc):
    rows = N_HEADS * tq
    ql = ql_ref[...].reshape(rows, KV_LORA)
    qp = qp_ref[...].reshape(rows, QK_ROPE)
    m_scr[...] = jnp.full(m_scr.shape, NEG_BIG, F32)
    l_scr[...] = jnp.zeros(l_scr.shape, F32)
    acc_scr[...] = jnp.zeros(acc_scr.shape, F32)
    n_kt = (qi * tq + tq + tk - 1) // tk

    def scores(kt, slot):
        start = pl.multiple_of(kt * tk, tk)
        s_scr[slot] = (_dot_nt(ql, ckv_ref[0, pl.ds(start, tk), :])
                       + _dot_nt(qp, kpe_ref[0, pl.ds(start, tk), :]))

    def softmax(kt, slot, masked):
        for c in range(rows // rc):
            rs = slice(c * rc, (c + 1) * rc)
            s = s_scr[slot, rs, :]
            if masked:
                q_pos = qi * tq + (c * rc) % tq + lax.broadcasted_iota(jnp.int32, (rc, tk), 0)
                k_pos = kt * tk + lax.broadcasted_iota(jnp.int32, (rc, tk), 1)
                s = jnp.where(k_pos <= q_pos, s, NEG_BIG)
            p, m_new, l_new, alpha = _softmax_update(s, m_scr[rs, :], l_scr[rs, :])
            m_scr[rs, :] = m_new
            l_scr[rs, :] = l_new
            p_scr[rs, :] = p.astype(BF16)
            acc_scr[rs, :] = jnp.tile(alpha, (1, KV_LORA // LANE)) * acc_scr[rs, :]

    def weighted_values(kt):
        start = pl.multiple_of(kt * tk, tk)
        acc_scr[...] += _dot(p_scr[...], ckv_ref[0, pl.ds(start, tk), :])

    last = n_kt - 1
    scores(last, 0)
    softmax(last, 0, True)
    scores(0, 1)
    weighted_values(last)

    def body(kt, carry):
        slot = (kt + 1) % 2
        softmax(kt, slot, False)
        scores(kt + 1, 1 - slot)
        weighted_values(kt)
        return carry

    lax.fori_loop(0, last - 1, body, 0)

    @pl.when(last >= 1)
    def _():
        softmax(last - 1, last % 2, False)
        weighted_values(last - 1)
    o_lat = acc_scr[...] / jnp.tile(l_scr[...], (1, KV_LORA // LANE))
    o_ref[...] = _value_up_proj(o_lat, wuv_ref, tq).astype(o_ref.dtype)


def _attn_prompt(qlat, qpe, ckv_bf, kpe_bf, wuv_pad):
    bsz, _, seq, _ = qlat.shape
    tq = ATTN_TQ
    tk = min(ATTN_TK, seq)
    qt = ATTN_QT if seq % (ATTN_QT * tq) == 0 else 1
    assert tk % tq == 0 and seq % tk == 0
    rc = ATTN_RC
    kern = functools.partial(_attn_prompt_kernel, tq=tq, tk=tk, rc=rc, qt=qt)
    rows = N_HEADS * tq
    return pl.pallas_call(
        kern,
        grid=(bsz, seq // (qt * tq)),
        in_specs=[pl.BlockSpec((1, N_HEADS, qt * tq, KV_LORA), lambda b, q: (b, 0, q, 0)),
                  pl.BlockSpec((1, N_HEADS, qt * tq, QK_ROPE), lambda b, q: (b, 0, q, 0)),
                  pl.BlockSpec((1, seq, KV_LORA), lambda b, q: (b, 0, 0)),
                  pl.BlockSpec((1, seq, QK_ROPE), lambda b, q: (b, 0, 0)),
                  pl.BlockSpec(wuv_pad.shape, lambda b, q: (0, 0, 0))],
        out_specs=pl.BlockSpec((1, qt * tq, D_MLA), lambda b, q: (b, q, 0)),
        out_shape=jax.ShapeDtypeStruct((bsz, seq, D_MLA), BF16),
        scratch_shapes=[pltpu.VMEM((qt, rows, LANE), F32), pltpu.VMEM((qt, rows, LANE), F32),
                        pltpu.VMEM((qt, rows, KV_LORA), F32),
                        pltpu.VMEM((qt, 2, rows, tk), F32), pltpu.VMEM((qt, rows, tk), BF16)],
        compiler_params=_cparams(("parallel", "arbitrary")),
        name="attn_prompt",
    )(qlat, qpe, ckv_bf, kpe_bf, wuv_pad)


def _attn_sample_kernel(pt_ref, ql_ref, qp_ref, ckvn_ref, kpen_ref, wuv_ref, cache_ckv, cache_kpe_t,
                        o_ref, ckv_buf, kpe_buf, sem, *, layer, pp, cp, n_steps, ts):
    ahead = PAGE_SLOTS - 1
    b = pl.program_id(0)
    nb = pl.num_programs(0)
    rows = N_HEADS * ts
    total = nb * n_steps
    ql = ql_ref[0].reshape(rows, KV_LORA).astype(BF16)
    qp = qp_ref[0].reshape(rows, QK_ROPE).astype(BF16)

    def page_copies(seq, grp, slot):
        out = []
        for i in range(pp):
            page = pt_ref[seq, grp * pp + i]
            out.append(pltpu.make_async_copy(cache_ckv.at[layer, page], ckv_buf.at[slot, i], sem.at[slot]))
            out.append(pltpu.make_async_copy(cache_kpe_t.at[layer, page], kpe_buf.at[slot, i], sem.at[slot]))
        return out

    def start_group(g):
        slot = lax.rem(g, PAGE_SLOTS)
        g = jnp.minimum(g, total - 1)
        seq = lax.shift_right_logical(g, n_steps.bit_length() - 1)
        for c in page_copies(seq, jnp.bitwise_and(g, n_steps - 1), slot):
            c.start()

    def wait_group(g):
        for c in page_copies(0, 0, lax.rem(g, PAGE_SLOTS)):
            c.wait()

    @pl.when(b == 0)
    def _():
        for g in range(ahead):
            start_group(jnp.int32(g))

    def partial_softmax(s, vals):
        m = jnp.max(s, axis=-1, keepdims=True)
        p = jnp.exp2(s - m)
        return m, jnp.sum(p, axis=-1, keepdims=True), _dot(p.astype(BF16), vals)

    def merge(parts):
        m = functools.reduce(jnp.maximum, [pm for pm, _, _ in parts])
        l = sum(pl_ * jnp.exp2(pm - m) for pm, pl_, _ in parts)
        acc = sum(pa * jnp.exp2(pm - m) for pm, _, pa in parts)
        return m, l, acc

    def group_parts(slot):
        n_c = pp // cp
        cks = [ckv_buf[slot, c * cp:(c + 1) * cp].reshape(cp * PAGE, KV_LORA).astype(BF16) for c in range(n_c)]
        kps = [jnp.concatenate([kpe_buf[slot, c * cp + i].astype(BF16) for i in range(cp)], axis=1)
               for c in range(n_c)]
        ss = [_dot_nt(ql, ck) + _dot(qp, kp_t) for ck, kp_t in zip(cks, kps)]
        ms = [jnp.max(s, axis=-1, keepdims=True) for s in ss]
        ps = [jnp.exp2(s - m) for s, m in zip(ss, ms)]
        ls = [jnp.sum(p, axis=-1, keepdims=True) for p in ps]
        accs = [_dot(p.astype(BF16), ck) for p, ck in zip(ps, cks)]
        return list(zip(ms, ls, accs))

    ckn = ckvn_ref[0]
    kpn = kpen_ref[0]
    s_new = _dot_nt(ql, ckn) + _dot_nt(qp, kpn)
    t_q = lax.broadcasted_iota(jnp.int32, (rows, ts), 0) % ts
    t_k = lax.broadcasted_iota(jnp.int32, (rows, ts), 1)
    state = partial_softmax(jnp.where(t_k <= t_q, s_new, NEG_BIG), ckn)

    for st in range(n_steps):
        g = b * n_steps + st
        start_group(g + ahead)
        wait_group(g)
        state = merge([state] + group_parts(lax.rem(g, PAGE_SLOTS)))

    @pl.when(b == nb - 1)
    def _():
        for g in range(ahead):
            wait_group(total + g)

    _, l, acc = state
    o_ref[0] = _value_up_proj(acc / l, wuv_ref, ts).astype(o_ref.dtype)


def _attn_sample(page_table, qlat, qpe, ckv_new_bf, kpe_new_bf, wuv_pad, cache_ckv, cache_kpe, layer):
    bsz, _, ts, _ = qlat.shape
    n_pages = page_table.shape[1]
    pp = min(PAGE_GROUP, n_pages // 2)
    cp = min(PAGE_PART, pp)
    n_steps = n_pages // pp
    assert n_pages % pp == 0 and pp % cp == 0
    assert n_steps & (n_steps - 1) == 0
    kern = functools.partial(_attn_sample_kernel, layer=layer, pp=pp, cp=cp, n_steps=n_steps, ts=ts)

    cache_kpe_t = jnp.swapaxes(cache_kpe, 2, 3)

    grid_spec = pltpu.PrefetchScalarGridSpec(
        num_scalar_prefetch=1,
        grid=(bsz,),
        in_specs=[pl.BlockSpec((1, N_HEADS, ts, KV_LORA), lambda b, pt: (b, 0, 0, 0)),
                  pl.BlockSpec((1, N_HEADS, ts, QK_ROPE), lambda b, pt: (b, 0, 0, 0)),
                  pl.BlockSpec((1, ts, KV_LORA), lambda b, pt: (b, 0, 0)),
                  pl.BlockSpec((1, ts, QK_ROPE), lambda b, pt: (b, 0, 0)),
                  pl.BlockSpec(wuv_pad.shape, lambda b, pt: (0, 0, 0)),
                  pl.BlockSpec(memory_space=pl.ANY),
                  pl.BlockSpec(memory_space=pl.ANY)],
        out_specs=pl.BlockSpec((1, ts, D_MLA), lambda b, pt: (b, 0, 0)),
        scratch_shapes=[pltpu.VMEM((PAGE_SLOTS, pp, PAGE, KV_LORA), F32),
                        pltpu.VMEM((PAGE_SLOTS, pp, QK_ROPE, PAGE), F32),
                        pltpu.SemaphoreType.DMA((PAGE_SLOTS,))],
    )
    return pl.pallas_call(
        kern,
        grid_spec=grid_spec,
        out_shape=jax.ShapeDtypeStruct((bsz, ts, D_MLA), BF16),
        compiler_params=_cparams(("arbitrary",)),
        name="attn_sample",
    )(page_table, qlat, qpe, ckv_new_bf, kpe_new_bf, wuv_pad, cache_ckv, cache_kpe_t)


def _rwkv_prep_body(rw, mu_ref, w0_ref, w2_ref, a0_ref, a2_ref, kk_ref, ka_ref, rk_ref, bd_ref, sprev_ref,
                    q_out, w_out, k_out, v_out, kk_out, b_out, vkr_out, bonus_out, shift_out, buf):
    nb, tt, _ = rw.shape
    ti = pl.program_id(1)
    rows = nb * tt

    @pl.when(ti == 0)
    def _():
        buf[:, SUBLANE - 1:SUBLANE, :] = sprev_ref[...]

    @pl.when(ti > 0)
    def _():
        buf[:, SUBLANE - 1:SUBLANE, :] = buf[:, tt + SUBLANE - 1:tt + SUBLANE, :]

    buf[:, SUBLANE:, :] = rw
    shift_out[...] = rw[:, tt - 1:tt, :]
    rw_prev = buf[:, SUBLANE - 1:SUBLANE - 1 + tt, :]
    rws = (rw + mu_ref[...] * (rw_prev - rw)).reshape(rows, SHIFT_W)
    r = rws[:, 0:D_R]
    k = rws[:, D_R:2 * D_R]
    v = rws[:, 2 * D_R:3 * D_R]
    wa = rws[:, 3 * D_R:]
    w_log = -_softplus(-(w0_ref[...] + _dot(jnp.tanh(wa).astype(BF16), w2_ref[...]))) - 0.5
    decay = jnp.exp(-jnp.exp(w_log))
    a = _sigmoid(a0_ref[...] + _dot(wa.astype(BF16), a2_ref[...]))
    bd = bd_ref[...]
    kk = k * kk_ref[...]
    kk = kk / jnp.maximum(jnp.sqrt(_seg_sum(kk * kk, bd)), 1e-12)
    k = k * (1.0 + (a - 1.0) * ka_ref[...])
    bonus = _seg_sum(r * k * rk_ref[...], bd, split=False) * v
    b = kk * a
    q = decay * r - kk * _seg_sum(b * r, bd, split=False)
    vkr = v * _seg_sum(k * r, bd, split=False)
    shp = (nb, tt, D_R)
    q_out[...] = q.reshape(shp)
    w_out[...] = decay.reshape(shp)
    k_out[...] = k.reshape(shp)
    v_out[...] = v.reshape(shp)
    kk_out[...] = kk.reshape(shp)
    b_out[...] = b.reshape(shp)
    vkr_out[...] = vkr.reshape(shp)
    bonus_out[...] = bonus.reshape(shp)


def _rwkv_scan_kernel(q_ref, w_ref, k_ref, v_ref, kk_ref, b_ref, vkr_ref, s0_ref, bd_ref, eye_ref,
                      y_ref, sT_ref, s_scr):
    nb, tc, _ = q_ref.shape
    ci = pl.program_id(1)
    rows = nb * R_HEAD

    @pl.when(ci == 0)
    def _():
        s_scr[...] = s0_ref[...]

    bd = bd_ref[...]
    eye = eye_ref[...][None]
    eye_bf = eye.astype(BF16)

    gb = max(nb // SCAN_GROUPS, 1)

    def seg(x):
        return _dot(x.reshape(gb * R_HEAD, D_R), bd).reshape(gb, R_HEAD, D_R)

    def step(t, carry):
        for g0 in range(0, nb, gb):
            grp = slice(g0, g0 + gb)
            row = lambda ref: ref[grp, pl.ds(t, 1), :]
            s = s_scr[grp]
            s_bf = s.astype(BF16)
            sa = seg(s_bf * row(kk_ref).astype(BF16))
            y_col = seg(s_bf * row(q_ref).astype(BF16))
            v_col = seg(eye_bf * row(v_ref).astype(BF16))
            s_scr[grp] = s * row(w_ref) - sa * row(b_ref) + v_col * row(k_ref)
            y_ref[grp, pl.ds(t, 1), :] = jnp.sum(y_col * eye, axis=1, keepdims=True) + row(vkr_ref)
        return carry

    lax.fori_loop(0, tc, step, 0, unroll=SCAN_UNROLL)

    @pl.when(ci == pl.num_programs(1) - 1)
    def _():
        sT_ref[...] = s_scr[...]


def _rwkv_scan(q, w, k, v, kk, b, vkr, s0, bd, eye):
    bsz, seq, _ = q.shape
    nb = math.gcd(bsz, SCAN_BATCH)
    tc = min(seq, SCAN_CHUNK)
    row = pl.BlockSpec((nb, tc, D_R), lambda bi, c: (bi, c, 0))
    st = pl.BlockSpec((nb, R_HEAD, D_R), lambda bi, c: (bi, 0, 0))
    return pl.pallas_call(
        _rwkv_scan_kernel,
        grid=(bsz // nb, seq // tc),
        in_specs=[row] * 7 + [st, pl.BlockSpec(bd.shape, lambda bi, c: (0, 0)),
                              pl.BlockSpec(eye.shape, lambda bi, c: (0, 0))],
        out_specs=[row, st],
        out_shape=[jax.ShapeDtypeStruct((bsz, seq, D_R), F32),
                   jax.ShapeDtypeStruct((bsz, R_HEAD, D_R), F32)],
        scratch_shapes=[pltpu.VMEM((nb, R_HEAD, D_R), F32)],
        compiler_params=_cparams(("parallel", "arbitrary")),
        name="rwkv_scan",
    )(q, w, k, v, kk, b, vkr, s0, bd, eye)


def _out_kernel(x_ref, mod_ref, omla_ref, szmla_ref, cbz_ref, ccx_ref, cprev_ref, szrw_ref,
                yr_ref, bonus_ref, gm_ref, wmla_ref, wconv_ref, wrw_ref, wout_ref, cw_ref,
                gng_ref, gnb_ref, bd_ref, fg_ref, xo_ref, cstate_ref, buf, *, final):
    nb, tt, d = x_ref.shape
    ti = pl.program_id(1)
    rows = nb * tt

    @pl.when(ti == 0)
    def _():
        buf[:, SUBLANE - 2:SUBLANE, :] = cprev_ref[...]

    @pl.when(ti > 0)
    def _():
        buf[:, SUBLANE - 2:SUBLANE, :] = buf[:, tt + SUBLANE - 2:tt + SUBLANE, :]

    buf[:, SUBLANE:, :] = ccx_ref[...]
    cstate_ref[...] = buf[:, tt + SUBLANE - 2:tt + SUBLANE, :]
    conv = (buf[:, SUBLANE - 2:SUBLANE - 2 + tt, :] * cw_ref[0:1, :]
            + buf[:, SUBLANE - 1:SUBLANE - 1 + tt, :] * cw_ref[1:2, :]
            + buf[:, SUBLANE:, :] * cw_ref[2:3, :])
    y_conv = _dot((cbz_ref[...] * conv).reshape(rows, D_CONV).astype(BF16), wconv_ref[...])

    y_mla = _dot((omla_ref[...].astype(F32) * szmla_ref[...].astype(F32)).reshape(rows, D_MLA).astype(BF16),
                 wmla_ref[...])

    bd = bd_ref[...]
    yr = yr_ref[...].reshape(rows, D_R)
    mu = _seg_sum(yr, bd) * (1.0 / R_HEAD)
    dy = yr - mu
    var = _seg_sum(dy * dy, bd) * (1.0 / R_HEAD)
    yn = dy * lax.rsqrt(var + GN_EPS) * gng_ref[...] + gnb_ref[...]
    o_rw = yn + bonus_ref[...].reshape(rows, D_R)
    y_rw = _dot((o_rw * szrw_ref[...].astype(F32).reshape(rows, D_R)).astype(BF16), wrw_ref[...])

    g = gm_ref[...].astype(F32).reshape(rows, N_BRANCH * d)
    merged = g[:, 0:d] * y_mla + g[:, d:2 * d] * y_conv + g[:, 2 * d:] * y_rw
    delta = _dot(merged.astype(BF16), wout_ref[...]).reshape(nb, tt, d)
    xo = x_ref[...] + mod_ref[:, :, 2 * d:] * delta
    if final:
        xo = xo * lax.rsqrt(jnp.mean(xo * xo, axis=-1, keepdims=True) + RMS_EPS) * fg_ref[...]
    xo_ref[...] = xo


def _out_proj(x, mod, o_mla, sz_mla, cbz, ccx, conv_prev, sz_rw, y_r, bonus, gm,
              wmla, wconv, wrw, wout, conv_w, gn_g, gn_b, bd, final_g, final):
    bsz, seq, d = x.shape
    nb, tt = _row_tiles(bsz, seq, ROW_TILE)
    grid = (bsz // nb, seq // tt)
    row = lambda w: pl.BlockSpec((nb, tt, w), lambda b, t: (b, t, 0))
    full = lambda a: pl.BlockSpec(a.shape, lambda b, t: (0,) * a.ndim)
    consts = [wmla, wconv, wrw, wout, conv_w, gn_g.reshape(1, -1), gn_b.reshape(1, -1), bd,
              final_g.reshape(1, -1)]
    return pl.pallas_call(
        functools.partial(_out_kernel, final=final),
        grid=grid,
        in_specs=[row(d), pl.BlockSpec((nb, 1, 3 * d), lambda b, t: (b, 0, 0)),
                  row(D_MLA), row(D_MLA), row(D_CONV), row(D_CONV),
                  pl.BlockSpec((nb, CONV_W - 1, D_CONV), lambda b, t: (b, 0, 0)),
                  row(D_R), row(D_R), row(D_R), row(N_BRANCH * d)] + [full(a) for a in consts],
        out_specs=[row(d), pl.BlockSpec((nb, CONV_W - 1, D_CONV), lambda b, t: (b, 0, 0))],
        out_shape=[jax.ShapeDtypeStruct((bsz, seq, d), F32),
                   jax.ShapeDtypeStruct((bsz, CONV_W - 1, D_CONV), F32)],
        scratch_shapes=[pltpu.VMEM((nb, tt + SUBLANE, D_CONV), F32)],
        compiler_params=_cparams(("parallel", "arbitrary")),
        name="out_proj",
    )(x, mod, o_mla, sz_mla, cbz, ccx, conv_prev, sz_rw, y_r, bonus, gm, *consts)


def _swap_halves(w):
    half = QK_ROPE // 2
    return jnp.concatenate([w[..., half:], w[..., :half]], axis=-1)


def _pad_w_in(w_in):
    a = Q_LORA + KV_LORA
    k_rope = w_in[:, :, a:a + QK_ROPE]
    tail = w_in.shape[2] - a - QK_ROPE

    def place(piece, at):
        return jnp.pad(piece, ((0, 0), (0, 0), (at, PROJ_P - at - piece.shape[2])))

    return (place(w_in[:, :, :a], 0) + place(k_rope, a) + place(_swap_halves(k_rope), a + QK_ROPE)
            + place(w_in[:, :, a + QK_ROPE:], PROJ_P - tail)).astype(BF16)


def _layer_params(l, w_ada, w_in_p, w_q_b, w_uk, w_uv, w_mla_out, w_conv_out, rwkv_w2, rwkv_a2,
                  w_rwkv_out, w_out):
    w_in_p = w_in_p[l]
    wq = w_q_b[l].reshape(Q_LORA, N_HEADS, QK_NOPE + QK_ROPE)
    wn = jnp.pad(wq[:, :, :QK_NOPE], ((0, 0), (0, 0), (0, LANE - QK_NOPE))).reshape(Q_LORA, N_HEADS * LANE)
    wr = wq[:, :, QK_NOPE:]
    wrs = _swap_halves(wr)
    uk = jnp.pad(jnp.transpose(w_uk[l], (1, 2, 0)), ((0, 0), (0, LANE - QK_NOPE), (0, 0)))
    wuv = jnp.transpose(w_uv[l], (1, 0, 2))
    zv = jnp.zeros_like(wuv)
    even = (jnp.arange(N_HEADS) % 2 == 0)[:, None, None]
    wuv_pad = jnp.where(even, jnp.concatenate([wuv, zv], axis=2), jnp.concatenate([zv, wuv], axis=2))
    zeros = jnp.zeros((W_LORA, D_R), F32)
    return dict(
        w_ada=w_ada[l].astype(BF16), w_in_p=w_in_p,
        wn=wn.astype(BF16), wr=wr.reshape(Q_LORA, -1).astype(BF16), wrs=wrs.reshape(Q_LORA, -1).astype(BF16),
        uk=uk.astype(BF16), wuv_pad=wuv_pad.astype(BF16),
        wmla=w_mla_out[l].astype(BF16), wconv=w_conv_out[l].astype(BF16),
        wrw=w_rwkv_out[l].astype(BF16), wout=w_out[l].astype(BF16),
        w2p=jnp.concatenate([rwkv_w2[l], zeros], axis=0).astype(BF16),
        a2p=jnp.concatenate([zeros, rwkv_a2[l]], axis=0).astype(BF16),
    )


def _rope_tables(pos):
    half = QK_ROPE // 2
    inv = ROPE_THETA ** (-jnp.arange(half, dtype=F32) / half)
    ang = pos.astype(F32)[:, None] * inv[None, :]
    cos, sin = jnp.cos(ang), jnp.sin(ang)
    cos_k = jnp.concatenate([cos, cos], axis=1)
    sin_k = jnp.concatenate([-sin, sin], axis=1)
    return jnp.tile(cos_k, (1, N_HEADS)), jnp.tile(sin_k, (1, N_HEADS)), cos_k, sin_k


def _state_to_lanes(s):
    b = s.shape[0]
    return jnp.transpose(s, (0, 2, 1, 3)).reshape(b, R_HEAD, D_R)


def _state_from_lanes(s):
    b = s.shape[0]
    return jnp.transpose(s.reshape(b, R_HEAD, R_HEADS, R_HEAD), (0, 2, 1, 3))


def _layer(x, c_mod, tabs, p, vecs, conv_prev, shift_prev, s0, consts, final_g, final, attend, q_dtype):
    bd, eye = consts
    (sz_mla, cbz, ccx, sz_rw, gm, qlat, qpe, ckv, kpe, ckv_bf, kpe_bf,
     q, w, k, v, kk, b, vkr, bonus, shift_state) = _front(
        x, c_mod, vecs["norm_g"], p["w_in_p"], tabs, vecs["q_norm_g"], vecs["kv_norm_g"],
        p["wn"], p["wr"], p["wrs"], p["uk"], q_dtype,
        shift_prev, vecs["mu"], vecs["w0"], p["w2p"], vecs["a0"], p["a2p"],
        vecs["k_k"], vecs["k_a"], vecs["r_k"], bd)
    o_mla = attend(qlat, qpe, ckv_bf, kpe_bf, p["wuv_pad"])
    y_r, s_new = _rwkv_scan(q, w, k, v, kk, b, vkr, s0, bd, eye)
    x_new, conv_state = _out_proj(
        x, c_mod, o_mla, sz_mla, cbz, ccx, conv_prev, sz_rw, y_r, bonus, gm,
        p["wmla"], p["wconv"], p["wrw"], p["wout"], vecs["conv_w"], vecs["gn_g"], vecs["gn_b"], bd,
        final_g, final)
    return x_new, ckv, kpe, conv_state, shift_state[:, 0], s_new


def kernel(x_prompt, x_sample, cache_mla_ckv, cache_mla_kpe, state_conv, state_rwkv_shift, state_rwkv, page_table, c_prompt, c_sample, norm_g, w_ada, b_ada, w_in, q_norm_g, w_q_b, kv_norm_g, w_uk, w_uv, w_mla_out, conv_w, w_conv_out, rwkv_mu, rwkv_w0, rwkv_w2, rwkv_a0, rwkv_a2, rwkv_k_k, rwkv_k_a, rwkv_r_k, rwkv_gn_g, rwkv_gn_b, w_rwkv_out, w_out, final_norm_g):
    depth = norm_g.shape[0]
    bp, tp, _ = x_prompt.shape
    bs, ts, _ = x_sample.shape
    past = page_table.shape[1] * PAGE
    tabs_p = _rope_tables(jnp.arange(tp, dtype=jnp.int32))
    tabs_s = _rope_tables(past + jnp.arange(ts, dtype=jnp.int32))
    seg = jnp.arange(D_R, dtype=jnp.int32) // R_HEAD
    bd = (seg[:, None] == seg[None, :]).astype(BF16)
    eye = (jnp.arange(R_HEAD, dtype=jnp.int32)[:, None] == (jnp.arange(D_R, dtype=jnp.int32) % R_HEAD)[None, :]).astype(F32)
    consts = (bd, eye)
    w_in = _pad_w_in(w_in)
    xp, xs = x_prompt, x_sample
    st_p = ([], [], [], [], [])
    st_s = ([], [], [], [], [])
    for l in range(depth):
        final = l == depth - 1
        p = _layer_params(l, w_ada, w_in, w_q_b, w_uk, w_uv, w_mla_out, w_conv_out, rwkv_w2, rwkv_a2,
                          w_rwkv_out, w_out)
        vecs = dict(norm_g=norm_g[l], q_norm_g=q_norm_g[l], kv_norm_g=kv_norm_g[l], mu=rwkv_mu[l],
                    w0=rwkv_w0[l], a0=rwkv_a0[l], k_k=rwkv_k_k[l], k_a=rwkv_k_a[l],
                    r_k=rwkv_r_k[l].reshape(-1), conv_w=conv_w[l], gn_g=rwkv_gn_g[l], gn_b=rwkv_gn_b[l])
        mod_p = _ada_mod(c_prompt, p["w_ada"], b_ada[l])
        mod_s = _ada_mod(c_sample, p["w_ada"], b_ada[l])
        xp, ckv, kpe, cst, sst, rst = _layer(
            xp, mod_p, tabs_p, p, vecs,
            jnp.zeros((bp, CONV_W - 1, D_CONV), F32), jnp.zeros((bp, SHIFT_W), F32),
            jnp.zeros((bp, R_HEAD, D_R), F32), consts, final_norm_g, final, _attn_prompt, BF16)
        for lst, val in zip(st_p, (ckv, kpe, cst, sst, _state_from_lanes(rst))):
            lst.append(val)
        attend_s = functools.partial(_attn_sample_bound, page_table, cache_mla_ckv, cache_mla_kpe, l)
        xs, ckv, kpe, cst, sst, rst = _layer(
            xs, mod_s, tabs_s, p, vecs, state_conv[l], state_rwkv_shift[l],
            _state_to_lanes(state_rwkv[l]), consts, final_norm_g, final, attend_s, F32)
        for lst, val in zip(st_s, (ckv, kpe, cst, sst, _state_from_lanes(rst))):
            lst.append(val)
    return (xp, xs,
            jnp.stack(st_p[0]), jnp.stack(st_p[1]), jnp.stack(st_p[2]), jnp.stack(st_p[3]), jnp.stack(st_p[4]),
            jnp.stack(st_s[0]), jnp.stack(st_s[1]), jnp.stack(st_s[2]), jnp.stack(st_s[3]), jnp.stack(st_s[4]))


def _attn_sample_bound(page_table, cache_ckv, cache_kpe, layer, qlat, qpe, ckv_bf, kpe_bf, wuv_pad):
    return _attn_sample(page_table, qlat, qpe, ckv_bf, kpe_bf, wuv_pad, cache_ckv, cache_kpe, layer)
```

```python
import functools
import math

import jax
import jax.numpy as jnp
from jax import lax
from jax.experimental import pallas as pl
from jax.experimental.pallas import tpu as pltpu

F32 = jnp.float32
BF16 = jnp.bfloat16

D_MODEL = 1024
PAGE = 128
N_HEADS = 8
QK_NOPE = 64
QK_ROPE = 32
V_HEAD = 64
Q_LORA = 384
KV_LORA = 256
D_MLA = N_HEADS * V_HEAD
ROPE_THETA = 10000.0
ATTN_SCALE = (QK_NOPE + QK_ROPE) ** -0.5
Q_SCALE = ATTN_SCALE * math.log2(math.e)
D_CONV = 256
CONV_W = 3
R_HEADS = 4
R_HEAD = 64
D_R = R_HEADS * R_HEAD
W_LORA = 64
A_LORA = 64
SHIFT_W = 3 * D_R + W_LORA + A_LORA
GN_EPS = 64e-5
RMS_EPS = 1e-6
N_BRANCH = 3

LANE = 128
SUBLANE = 8
KR_W = LANE
SEG_W = (Q_LORA, KV_LORA, KR_W, D_MLA, 4 * D_CONV, SHIFT_W, D_R, N_BRANCH * D_MODEL)
SEG_OFF = tuple(sum(SEG_W[:i]) for i in range(len(SEG_W)))
PROJ_P = sum(SEG_W)
VMEM_LIMIT = 56 * 1024 * 1024
NEG_BIG = -1e30

ROW_TILE = 256
ATTN_TQ = 128
ATTN_TK = 256
ATTN_QT = 4
ATTN_RC = 64
PAGE_GROUP = 32
PAGE_PART = 8
PAGE_SLOTS = 4
SCAN_BATCH = 8
SCAN_CHUNK = 256
SCAN_UNROLL = 32
SCAN_GROUPS = 2


def _cparams(sem):
    return pltpu.CompilerParams(dimension_semantics=sem, vmem_limit_bytes=VMEM_LIMIT)


def _dot(a, b):
    return jnp.dot(a, b, preferred_element_type=F32)


def _dot_nt(a, b):
    return lax.dot_general(a, b, (((1,), (1,)), ((), ())), preferred_element_type=F32)


def _sigmoid(x):
    return 1.0 / (1.0 + jnp.exp(-x))


def _silu(x):
    return x * _sigmoid(x)


def _softplus(x):
    return jnp.maximum(x, 0.0) + jnp.log1p(jnp.exp(-jnp.abs(x)))


def _seg_sum(x, bd, split=True):
    hi = x.astype(BF16)
    if not split:
        return _dot(hi, bd)
    lo = (x - hi.astype(F32)).astype(BF16)
    return _dot(hi, bd) + _dot(lo, bd)


def _row_tiles(batch, seq, cap):
    if seq >= LANE:
        tt = min(seq, cap)
        assert seq % tt == 0
        return 1, tt
    assert seq % SUBLANE == 0
    return batch, seq


def _ada_kernel(c_ref, w_ref, b_ref, o_ref):
    s = _silu(c_ref[...])
    o_ref[...] = _dot(s.astype(BF16), w_ref[...]) + b_ref[...]


def _ada_mod(c, w_ada_bf, b_ada):
    bsz = c.shape[0]
    out = pl.pallas_call(
        _ada_kernel,
        out_shape=jax.ShapeDtypeStruct((bsz, 3 * D_MODEL), F32),
        compiler_params=pltpu.CompilerParams(vmem_limit_bytes=VMEM_LIMIT),
        name="ada_mod",
    )(c, w_ada_bf, b_ada.reshape(1, -1))
    return out.reshape(bsz, 1, 3 * D_MODEL)


N_MLA_IN, N_RWKV_IN = 10, 10
N_GATE_OUT = 5


def _front_kernel(x_ref, mod_ref, g_ref, w_ref, *refs):
    nb, tt, d = x_ref.shape
    mla_in = refs[:N_MLA_IN]
    rwkv_in = refs[N_MLA_IN:N_MLA_IN + N_RWKV_IN]
    outs = refs[N_MLA_IN + N_RWKV_IN:]
    zmla_out, cbz_out, ccx_out, zrw_out, gm_out = outs[:N_GATE_OUT]
    mla_out = outs[N_GATE_OUT:N_GATE_OUT + 6]
    rwkv_out = outs[N_GATE_OUT + 6:N_GATE_OUT + 15]
    buf = outs[N_GATE_OUT + 15]
    x = x_ref[...]
    ms = jnp.mean(x * x, axis=-1, keepdims=True)
    xn = x * lax.rsqrt(ms + RMS_EPS) * g_ref[...]
    shift = mod_ref[:, :, 0:D_MODEL]
    scale = mod_ref[:, :, D_MODEL:2 * D_MODEL]
    u = (xn * (1.0 + scale) + shift).reshape(nb * tt, d).astype(BF16)

    def seg(i):
        return _dot(u, w_ref[:, SEG_OFF[i]:SEG_OFF[i] + SEG_W[i]])

    gm_out[...] = _sigmoid(seg(7)).reshape(nb, tt, SEG_W[7]).astype(gm_out.dtype)
    _rwkv_prep_body(seg(5).reshape(nb, tt, SHIFT_W), *rwkv_in, *rwkv_out, buf)
    zmla_out[...] = _silu(seg(3)).reshape(nb, tt, SEG_W[3]).astype(zmla_out.dtype)
    c4 = seg(4)
    cbz_out[...] = (c4[:, 0:D_CONV] * _silu(c4[:, 3 * D_CONV:])).reshape(nb, tt, D_CONV)
    ccx_out[...] = (c4[:, D_CONV:2 * D_CONV] * c4[:, 2 * D_CONV:3 * D_CONV]).reshape(nb, tt, D_CONV)
    zrw_out[...] = _silu(seg(6)).reshape(nb, tt, SEG_W[6]).astype(zrw_out.dtype)
    _mla_prep_body(seg(0), seg(1).reshape(nb, tt, KV_LORA), seg(2).reshape(nb, tt, KR_W), nb, tt,
                   *mla_in, *mla_out)


def _front(x, mod, norm_g, w_in_p, tabs, q_norm_g, kv_norm_g, wn, wr, wrs, uk, q_dtype,
           shift_prev, mu, w0, w2p, a0, a2p, k_k, k_a, r_k, bd):
    bsz, seq, d = x.shape
    nb, tt = _row_tiles(bsz, seq, ROW_TILE)
    grid = (bsz // nb, seq // tt)
    cos_q, sin_q, cos_k, sin_k = tabs
    row = lambda w: pl.BlockSpec((nb, tt, w), lambda b, t: (b, t, 0))
    tab = lambda w: pl.BlockSpec((tt, w), lambda b, t: (t, 0))
    full = lambda a: pl.BlockSpec(a.shape, lambda b, t: (0,) * a.ndim)
    hq = lambda w: pl.BlockSpec((nb, N_HEADS, tt, w), lambda b, t: (b, 0, t, 0))
    state = pl.BlockSpec((nb, 1, SHIFT_W), lambda b, t: (b, 0, 0))
    mla_consts = [q_norm_g.reshape(1, -1), kv_norm_g.reshape(1, -1), wn, wr, wrs, uk]
    rwkv_consts = [mu.reshape(1, -1), w0.reshape(1, -1), w2p, a0.reshape(1, -1), a2p,
                   k_k.reshape(1, -1), k_a.reshape(1, -1), r_k.reshape(1, -1), bd]
    assert 4 + len(mla_consts) == N_MLA_IN and len(rwkv_consts) + 1 == N_RWKV_IN
    g = norm_g.reshape(1, d)
    outs = pl.pallas_call(
        _front_kernel,
        grid=grid,
        in_specs=[row(d), pl.BlockSpec((nb, 1, 3 * d), lambda b, t: (b, 0, 0)), full(g), full(w_in_p),
                  tab(N_HEADS * QK_ROPE), tab(N_HEADS * QK_ROPE), tab(QK_ROPE), tab(QK_ROPE)]
                 + [full(a) for a in mla_consts] + [full(a) for a in rwkv_consts[:-1]] + [full(bd), state],
        out_specs=[row(SEG_W[3]), row(D_CONV), row(D_CONV), row(SEG_W[6]), row(SEG_W[7]),
                   hq(KV_LORA), hq(QK_ROPE), row(KV_LORA), row(QK_ROPE), row(KV_LORA), row(QK_ROPE)]
                  + [row(D_R)] * 8 + [state],
        out_shape=[jax.ShapeDtypeStruct((bsz, seq, SEG_W[3]), BF16),
                   jax.ShapeDtypeStruct((bsz, seq, D_CONV), F32),
                   jax.ShapeDtypeStruct((bsz, seq, D_CONV), F32),
                   jax.ShapeDtypeStruct((bsz, seq, SEG_W[6]), BF16),
                   jax.ShapeDtypeStruct((bsz, seq, SEG_W[7]), BF16)]
                  + [jax.ShapeDtypeStruct((bsz, N_HEADS, seq, KV_LORA), q_dtype),
                     jax.ShapeDtypeStruct((bsz, N_HEADS, seq, QK_ROPE), q_dtype),
                     jax.ShapeDtypeStruct((bsz, seq, KV_LORA), F32),
                     jax.ShapeDtypeStruct((bsz, seq, QK_ROPE), F32),
                     jax.ShapeDtypeStruct((bsz, seq, KV_LORA), BF16),
                     jax.ShapeDtypeStruct((bsz, seq, QK_ROPE), BF16)]
                  + [jax.ShapeDtypeStruct((bsz, seq, D_R), F32)] * 8
                  + [jax.ShapeDtypeStruct((bsz, 1, SHIFT_W), F32)],
        scratch_shapes=[pltpu.VMEM((nb, tt + SUBLANE, SHIFT_W), F32)],
        compiler_params=_cparams(("parallel", "arbitrary")),
        name="front",
    )(x, mod, g, w_in_p, cos_q, sin_q, cos_k, sin_k, *mla_consts, *rwkv_consts,
      shift_prev.reshape(bsz, 1, SHIFT_W))
    return outs


def _mla_prep_body(qa, kva, kr, nb, tt, cq_ref, sq_ref, ck_ref, sk_ref,
                   gq_ref, gkv_ref, wn_ref, wr_ref, wrs_ref, uk_ref,
                   qlat_ref, qpe_ref, ckv_ref, kpe_ref, ckvb_ref, kpeb_ref):
    rows = nb * tt
    cq = qa * lax.rsqrt(jnp.mean(qa * qa, axis=-1, keepdims=True) + RMS_EPS) * gq_ref[...]
    cqb = cq.astype(BF16)
    qn = _dot(cqb, wn_ref[...])
    cos_q = jnp.broadcast_to(cq_ref[...][None], (nb, tt, N_HEADS * QK_ROPE)).reshape(rows, -1)
    sin_q = jnp.broadcast_to(sq_ref[...][None], (nb, tt, N_HEADS * QK_ROPE)).reshape(rows, -1)
    qp = (_dot(cqb, wr_ref[...]) * cos_q + _dot(cqb, wrs_ref[...]) * sin_q) * Q_SCALE
    for h in range(N_HEADS):
        qn_h = qn[:, h * LANE:(h + 1) * LANE].astype(BF16)
        ql = _dot(qn_h, uk_ref[h]) * Q_SCALE
        qlat_ref[:, h] = ql.reshape(nb, tt, KV_LORA).astype(qlat_ref.dtype)
        qpe_ref[:, h] = qp[:, h * QK_ROPE:(h + 1) * QK_ROPE].reshape(nb, tt, QK_ROPE).astype(qpe_ref.dtype)
    ckv = kva * lax.rsqrt(jnp.mean(kva * kva, axis=-1, keepdims=True) + RMS_EPS) * gkv_ref[...]
    ckv_ref[...] = ckv
    ckvb_ref[...] = ckv.astype(BF16)
    kpe = kr[:, :, 0:QK_ROPE] * ck_ref[...][None] + kr[:, :, QK_ROPE:2 * QK_ROPE] * sk_ref[...][None]
    kpe_ref[...] = kpe
    kpeb_ref[...] = kpe.astype(BF16)


def _value_up_proj(o_lat, wuv_ref, t):
    tiles = []
    for j in range(N_HEADS // 2):
        h0, h1 = 2 * j, 2 * j + 1
        tiles.append(_dot(o_lat[h0 * t:(h0 + 1) * t].astype(BF16), wuv_ref[h0])
                     + _dot(o_lat[h1 * t:(h1 + 1) * t].astype(BF16), wuv_ref[h1]))
    return jnp.concatenate(tiles, axis=1)


def _softmax_update(s, m_old, l_old):
    m_new = jnp.maximum(m_old, jnp.max(s, axis=-1, keepdims=True))
    alpha = jnp.exp2(m_old - m_new)
    p = jnp.exp2(s - jnp.tile(m_new, (1, s.shape[1] // LANE)))
    l_new = alpha * l_old + jnp.sum(p, axis=-1, keepdims=True)
    return p, m_new, l_new, alpha


def _attn_prompt_kernel(ql_ref, qp_ref, ckv_ref, kpe_ref, wuv_ref, o_ref,
                        m_all, l_all, acc_all, s_all, p_all, *, tq, tk, rc, qt):
    for j in range(qt):
        _attn_prompt_tile(pl.program_id(1) * qt + j,
                          ql_ref.at[0, :, j * tq:(j + 1) * tq, :], qp_ref.at[0, :, j * tq:(j + 1) * tq, :],
                          ckv_ref, kpe_ref, wuv_ref, o_ref.at[0, j * tq:(j + 1) * tq, :],
                          m_all.at[j], l_all.at[j], acc_all.at[j], s_all.at[j], p_all.at[j],
                          tq=tq, tk=tk, rc=rc)


def _attn_prompt_tile(qi, ql_ref, qp_ref, ckv_ref, kpe_ref, wuv_ref, o_ref,
                      m_scr, l_scr, acc_scr, s_scr, p_scr, *, tq, tk, rc):
    rows = N_HEADS * tq
    ql = ql_ref[...].reshape(rows, KV_LORA)
    qp = qp_ref[...].reshape(rows, QK_ROPE)
    m_scr[...] = jnp.full(m_scr.shape, NEG_BIG, F32)
    l_scr[...] = jnp.zeros(l_scr.shape, F32)
    acc_scr[...] = jnp.zeros(acc_scr.shape, F32)
    n_kt = (qi * tq + tq + tk - 1) // tk

    def scores(kt, slot):
        start = pl.multiple_of(kt * tk, tk)
        s_scr[slot] = (_dot_nt(ql, ckv_ref[0, pl.ds(start, tk), :])
                       + _dot_nt(qp, kpe_ref[0, pl.ds(start, tk), :]))

    def softmax(kt, slot, masked):
        for c in range(rows // rc):
            rs = slice(c * rc, (c + 1) * rc)
            s = s_scr[slot, rs, :]
            if masked:
                q_pos = qi * tq + (c * rc) % tq + lax.broadcasted_iota(jnp.int32, (rc, tk), 0)
                k_pos = kt * tk + lax.broadcasted_iota(jnp.int32, (rc, tk), 1)
                s = jnp.where(k_pos <= q_pos, s, NEG_BIG)
            p, m_new, l_new, alpha = _softmax_update(s, m_scr[rs, :], l_scr[rs, :])
            m_scr[rs, :] = m_new
            l_scr[rs, :] = l_new
            p_scr[rs, :] = p.astype(BF16)
            acc_scr[rs, :] = jnp.tile(alpha, (1, KV_LORA // LANE)) * acc_scr[rs, :]

    def weighted_values(kt):
        start = pl.multiple_of(kt * tk, tk)
        acc_scr[...] += _dot(p_scr[...], ckv_ref[0, pl.ds(start, tk), :])

    last = n_kt - 1
    scores(last, 0)
    softmax(last, 0, True)
    scores(0, 1)
    weighted_values(last)

    def body(kt, carry):
        slot = (kt + 1) % 2
        softmax(kt, slot, False)
        scores(kt + 1, 1 - slot)
        weighted_values(kt)
        return carry

    lax.fori_loop(0, last - 1, body, 0)

    @pl.when(last >= 1)
    def _():
        softmax(last - 1, last % 2, False)
        weighted_values(last - 1)
    o_lat = acc_scr[...] / jnp.tile(l_scr[...], (1, KV_LORA // LANE))
    o_ref[...] = _value_up_proj(o_lat, wuv_ref, tq).astype(o_ref.dtype)


def _attn_prompt(qlat, qpe, ckv_bf, kpe_bf, wuv_pad):
    bsz, _, seq, _ = qlat.shape
    tq = ATTN_TQ
    tk = min(ATTN_TK, seq)
    qt = ATTN_QT if seq % (ATTN_QT * tq) == 0 else 1
    assert tk % tq == 0 and seq % tk == 0
    rc = ATTN_RC
    kern = functools.partial(_attn_prompt_kernel, tq=tq, tk=tk, rc=rc, qt=qt)
    rows = N_HEADS * tq
    return pl.pallas_call(
        kern,
        grid=(bsz, seq // (qt * tq)),
        in_specs=[pl.BlockSpec((1, N_HEADS, qt * tq, KV_LORA), lambda b, q: (b, 0, q, 0)),
                  pl.BlockSpec((1, N_HEADS, qt * tq, QK_ROPE), lambda b, q: (b, 0, q, 0)),
                  pl.BlockSpec((1, seq, KV_LORA), lambda b, q: (b, 0, 0)),
                  pl.BlockSpec((1, seq, QK_ROPE), lambda b, q: (b, 0, 0)),
                  pl.BlockSpec(wuv_pad.shape, lambda b, q: (0, 0, 0))],
        out_specs=pl.BlockSpec((1, qt * tq, D_MLA), lambda b, q: (b, q, 0)),
        out_shape=jax.ShapeDtypeStruct((bsz, seq, D_MLA), BF16),
        scratch_shapes=[pltpu.VMEM((qt, rows, LANE), F32), pltpu.VMEM((qt, rows, LANE), F32),
                        pltpu.VMEM((qt, rows, KV_LORA), F32),
                        pltpu.VMEM((qt, 2, rows, tk), F32), pltpu.VMEM((qt, rows, tk), BF16)],
        compiler_params=_cparams(("parallel", "arbitrary")),
        name="attn_prompt",
    )(qlat, qpe, ckv_bf, kpe_bf, wuv_pad)


def _attn_sample_kernel(pt_ref, ql_ref, qp_ref, ckvn_ref, kpen_ref, wuv_ref, cache_ckv, cache_kpe_t,
                        o_ref, ckv_buf, kpe_buf, sem, *, layer, pp, cp, n_steps, ts):
    ahead = PAGE_SLOTS - 1
    b = pl.program_id(0)
    nb = pl.num_programs(0)
    rows = N_HEADS * ts
    total = nb * n_steps
    ql = ql_ref[0].reshape(rows, KV_LORA).astype(BF16)
    qp = qp_ref[0].reshape(rows, QK_ROPE).astype(BF16)

    def page_copies(seq, grp, slot):
        out = []
        for i in range(pp):
            page = pt_ref[seq, grp * pp + i]
            out.append(pltpu.make_async_copy(cache_ckv.at[layer, page], ckv_buf.at[slot, i], sem.at[slot]))
            out.append(pltpu.make_async_copy(cache_kpe_t.at[layer, page], kpe_buf.at[slot, i], sem.at[slot]))
        return out

    def start_group(g):
        slot = lax.rem(g, PAGE_SLOTS)
        g = jnp.minimum(g, total - 1)
        seq = lax.shift_right_logical(g, n_steps.bit_length() - 1)
        for c in page_copies(seq, jnp.bitwise_and(g, n_steps - 1), slot):
            c.start()

    def wait_group(g):
        for c in page_copies(0, 0, lax.rem(g, PAGE_SLOTS)):
            c.wait()

    @pl.when(b == 0)
    def _():
        for g in range(ahead):
            start_group(jnp.int32(g))

    def partial_softmax(s, vals):
        m = jnp.max(s, axis=-1, keepdims=True)
        p = jnp.exp2(s - m)
        return m, jnp.sum(p, axis=-1, keepdims=True), _dot(p.astype(BF16), vals)

    def merge(parts):
        m = functools.reduce(jnp.maximum, [pm for pm, _, _ in parts])
        l = sum(pl_ * jnp.exp2(pm - m) for pm, pl_, _ in parts)
        acc = sum(pa * jnp.exp2(pm - m) for pm, _, pa in parts)
        return m, l, acc

    def group_parts(slot):
        n_c = pp // cp
        cks = [ckv_buf[slot, c * cp:(c + 1) * cp].reshape(cp * PAGE, KV_LORA).astype(BF16) for c in range(n_c)]
        kps = [jnp.concatenate([kpe_buf[slot, c * cp + i].astype(BF16) for i in range(cp)], axis=1)
               for c in range(n_c)]
        ss = [_dot_nt(ql, ck) + _dot(qp, kp_t) for ck, kp_t in zip(cks, kps)]
        ms = [jnp.max(s, axis=-1, keepdims=True) for s in ss]
        ps = [jnp.exp2(s - m) for s, m in zip(ss, ms)]
        ls = [jnp.sum(p, axis=-1, keepdims=True) for p in ps]
        accs = [_dot(p.astype(BF16), ck) for p, ck in zip(ps, cks)]
        return list(zip(ms, ls, accs))

    ckn = ckvn_ref[0]
    kpn = kpen_ref[0]
    s_new = _dot_nt(ql, ckn) + _dot_nt(qp, kpn)
    t_q = lax.broadcasted_iota(jnp.int32, (rows, ts), 0) % ts
    t_k = lax.broadcasted_iota(jnp.int32, (rows, ts), 1)
    state = partial_softmax(jnp.where(t_k <= t_q, s_new, NEG_BIG), ckn)

    for st in range(n_steps):
        g = b * n_steps + st
        start_group(g + ahead)
        wait_group(g)
        state = merge([state] + group_parts(lax.rem(g, PAGE_SLOTS)))

    @pl.when(b == nb - 1)
    def _():
        for g in range(ahead):
            wait_group(total + g)

    _, l, acc = state
    o_ref[0] = _value_up_proj(acc / l, wuv_ref, ts).astype(o_ref.dtype)


def _attn_sample(page_table, qlat, qpe, ckv_new_bf, kpe_new_bf, wuv_pad, cache_ckv, cache_kpe, layer):
    bsz, _, ts, _ = qlat.shape
    n_pages = page_table.shape[1]
    pp = min(PAGE_GROUP, n_pages // 2)
    cp = min(PAGE_PART, pp)
    n_steps = n_pages // pp
    assert n_pages % pp == 0 and pp % cp == 0
    assert n_steps & (n_steps - 1) == 0
    kern = functools.partial(_attn_sample_kernel, layer=layer, pp=pp, cp=cp, n_steps=n_steps, ts=ts)

    cache_kpe_t = jnp.swapaxes(cache_kpe, 2, 3)

    grid_spec = pltpu.PrefetchScalarGridSpec(
        num_scalar_prefetch=1,
        grid=(bsz,),
        in_specs=[pl.BlockSpec((1, N_HEADS, ts, KV_LORA), lambda b, pt: (b, 0, 0, 0)),
                  pl.BlockSpec((1, N_HEADS, ts, QK_ROPE), lambda b, pt: (b, 0, 0, 0)),
                  pl.BlockSpec((1, ts, KV_LORA), lambda b, pt: (b, 0, 0)),
                  pl.BlockSpec((1, ts, QK_ROPE), lambda b, pt: (b, 0, 0)),
                  pl.BlockSpec(wuv_pad.shape, lambda b, pt: (0, 0, 0)),
                  pl.BlockSpec(memory_space=pl.ANY),
                  pl.BlockSpec(memory_space=pl.ANY)],
        out_specs=pl.BlockSpec((1, ts, D_MLA), lambda b, pt: (b, 0, 0)),
        scratch_shapes=[pltpu.VMEM((PAGE_SLOTS, pp, PAGE, KV_LORA), F32),
                        pltpu.VMEM((PAGE_SLOTS, pp, QK_ROPE, PAGE), F32),
                        pltpu.SemaphoreType.DMA((PAGE_SLOTS,))],
    )
    return pl.pallas_call(
        kern,
        grid_spec=grid_spec,
        out_shape=jax.ShapeDtypeStruct((bsz, ts, D_MLA), BF16),
        compiler_params=_cparams(("arbitrary",)),
        name="attn_sample",
    )(page_table, qlat, qpe, ckv_new_bf, kpe_new_bf, wuv_pad, cache_ckv, cache_kpe_t)


def _rwkv_prep_body(rw, mu_ref, w0_ref, w2_ref, a0_ref, a2_ref, kk_ref, ka_ref, rk_ref, bd_ref, sprev_ref,
                    q_out, w_out, k_out, v_out, kk_out, b_out, vkr_out, bonus_out, shift_out, buf):
    nb, tt, _ = rw.shape
    ti = pl.program_id(1)
    rows = nb * tt

    @pl.when(ti == 0)
    def _():
        buf[:, SUBLANE - 1:SUBLANE, :] = sprev_ref[...]

    @pl.when(ti > 0)
    def _():
        buf[:, SUBLANE - 1:SUBLANE, :] = buf[:, tt + SUBLANE - 1:tt + SUBLANE, :]

    buf[:, SUBLANE:, :] = rw
    shift_out[...] = rw[:, tt - 1:tt, :]
    rw_prev = buf[:, SUBLANE - 1:SUBLANE - 1 + tt, :]
    rws = (rw + mu_ref[...] * (rw_prev - rw)).reshape(rows, SHIFT_W)
    r = rws[:, 0:D_R]
    k = rws[:, D_R:2 * D_R]
    v = rws[:, 2 * D_R:3 * D_R]
    wa = rws[:, 3 * D_R:]
    w_log = -_softplus(-(w0_ref[...] + _dot(jnp.tanh(wa).astype(BF16), w2_ref[...]))) - 0.5
    decay = jnp.exp(-jnp.exp(w_log))
    a = _sigmoid(a0_ref[...] + _dot(wa.astype(BF16), a2_ref[...]))
    bd = bd_ref[...]
    kk = k * kk_ref[...]
    kk = kk / jnp.maximum(jnp.sqrt(_seg_sum(kk * kk, bd)), 1e-12)
    k = k * (1.0 + (a - 1.0) * ka_ref[...])
    bonus = _seg_sum(r * k * rk_ref[...], bd, split=False) * v
    b = kk * a
    q = decay * r - kk * _seg_sum(b * r, bd, split=False)
    vkr = v * _seg_sum(k * r, bd, split=False)
    shp = (nb, tt, D_R)
    q_out[...] = q.reshape(shp)
    w_out[...] = decay.reshape(shp)
    k_out[...] = k.reshape(shp)
    v_out[...] = v.reshape(shp)
    kk_out[...] = kk.reshape(shp)
    b_out[...] = b.reshape(shp)
    vkr_out[...] = vkr.reshape(shp)
    bonus_out[...] = bonus.reshape(shp)


def _rwkv_scan_kernel(q_ref, w_ref, k_ref, v_ref, kk_ref, b_ref, vkr_ref, s0_ref, bd_ref, eye_ref,
                      y_ref, sT_ref, s_scr):
    nb, tc, _ = q_ref.shape
    ci = pl.program_id(1)
    rows = nb * R_HEAD

    @pl.when(ci == 0)
    def _():
        s_scr[...] = s0_ref[...]

    bd = bd_ref[...]
    eye = eye_ref[...][None]
    eye_bf = eye.astype(BF16)

    gb = max(nb // SCAN_GROUPS, 1)

    def seg(x):
        return _dot(x.reshape(gb * R_HEAD, D_R), bd).reshape(gb, R_HEAD, D_R)

    def step(t, carry):
        for g0 in range(0, nb, gb):
            grp = slice(g0, g0 + gb)
            row = lambda ref: ref[grp, pl.ds(t, 1), :]
            s = s_scr[grp]
            s_bf = s.astype(BF16)
            sa = seg(s_bf * row(kk_ref).astype(BF16))
            y_col = seg(s_bf * row(q_ref).astype(BF16))
            v_col = seg(eye_bf * row(v_ref).astype(BF16))
            s_scr[grp] = s * row(w_ref) - sa * row(b_ref) + v_col * row(k_ref)
            y_ref[grp, pl.ds(t, 1), :] = jnp.sum(y_col * eye, axis=1, keepdims=True) + row(vkr_ref)
        return carry

    lax.fori_loop(0, tc, step, 0, unroll=SCAN_UNROLL)

    @pl.when(ci == pl.num_programs(1) - 1)
    def _():
        sT_ref[...] = s_scr[...]


def _rwkv_scan(q, w, k, v, kk, b, vkr, s0, bd, eye):
    bsz, seq, _ = q.shape
    nb = math.gcd(bsz, SCAN_BATCH)
    tc = min(seq, SCAN_CHUNK)
    row = pl.BlockSpec((nb, tc, D_R), lambda bi, c: (bi, c, 0))
    st = pl.BlockSpec((nb, R_HEAD, D_R), lambda bi, c: (bi, 0, 0))
    return pl.pallas_call(
        _rwkv_scan_kernel,
        grid=(bsz // nb, seq // tc),
        in_specs=[row] * 7 + [st, pl.BlockSpec(bd.shape, lambda bi, c: (0, 0)),
                              pl.BlockSpec(eye.shape, lambda bi, c: (0, 0))],
        out_specs=[row, st],
        out_shape=[jax.ShapeDtypeStruct((bsz, seq, D_R), F32),
                   jax.ShapeDtypeStruct((bsz, R_HEAD, D_R), F32)],
        scratch_shapes=[pltpu.VMEM((nb, R_HEAD, D_R), F32)],
        compiler_params=_cparams(("parallel", "arbitrary")),
        name="rwkv_scan",
    )(q, w, k, v, kk, b, vkr, s0, bd, eye)


def _out_kernel(x_ref, mod_ref, omla_ref, szmla_ref, cbz_ref, ccx_ref, cprev_ref, szrw_ref,
                yr_ref, bonus_ref, gm_ref, wmla_ref, wconv_ref, wrw_ref, wout_ref, cw_ref,
                gng_ref, gnb_ref, bd_ref, fg_ref, xo_ref, cstate_ref, buf, *, final):
    nb, tt, d = x_ref.shape
    ti = pl.program_id(1)
    rows = nb * tt

    @pl.when(ti == 0)
    def _():
        buf[:, SUBLANE - 2:SUBLANE, :] = cprev_ref[...]

    @pl.when(ti > 0)
    def _():
        buf[:, SUBLANE - 2:SUBLANE, :] = buf[:, tt + SUBLANE - 2:tt + SUBLANE, :]

    buf[:, SUBLANE:, :] = ccx_ref[...]
    cstate_ref[...] = buf[:, tt + SUBLANE - 2:tt + SUBLANE, :]
    conv = (buf[:, SUBLANE - 2:SUBLANE - 2 + tt, :] * cw_ref[0:1, :]
            + buf[:, SUBLANE - 1:SUBLANE - 1 + tt, :] * cw_ref[1:2, :]
            + buf[:, SUBLANE:, :] * cw_ref[2:3, :])
    y_conv = _dot((cbz_ref[...] * conv).reshape(rows, D_CONV).astype(BF16), wconv_ref[...])

    y_mla = _dot((omla_ref[...].astype(F32) * szmla_ref[...].astype(F32)).reshape(rows, D_MLA).astype(BF16),
                 wmla_ref[...])

    bd = bd_ref[...]
    yr = yr_ref[...].reshape(rows, D_R)
    mu = _seg_sum(yr, bd) * (1.0 / R_HEAD)
    dy = yr - mu
    var = _seg_sum(dy * dy, bd) * (1.0 / R_HEAD)
    yn = dy * lax.rsqrt(var + GN_EPS) * gng_ref[...] + gnb_ref[...]
    o_rw = yn + bonus_ref[...].reshape(rows, D_R)
    y_rw = _dot((o_rw * szrw_ref[...].astype(F32).reshape(rows, D_R)).astype(BF16), wrw_ref[...])

    g = gm_ref[...].astype(F32).reshape(rows, N_BRANCH * d)
    merged = g[:, 0:d] * y_mla + g[:, d:2 * d] * y_conv + g[:, 2 * d:] * y_rw
    delta = _dot(merged.astype(BF16), wout_ref[...]).reshape(nb, tt, d)
    xo = x_ref[...] + mod_ref[:, :, 2 * d:] * delta
    if final:
        xo = xo * lax.rsqrt(jnp.mean(xo * xo, axis=-1, keepdims=True) + RMS_EPS) * fg_ref[...]
    xo_ref[...] = xo


def _out_proj(x, mod, o_mla, sz_mla, cbz, ccx, conv_prev, sz_rw, y_r, bonus, gm,
              wmla, wconv, wrw, wout, conv_w, gn_g, gn_b, bd, final_g, final):
    bsz, seq, d = x.shape
    nb, tt = _row_tiles(bsz, seq, ROW_TILE)
    grid = (bsz // nb, seq // tt)
    row = lambda w: pl.BlockSpec((nb, tt, w), lambda b, t: (b, t, 0))
    full = lambda a: pl.BlockSpec(a.shape, lambda b, t: (0,) * a.ndim)
    consts = [wmla, wconv, wrw, wout, conv_w, gn_g.reshape(1, -1), gn_b.reshape(1, -1), bd,
              final_g.reshape(1, -1)]
    return pl.pallas_call(
        functools.partial(_out_kernel, final=final),
        grid=grid,
        in_specs=[row(d), pl.BlockSpec((nb, 1, 3 * d), lambda b, t: (b, 0, 0)),
                  row(D_MLA), row(D_MLA), row(D_CONV), row(D_CONV),
                  pl.BlockSpec((nb, CONV_W - 1, D_CONV), lambda b, t: (b, 0, 0)),
                  row(D_R), row(D_R), row(D_R), row(N_BRANCH * d)] + [full(a) for a in consts],
        out_specs=[row(d), pl.BlockSpec((nb, CONV_W - 1, D_CONV), lambda b, t: (b, 0, 0))],
        out_shape=[jax.ShapeDtypeStruct((bsz, seq, d), F32),
                   jax.ShapeDtypeStruct((bsz, CONV_W - 1, D_CONV), F32)],
        scratch_shapes=[pltpu.VMEM((nb, tt + SUBLANE, D_CONV), F32)],
        compiler_params=_cparams(("parallel", "arbitrary")),
        name="out_proj",
    )(x, mod, o_mla, sz_mla, cbz, ccx, conv_prev, sz_rw, y_r, bonus, gm, *consts)


def _swap_halves(w):
    half = QK_ROPE // 2
    return jnp.concatenate([w[..., half:], w[..., :half]], axis=-1)


def _pad_w_in(w_in):
    a = Q_LORA + KV_LORA
    k_rope = w_in[:, :, a:a + QK_ROPE]
    tail = w_in.shape[2] - a - QK_ROPE

    def place(piece, at):
        return jnp.pad(piece, ((0, 0), (0, 0), (at, PROJ_P - at - piece.shape[2])))

    return (place(w_in[:, :, :a], 0) + place(k_rope, a) + place(_swap_halves(k_rope), a + QK_ROPE)
            + place(w_in[:, :, a + QK_ROPE:], PROJ_P - tail)).astype(BF16)


def _layer_params(l, w_ada, w_in_p, w_q_b, w_uk, w_uv, w_mla_out, w_conv_out, rwkv_w2, rwkv_a2,
                  w_rwkv_out, w_out):
    w_in_p = w_in_p[l]
    wq = w_q_b[l].reshape(Q_LORA, N_HEADS, QK_NOPE + QK_ROPE)
    wn = jnp.pad(wq[:, :, :QK_NOPE], ((0, 0), (0, 0), (0, LANE - QK_NOPE))).reshape(Q_LORA, N_HEADS * LANE)
    wr = wq[:, :, QK_NOPE:]
    wrs = _swap_halves(wr)
    uk = jnp.pad(jnp.transpose(w_uk[l], (1, 2, 0)), ((0, 0), (0, LANE - QK_NOPE), (0, 0)))
    wuv = jnp.transpose(w_uv[l], (1, 0, 2))
    zv = jnp.zeros_like(wuv)
    even = (jnp.arange(N_HEADS) % 2 == 0)[:, None, None]
    wuv_pad = jnp.where(even, jnp.concatenate([wuv, zv], axis=2), jnp.concatenate([zv, wuv], axis=2))
    zeros = jnp.zeros((W_LORA, D_R), F32)
    return dict(
        w_ada=w_ada[l].astype(BF16), w_in_p=w_in_p,
        wn=wn.astype(BF16), wr=wr.reshape(Q_LORA, -1).astype(BF16), wrs=wrs.reshape(Q_LORA, -1).astype(BF16),
        uk=uk.astype(BF16), wuv_pad=wuv_pad.astype(BF16),
        wmla=w_mla_out[l].astype(BF16), wconv=w_conv_out[l].astype(BF16),
        wrw=w_rwkv_out[l].astype(BF16), wout=w_out[l].astype(BF16),
        w2p=jnp.concatenate([rwkv_w2[l], zeros], axis=0).astype(BF16),
        a2p=jnp.concatenate([zeros, rwkv_a2[l]], axis=0).astype(BF16),
    )


def _rope_tables(pos):
    half = QK_ROPE // 2
    inv = ROPE_THETA ** (-jnp.arange(half, dtype=F32) / half)
    ang = pos.astype(F32)[:, None] * inv[None, :]
    cos, sin = jnp.cos(ang), jnp.sin(ang)
    cos_k = jnp.concatenate([cos, cos], axis=1)
    sin_k = jnp.concatenate([-sin, sin], axis=1)
    return jnp.tile(cos_k, (1, N_HEADS)), jnp.tile(sin_k, (1, N_HEADS)), cos_k, sin_k


def _state_to_lanes(s):
    b = s.shape[0]
    return jnp.transpose(s, (0, 2, 1, 3)).reshape(b, R_HEAD, D_R)


def _state_from_lanes(s):
    b = s.shape[0]
    return jnp.transpose(s.reshape(b, R_HEAD, R_HEADS, R_HEAD), (0, 2, 1, 3))


def _layer(x, c_mod, tabs, p, vecs, conv_prev, shift_prev, s0, consts, final_g, final, attend, q_dtype):
    bd, eye = consts
    (sz_mla, cbz, ccx, sz_rw, gm, qlat, qpe, ckv, kpe, ckv_bf, kpe_bf,
     q, w, k, v, kk, b, vkr, bonus, shift_state) = _front(
        x, c_mod, vecs["norm_g"], p["w_in_p"], tabs, vecs["q_norm_g"], vecs["kv_norm_g"],
        p["wn"], p["wr"], p["wrs"], p["uk"], q_dtype,
        shift_prev, vecs["mu"], vecs["w0"], p["w2p"], vecs["a0"], p["a2p"],
        vecs["k_k"], vecs["k_a"], vecs["r_k"], bd)
    o_mla = attend(qlat, qpe, ckv_bf, kpe_bf, p["wuv_pad"])
    y_r, s_new = _rwkv_scan(q, w, k, v, kk, b, vkr, s0, bd, eye)
    x_new, conv_state = _out_proj(
        x, c_mod, o_mla, sz_mla, cbz, ccx, conv_prev, sz_rw, y_r, bonus, gm,
        p["wmla"], p["wconv"], p["wrw"], p["wout"], vecs["conv_w"], vecs["gn_g"], vecs["gn_b"], bd,
        final_g, final)
    return x_new, ckv, kpe, conv_state, shift_state[:, 0], s_new


def kernel(x_prompt, x_sample, cache_mla_ckv, cache_mla_kpe, state_conv, state_rwkv_shift, state_rwkv, page_table, c_prompt, c_sample, norm_g, w_ada, b_ada, w_in, q_norm_g, w_q_b, kv_norm_g, w_uk, w_uv, w_mla_out, conv_w, w_conv_out, rwkv_mu, rwkv_w0, rwkv_w2, rwkv_a0, rwkv_a2, rwkv_k_k, rwkv_k_a, rwkv_r_k, rwkv_gn_g, rwkv_gn_b, w_rwkv_out, w_out, final_norm_g):
    depth = norm_g.shape[0]
    bp, tp, _ = x_prompt.shape
    bs, ts, _ = x_sample.shape
    past = page_table.shape[1] * PAGE
    tabs_p = _rope_tables(jnp.arange(tp, dtype=jnp.int32))
    tabs_s = _rope_tables(past + jnp.arange(ts, dtype=jnp.int32))
    seg = jnp.arange(D_R, dtype=jnp.int32) // R_HEAD
    bd = (seg[:, None] == seg[None, :]).astype(BF16)
    eye = (jnp.arange(R_HEAD, dtype=jnp.int32)[:, None] == (jnp.arange(D_R, dtype=jnp.int32) % R_HEAD)[None, :]).astype(F32)
    consts = (bd, eye)
    w_in = _pad_w_in(w_in)
    xp, xs = x_prompt, x_sample
    st_p = ([], [], [], [], [])
    st_s = ([], [], [], [], [])
    c_all = jnp.concatenate([c_prompt, c_sample], axis=0)
    for l in range(depth):
        final = l == depth - 1
        p = _layer_params(l, w_ada, w_in, w_q_b, w_uk, w_uv, w_mla_out, w_conv_out, rwkv_w2, rwkv_a2,
                          w_rwkv_out, w_out)
        vecs = dict(norm_g=norm_g[l], q_norm_g=q_norm_g[l], kv_norm_g=kv_norm_g[l], mu=rwkv_mu[l],
                    w0=rwkv_w0[l], a0=rwkv_a0[l], k_k=rwkv_k_k[l], k_a=rwkv_k_a[l],
                    r_k=rwkv_r_k[l].reshape(-1), conv_w=conv_w[l], gn_g=rwkv_gn_g[l], gn_b=rwkv_gn_b[l])
        mod = _ada_mod(c_all, p["w_ada"], b_ada[l])
        mod_p, mod_s = mod[:bp], mod[bp:]
        xp, ckv, kpe, cst, sst, rst = _layer(
            xp, mod_p, tabs_p, p, vecs,
            jnp.zeros((bp, CONV_W - 1, D_CONV), F32), jnp.zeros((bp, SHIFT_W), F32),
            jnp.zeros((bp, R_HEAD, D_R), F32), consts, final_norm_g, final, _attn_prompt, BF16)
        for lst, val in zip(st_p, (ckv, kpe, cst, sst, _state_from_lanes(rst))):
            lst.append(val)
        attend_s = functools.partial(_attn_sample_bound, page_table, cache_mla_ckv, cache_mla_kpe, l)
        xs, ckv, kpe, cst, sst, rst = _layer(
            xs, mod_s, tabs_s, p, vecs, state_conv[l], state_rwkv_shift[l],
            _state_to_lanes(state_rwkv[l]), consts, final_norm_g, final, attend_s, F32)
        for lst, val in zip(st_s, (ckv, kpe, cst, sst, _state_from_lanes(rst))):
            lst.append(val)
    return (xp, xs,
            jnp.stack(st_p[0]), jnp.stack(st_p[1]), jnp.stack(st_p[2]), jnp.stack(st_p[3]), jnp.stack(st_p[4]),
            jnp.stack(st_s[0]), jnp.stack(st_s[1]), jnp.stack(st_s[2]), jnp.stack(st_s[3]), jnp.stack(st_s[4]))


def _attn_sample_bound(page_table, cache_ckv, cache_kpe, layer, qlat, qpe, ckv_bf, kpe_bf, wuv_pad):
    return _attn_sample(page_table, qlat, qpe, ckv_bf, kpe_bf, wuv_pad, cache_ckv, cache_kpe, layer)
```
